```python
import math, functools
import jax, jax.numpy as jnp
from jax import lax
import numpy as np

D_MODEL = 4096
BATCH = 16
SEQ = 256
DEPTH = 2
DEC_BATCH = 4
DEC_SEQ = 1024
PAST_LEN = 512

GRID_W = 64
N_EVEN = (DEPTH + 1) // 2
N_ODD = DEPTH // 2
N_MOD = 9
D_FF = 11008
NORM_EPS = 1e-6
A_WIDTH = D_MODEL // 2
A_HEADS = 16
A_DK = A_WIDTH // A_HEADS
A_DV = A_WIDTH // A_HEADS
HG_CHUNK = 32
B_WIDTH = D_MODEL // 2
B_HEAD = 64
B_HEADS = B_WIDTH // B_HEAD
B_DECAY_LORA = max(32, int(round(1.8 * B_WIDTH ** 0.5 / 32)) * 32)
B_AAA_LORA = max(32, int(round(1.8 * B_WIDTH ** 0.5 / 32)) * 32)
B_GATE_LORA = max(32, int(round(0.6 * B_WIDTH ** 0.8 / 32)) * 32)
B_GN_EPS = 64e-5
EV_A_IN = 5 * A_WIDTH
EV_B_IN = 3 * B_WIDTH + 2 * B_DECAY_LORA + 2 * B_AAA_LORA + B_GATE_LORA
EV_IN = EV_A_IN + EV_B_IN
B_SPLITS = [int(s) for s in np.cumsum([B_WIDTH, B_WIDTH, B_WIDTH, B_DECAY_LORA, B_DECAY_LORA, B_AAA_LORA, B_AAA_LORA])]
C_HEAD_DIM = 128
C_HEADS = 16
C_KV_HEADS = 4
C_GROUP = C_HEADS // C_KV_HEADS
C_WIDTH = C_HEADS * C_HEAD_DIM
C_KV_WIDTH = C_KV_HEADS * C_HEAD_DIM
C_IN = C_WIDTH + 2 * C_KV_WIDTH
ROPE_AXIS_DIM = C_HEAD_DIM // 2
ROPE_THETA = 10000.0
Q_BLOCK = 128
HY_W = D_MODEL // 2
HY_ORDER = 2
HY_BANDS = 16
HY_EMB = 2 * HY_BANDS + 1
HY_FILTER_DIM = 64
HY_TARGET = 1e-2
HY_MIN_DECAY = math.log(HY_TARGET) / 1.5
HY_MAX_DECAY = math.log(HY_TARGET) / 0.3
OD_IN = C_IN + (HY_ORDER + 1) * HY_W
F32 = jnp.float32

kernel_name = 'hybrid_diffusion_prefix_trunk_step'


def rms_norm(x, g, eps=NORM_EPS):
    xf = x.astype(F32)
    y = xf * lax.rsqrt(jnp.mean(xf * xf, axis=-1, keepdims=True) + eps)
    return (y * g.astype(F32)).astype(x.dtype)


def ada_in(h, m, i, g):
    return rms_norm(h, g) * (1.0 + m[:, None, 3 * i + 1]) + m[:, None, 3 * i]


def ada_out(h, m, i, y, w):
    return h + w * m[:, None, 3 * i + 2] * y


def swiglu(x, w1, w3, w2):
    return (jax.nn.silu(x @ w1) * (x @ w3)) @ w2


def neighbours(u):
    up = jnp.pad(u, ((0, 0), (1, 1), (0, 0)))
    return up[:, :-2], up[:, 2:]


def short_conv3(u, w, b):
    prev, nxt = neighbours(u)
    return prev * w[0] + u * w[1] + nxt * w[2] + b


def flip(z):
    return jnp.flip(z, axis=1)


def hgrn2_chunked(q, k, v, logf, s0):
    B, L, H, DK = q.shape
    DV = v.shape[-1]
    n = L // HG_CHUNK

    def chunks(z):
        return z.astype(F32).reshape(B, n, HG_CHUNK, H, z.shape[-1]).transpose(1, 0, 3, 2, 4)

    qc, kc, vc = chunks(q), chunks(k), chunks(v)
    bc = jnp.cumsum(chunks(logf), axis=3)
    causal = jnp.tril(jnp.ones((HG_CHUNK, HG_CHUNK), bool))[:, :, None]

    def step(S, inp):
        qi, ki, vi, bi = inp
        o_inter = jnp.einsum('bhck,bhkv->bhcv', qi * jnp.exp(bi), S)
        rel = jnp.exp(jnp.where(causal, bi[:, :, :, None, :] - bi[:, :, None, :, :], -jnp.inf))
        att = jnp.einsum('bhik,bhjk,bhijk->bhij', qi, ki, rel)
        o = o_inter + jnp.einsum('bhij,bhjv->bhiv', att, vi)
        b_last = bi[:, :, -1:, :]
        S = jnp.exp(b_last[:, :, 0, :])[..., None] * S + jnp.einsum('bhck,bhcv->bhkv', ki * jnp.exp(b_last - bi), vi)
        return S, o

    S, o = lax.scan(step, s0.astype(F32), (qc, kc, vc, bc))
    return o.transpose(1, 0, 3, 2, 4).reshape(B, L, H, DV), S


def rwkv7_scan(r, logw, k, v, kk, a, s0):
    def step(S, inp):
        rt, lwt, kt, vt, kkt, at = inp
        removed = jnp.einsum('bhvk,bhk->bhv', S, kkt)
        S = (S * jnp.exp(lwt)[:, :, None, :] - removed[..., None] * (kkt * at)[:, :, None, :]
             + vt[..., None] * kt[:, :, None, :])
        return S, jnp.einsum('bhvk,bhk->bhv', S, rt)

    xs = tuple(jnp.swapaxes(z, 0, 1) for z in (r, logw, k, v, kk, a))
    S, o = lax.scan(step, s0.astype(F32), xs)
    return jnp.swapaxes(o, 0, 1), S


def even_mixer(xn, s_hf, s_hb, s_rf, s_rb, w_in, w_out, lb_f, lb_b, hg_g, mu,
               w0f, w2f, w0b, w2b, a0f, a2f, a0b, a2b, g2, k_k, k_a, r_k, ln_g, ln_b):
    B, L, _ = xn.shape
    z = xn @ w_in
    za, zb = z[..., :EV_A_IN], z[..., EV_A_IN:]

    def heads(t, d):
        return t.reshape(B, L, -1, d)

    hq, hff, hfb, hi, hg = jnp.split(za, 5, axis=-1)
    q = heads(jax.nn.silu(hq.astype(F32)), A_DK) * A_DK ** -0.5
    v = heads(hi.astype(F32), A_DV)

    def gate(fpre, lb):
        f = lb + (1.0 - lb) * jax.nn.sigmoid(fpre.astype(F32))
        return heads(jnp.log(f), A_DK), heads(1.0 - f, A_DK)

    lf_f, k_f = gate(hff, lb_f)
    lf_b, k_b = gate(hfb, lb_b)
    oa_f, sf = hgrn2_chunked(q, k_f, v, lf_f, s_hf)
    oa_b, sb = hgrn2_chunked(flip(q), flip(k_b), flip(v), flip(lf_b), s_hb)
    o_a = (oa_f + flip(oa_b)).reshape(B, L, A_WIDTH)
    o_a = rms_norm(o_a, hg_g) * jax.nn.silu(hg.astype(F32))

    prev, nxt = neighbours(zb)
    zb = zb + mu * (0.5 * (prev + nxt) - zb)
    rr, kr, vr, wdf, wdb, adf, adb, gd = jnp.split(zb, B_SPLITS, axis=-1)
    kr = kr.astype(F32)
    r = heads(rr.astype(F32), B_HEAD)
    k = heads(kr, B_HEAD)
    vv = heads(vr.astype(F32), B_HEAD)
    kk = heads(kr * k_k, B_HEAD)
    kk = kk / jnp.maximum(jnp.sqrt(jnp.sum(kk * kk, axis=-1, keepdims=True)), 1e-12)

    def direction(wd, w0, w2, ad, a0, a2):
        logw = -math.exp(-0.5) * jax.nn.sigmoid((w0 + jnp.tanh(wd) @ w2).astype(F32))
        a = jax.nn.sigmoid((a0 + ad @ a2).astype(F32))
        kd = kr * (1.0 + (a - 1.0) * k_a)
        return heads(logw, B_HEAD), heads(a, B_HEAD), heads(kd, B_HEAD)

    lw_f, a_f, kd_f = direction(wdf, w0f, w2f, adf, a0f, a2f)
    lw_b, a_b, kd_b = direction(wdb, w0b, w2b, adb, a0b, a2b)
    ob_f, rf = rwkv7_scan(r, lw_f, kd_f, vv, kk, a_f, s_rf)
    ob_b, rb = rwkv7_scan(flip(r), flip(lw_b), flip(kd_b), flip(vv), flip(kk), flip(a_b), s_rb)
    o = ob_f + flip(ob_b)
    mean = jnp.mean(o, axis=-1, keepdims=True)
    var = jnp.mean(jnp.square(o - mean), axis=-1, keepdims=True)
    o = ((o - mean) * lax.rsqrt(var + B_GN_EPS)).reshape(B, L, B_WIDTH) * ln_g + ln_b
    bonus = jnp.sum(r * k * r_k, axis=-1, keepdims=True) * vv
    g = jax.nn.sigmoid(gd) @ g2
    o_b = (o + bonus.reshape(B, L, B_WIDTH)) * g

    out = jnp.concatenate([o_a, o_b], axis=-1).astype(xn.dtype) @ w_out
    return out, (sf, sb, rf, rb)


def axial_rope(x):
    L = x.shape[1]
    n_rows = L // GRID_W
    row = jnp.repeat(jnp.arange(n_rows), GRID_W).astype(F32)
    col = jnp.tile(jnp.arange(GRID_W), n_rows).astype(F32)
    half = ROPE_AXIS_DIM // 2
    inv = ROPE_THETA ** (-jnp.arange(half, dtype=F32) / half)
    ang = jnp.stack([row[:, None] * inv, col[:, None] * inv], axis=1)
    cos = jnp.cos(ang)[None, :, None]
    sin = jnp.sin(ang)[None, :, None]
    xr = x.astype(F32).reshape(x.shape[:3] + (2, 2, half))
    x1, x2 = xr[..., 0, :], xr[..., 1, :]
    out = jnp.stack([x1 * cos - x2 * sin, x1 * sin + x2 * cos], axis=-2)
    return out.reshape(x.shape).astype(x.dtype)


def block_attention(q, k, v):
    B, L = q.shape[:2]
    nb = L // Q_BLOCK
    qb = jnp.moveaxis(q.reshape(B, nb, Q_BLOCK, C_KV_HEADS, C_GROUP, C_HEAD_DIM), 1, 0)
    scale = C_HEAD_DIM ** -0.5

    def one(qblk):
        s = jnp.einsum('bqhgd,bkhd->bhgqk', qblk, k).astype(F32) * scale
        p = jax.nn.softmax(s, axis=-1).astype(v.dtype)
        return jnp.einsum('bhgqk,bkhd->bqhgd', p, v)

    o = lax.map(one, qb)
    return jnp.moveaxis(o, 0, 1).reshape(B, L, C_WIDTH)


def hyena_filters(L, w1, b1, w2, b2, w3, b3, w4, freq):
    t = jnp.linspace(0.0, 1.0, L, dtype=F32)[:, None]
    w_pos = 2.0 * math.pi * jnp.arange(L, dtype=F32)[:, None] / L
    bands = jnp.linspace(1e-4, HY_BANDS - 1, HY_BANDS, dtype=F32)[None]
    zpos = jnp.concatenate([t, jnp.cos(bands * w_pos), -jnp.sin(bands * w_pos)], axis=-1)
    f = freq.astype(F32)
    hdn = jnp.sin(f * (zpos @ w1 + b1))
    hdn = jnp.sin(f * (hdn @ w2 + b2))
    hdn = jnp.sin(f * (hdn @ w3 + b3))
    h = (hdn @ w4).astype(F32)
    deltas = jnp.abs(jnp.linspace(HY_MIN_DECAY, HY_MAX_DECAY, HY_ORDER * HY_W, dtype=F32))
    h = h * jnp.exp(-t * deltas)
    return h.reshape(L, HY_ORDER, HY_W)


def centred_long_conv(u, h, bias):
    L = u.shape[1]
    k2 = jnp.concatenate([h, jnp.zeros((1, h.shape[1]), F32), h[:0:-1]], axis=0)
    uf = jnp.fft.rfft(u.astype(F32), n=2 * L, axis=1)
    kf = jnp.fft.rfft(k2, n=2 * L, axis=0)
    y = jnp.fft.irfft(uf * kf[None], n=2 * L, axis=1)[:, :L]
    return (y + u.astype(F32) * bias.astype(F32)).astype(u.dtype)


def odd_mixer(xn, ctx_k, ctx_v, w_in, w_out, qn_g, kn_g, conv_w, conv_b,
              f_w1, f_b1, f_w2, f_b2, f_w3, f_b3, f_w4, sin_freq, hy_bias):
    B, L, _ = xn.shape
    z = xn @ w_in
    q = z[..., :C_WIDTH].reshape(B, L, C_HEADS, C_HEAD_DIM)
    k = z[..., C_WIDTH:C_WIDTH + C_KV_WIDTH].reshape(B, L, C_KV_HEADS, C_HEAD_DIM)
    v = z[..., C_WIDTH + C_KV_WIDTH:C_IN].reshape(B, L, C_KV_HEADS, C_HEAD_DIM)
    q = rms_norm(q, qn_g)
    k = rms_norm(k, kn_g)
    if ctx_k is None:
        k_all, v_all = k, v
    else:
        q = axial_rope(q)
        k = axial_rope(k)
        k_all = jnp.concatenate([ctx_k.astype(k.dtype), k], axis=1)
        v_all = jnp.concatenate([ctx_v.astype(v.dtype), v], axis=1)
    o_c = block_attention(q, k_all, v_all)
    u = short_conv3(z[..., C_IN:], conv_w, conv_b)
    hv, x1, x2 = jnp.split(u, 3, axis=-1)
    filt = hyena_filters(L, f_w1, f_b1, f_w2, f_b2, f_w3, f_b3, f_w4, sin_freq)
    z1 = x1 * centred_long_conv(hv, filt[:, 0], hy_bias[0])
    o_d = x2 * centred_long_conv(z1, filt[:, 1], hy_bias[1])
    out = jnp.concatenate([o_c, o_d.astype(o_c.dtype)], axis=-1) @ w_out
    return out, k, v


def setup_inputs(seed: int = 0) -> dict:
    key = jax.random.key(seed)
    keys = iter(jax.random.split(key, 64))

    def nrm(shape, scale=1.0):
        return jax.random.normal(next(keys), shape, F32) * scale

    def gain(shape):
        return 1.0 + nrm(shape, 0.02)

    return {
        'x_prompt': nrm((BATCH, SEQ, D_MODEL)),
        'x_sample': nrm((DEC_BATCH, DEC_SEQ, D_MODEL)),
        'c': nrm((DEC_BATCH, D_MODEL)),
        'state_hgrn_fwd': nrm((DEC_BATCH, N_EVEN, A_HEADS, A_DK, A_DV), 0.5),
        'state_hgrn_bwd': nrm((DEC_BATCH, N_EVEN, A_HEADS, A_DK, A_DV), 0.5),
        'state_rwkv_fwd': nrm((DEC_BATCH, N_EVEN, B_HEADS, B_HEAD, B_HEAD), 0.5),
        'state_rwkv_bwd': nrm((DEC_BATCH, N_EVEN, B_HEADS, B_HEAD, B_HEAD), 0.5),
        'cache_k': nrm((DEC_BATCH, N_ODD, PAST_LEN, C_KV_HEADS, C_HEAD_DIM)),
        'cache_v': nrm((DEC_BATCH, N_ODD, PAST_LEN, C_KV_HEADS, C_HEAD_DIM)),
        'c_ctx': nrm((D_MODEL,)),
        'ada_w': nrm((DEPTH, D_MODEL, N_MOD * D_MODEL), 0.5 * D_MODEL ** -0.5),
        'ada_b': nrm((DEPTH, N_MOD * D_MODEL), 0.01),
        'norm_g': gain((DEPTH, 3, D_MODEL)),
        'ffn_w1': nrm((DEPTH, 2, D_MODEL, D_FF), D_MODEL ** -0.5),
        'ffn_w3': nrm((DEPTH, 2, D_MODEL, D_FF), D_MODEL ** -0.5),
        'ffn_w2': nrm((DEPTH, 2, D_FF, D_MODEL), D_FF ** -0.5),
        'final_norm_g': gain((D_MODEL,)),
        'ev_w_in': nrm((N_EVEN, D_MODEL, EV_IN), D_MODEL ** -0.5),
        'ev_w_out': nrm((N_EVEN, A_WIDTH + B_WIDTH, D_MODEL), (A_WIDTH + B_WIDTH) ** -0.5),
        'hg_lb_fwd': nrm((N_EVEN + 1, A_WIDTH)),
        'hg_lb_bwd': nrm((N_EVEN + 1, A_WIDTH)),
        'hg_norm_g': gain((N_EVEN, A_WIDTH)),
        'rw_mu': jax.random.uniform(next(keys), (N_EVEN, EV_B_IN), F32),
        'rw_w0_f': nrm((N_EVEN, B_WIDTH), 0.5),
        'rw_w2_f': nrm((N_EVEN, B_DECAY_LORA, B_WIDTH), B_DECAY_LORA ** -0.5),
        'rw_w0_b': nrm((N_EVEN, B_WIDTH), 0.5),
        'rw_w2_b': nrm((N_EVEN, B_DECAY_LORA, B_WIDTH), B_DECAY_LORA ** -0.5),
        'rw_a0_f': nrm((N_EVEN, B_WIDTH), 0.5),
        'rw_a2_f': nrm((N_EVEN, B_AAA_LORA, B_WIDTH), B_AAA_LORA ** -0.5),
        'rw_a0_b': nrm((N_EVEN, B_WIDTH), 0.5),
        'rw_a2_b': nrm((N_EVEN, B_AAA_LORA, B_WIDTH), B_AAA_LORA ** -0.5),
        'rw_g2': nrm((N_EVEN, B_GATE_LORA, B_WIDTH), B_GATE_LORA ** -0.5),
        'rw_kk': 0.85 + nrm((N_EVEN, B_WIDTH), 0.05),
        'rw_ka': 1.0 + nrm((N_EVEN, B_WIDTH), 0.05),
        'rw_rk': nrm((N_EVEN, B_HEADS, B_HEAD), 0.1),
        'rw_ln_g': gain((N_EVEN, B_WIDTH)),
        'rw_ln_b': nrm((N_EVEN, B_WIDTH), 0.01),
        'od_w_in': nrm((N_ODD, D_MODEL, OD_IN), D_MODEL ** -0.5),
        'od_w_out': nrm((N_ODD, C_WIDTH + HY_W, D_MODEL), (C_WIDTH + HY_W) ** -0.5),
        'at_qn_g': gain((N_ODD, C_HEAD_DIM)),
        'at_kn_g': gain((N_ODD, C_HEAD_DIM)),
        'hy_conv_w': nrm((N_ODD, 3, (HY_ORDER + 1) * HY_W), 3 ** -0.5),
        'hy_conv_b': nrm((N_ODD, (HY_ORDER + 1) * HY_W), 0.01),
        'hy_f_w1': nrm((N_ODD, HY_EMB, HY_FILTER_DIM), HY_EMB ** -0.5),
        'hy_f_b1': nrm((N_ODD, HY_FILTER_DIM), 0.1),
        'hy_f_w2': nrm((N_ODD, HY_FILTER_DIM, HY_FILTER_DIM), HY_FILTER_DIM ** -0.5),
        'hy_f_b2': nrm((N_ODD, HY_FILTER_DIM), 0.1),
        'hy_f_w3': nrm((N_ODD, HY_FILTER_DIM, HY_FILTER_DIM), HY_FILTER_DIM ** -0.5),
        'hy_f_b3': nrm((N_ODD, HY_FILTER_DIM), 0.1),
        'hy_f_w4': nrm((N_ODD, HY_FILTER_DIM, HY_ORDER * HY_W), 0.1 * HY_FILTER_DIM ** -0.5),
        'hy_sin_freq': 1.0 + nrm((N_ODD, HY_FILTER_DIM), 0.1),
        'hy_bias': nrm((N_ODD, HY_ORDER, HY_W), 0.1),
    }


def reference(x_prompt, x_sample, c, state_hgrn_fwd, state_hgrn_bwd, state_rwkv_fwd, state_rwkv_bwd,
              cache_k, cache_v, c_ctx, ada_w, ada_b, norm_g, ffn_w1, ffn_w3, ffn_w2, final_norm_g,
              ev_w_in, ev_w_out, hg_lb_fwd, hg_lb_bwd, hg_norm_g, rw_mu, rw_w0_f, rw_w2_f, rw_w0_b, rw_w2_b,
              rw_a0_f, rw_a2_f, rw_a0_b, rw_a2_b, rw_g2, rw_kk, rw_ka, rw_rk, rw_ln_g, rw_ln_b,
              od_w_in, od_w_out, at_qn_g, at_kn_g, hy_conv_w, hy_conv_b, hy_f_w1, hy_f_b1, hy_f_w2, hy_f_b2,
              hy_f_w3, hy_f_b3, hy_f_w4, hy_sin_freq, hy_bias):
    h_ctx, h_lat = x_prompt, x_sample
    n_ctx = x_prompt.shape[0]
    zeros_h = jnp.zeros((n_ctx, A_HEADS, A_DK, A_DV), F32)
    zeros_r = jnp.zeros((n_ctx, B_HEADS, B_HEAD, B_HEAD), F32)
    lb_fwd_all = jnp.cumsum(jax.nn.softmax(hg_lb_fwd.astype(F32), axis=0)[1:], axis=0)
    lb_bwd_all = jnp.cumsum(jax.nn.softmax(hg_lb_bwd.astype(F32), axis=0)[1:], axis=0)
    new_hf, new_hb, new_rf, new_rb, new_k, new_v = [], [], [], [], [], []
    for l in range(DEPTH):
        m_ctx = (jax.nn.silu(c_ctx) @ ada_w[l] + ada_b[l]).reshape(1, N_MOD, D_MODEL)
        m_lat = (jax.nn.silu(c) @ ada_w[l] + ada_b[l]).reshape(-1, N_MOD, D_MODEL)
        ffn_a = functools.partial(swiglu, w1=ffn_w1[l, 0], w3=ffn_w3[l, 0], w2=ffn_w2[l, 0])
        h_ctx = ada_out(h_ctx, m_ctx, 0, ffn_a(ada_in(h_ctx, m_ctx, 0, norm_g[l, 0])), 0.5)
        h_lat = ada_out(h_lat, m_lat, 0, ffn_a(ada_in(h_lat, m_lat, 0, norm_g[l, 0])), 0.5)
        j = l // 2
        if l % 2 == 0:
            mix = functools.partial(
                even_mixer, w_in=ev_w_in[j], w_out=ev_w_out[j], lb_f=lb_fwd_all[j], lb_b=lb_bwd_all[j],
                hg_g=hg_norm_g[j], mu=rw_mu[j], w0f=rw_w0_f[j], w2f=rw_w2_f[j], w0b=rw_w0_b[j], w2b=rw_w2_b[j],
                a0f=rw_a0_f[j], a2f=rw_a2_f[j], a0b=rw_a0_b[j], a2b=rw_a2_b[j], g2=rw_g2[j], k_k=rw_kk[j],
                k_a=rw_ka[j], r_k=rw_rk[j], ln_g=rw_ln_g[j], ln_b=rw_ln_b[j])
            o_ctx, (s_hf, s_hb, s_rf, s_rb) = mix(ada_in(h_ctx, m_ctx, 1, norm_g[l, 1]),
                                                  zeros_h, zeros_h, zeros_r, zeros_r)
            o_lat, _ = mix(ada_in(h_lat, m_lat, 1, norm_g[l, 1]), state_hgrn_fwd[:, j], state_hgrn_bwd[:, j],
                           state_rwkv_fwd[:, j], state_rwkv_bwd[:, j])
            new_hf.append(s_hf)
            new_hb.append(s_hb)
            new_rf.append(s_rf)
            new_rb.append(s_rb)
        else:
            mix = functools.partial(
                odd_mixer, w_in=od_w_in[j], w_out=od_w_out[j], qn_g=at_qn_g[j], kn_g=at_kn_g[j],
                conv_w=hy_conv_w[j], conv_b=hy_conv_b[j], f_w1=hy_f_w1[j], f_b1=hy_f_b1[j], f_w2=hy_f_w2[j],
                f_b2=hy_f_b2[j], f_w3=hy_f_w3[j], f_b3=hy_f_b3[j], f_w4=hy_f_w4[j], sin_freq=hy_sin_freq[j],
                hy_bias=hy_bias[j])
            o_ctx, k_ctx, v_ctx = mix(ada_in(h_ctx, m_ctx, 1, norm_g[l, 1]), None, None)
            o_lat, _, _ = mix(ada_in(h_lat, m_lat, 1, norm_g[l, 1]), cache_k[:, j], cache_v[:, j])
            new_k.append(k_ctx)
            new_v.append(v_ctx)
        h_ctx = ada_out(h_ctx, m_ctx, 1, o_ctx, 1.0)
        h_lat = ada_out(h_lat, m_lat, 1, o_lat, 1.0)
        ffn_b = functools.partial(swiglu, w1=ffn_w1[l, 1], w3=ffn_w3[l, 1], w2=ffn_w2[l, 1])
        h_ctx = ada_out(h_ctx, m_ctx, 2, ffn_b(ada_in(h_ctx, m_ctx, 2, norm_g[l, 2])), 0.5)
        h_lat = ada_out(h_lat, m_lat, 2, ffn_b(ada_in(h_lat, m_lat, 2, norm_g[l, 2])), 0.5)
    y_prompt = rms_norm(h_ctx, final_norm_g)
    y_sample = rms_norm(h_lat, final_norm_g)
    return (y_prompt, y_sample, jnp.stack(new_hf, axis=1), jnp.stack(new_hb, axis=1),
            jnp.stack(new_rf, axis=1), jnp.stack(new_rb, axis=1), jnp.stack(new_k, axis=1),
            jnp.stack(new_v, axis=1))
```

```python
import functools
import math

import numpy as np
import jax
import jax.numpy as jnp
from jax import lax
from jax.experimental import pallas as pl
from jax.experimental.pallas import tpu as pltpu

F32 = jnp.float32
BF16 = jnp.bfloat16

D_MODEL = 4096
DEPTH = 2
GRID_W = 64
N_MOD = 9
D_FF = 11008
NORM_EPS = 1e-6
A_WIDTH = D_MODEL // 2
A_HEADS = 16
A_DK = A_WIDTH // A_HEADS
A_DV = A_WIDTH // A_HEADS
B_WIDTH = D_MODEL // 2
B_HEAD = 64
B_HEADS = B_WIDTH // B_HEAD
B_DECAY_LORA = max(32, int(round(1.8 * B_WIDTH ** 0.5 / 32)) * 32)
B_AAA_LORA = max(32, int(round(1.8 * B_WIDTH ** 0.5 / 32)) * 32)
B_GATE_LORA = max(32, int(round(0.6 * B_WIDTH ** 0.8 / 32)) * 32)
B_GN_EPS = 64e-5
EV_A_IN = 5 * A_WIDTH
EV_B_IN = 3 * B_WIDTH + 2 * B_DECAY_LORA + 2 * B_AAA_LORA + B_GATE_LORA
B_SPLITS = [int(s) for s in np.cumsum([B_WIDTH, B_WIDTH, B_WIDTH, B_DECAY_LORA, B_DECAY_LORA,
                                       B_AAA_LORA, B_AAA_LORA])]
C_HEAD_DIM = 128
C_HEADS = 16
C_KV_HEADS = 4
C_GROUP = C_HEADS // C_KV_HEADS
C_WIDTH = C_HEADS * C_HEAD_DIM
C_KV_WIDTH = C_KV_HEADS * C_HEAD_DIM
C_IN = C_WIDTH + 2 * C_KV_WIDTH
ROPE_AXIS_DIM = C_HEAD_DIM // 2
ROPE_THETA = 10000.0
Q_BLOCK = 128
HY_W = D_MODEL // 2
HY_ORDER = 2
HY_BANDS = 16
HY_TARGET = 1e-2
HY_MIN_DECAY = math.log(HY_TARGET) / 1.5
HY_MAX_DECAY = math.log(HY_TARGET) / 0.3

V7X_VMEM_LIMIT_BYTES = 56 * 1024 * 1024
LANES = 128
SUBLANES = 8
REC_PASSES = (2, 2, 2)


def _params(*sem):
    return pltpu.CompilerParams(dimension_semantics=sem, vmem_limit_bytes=V7X_VMEM_LIMIT_BYTES)


def _mm_body(x_ref, w_ref, o_ref, *, nk):
    part = jnp.dot(x_ref[...], w_ref[...], preferred_element_type=F32)
    if nk == 1:
        o_ref[...] = part.astype(o_ref.dtype)
    else:
        k = pl.program_id(2)

        @pl.when(k == 0)
        def _():
            o_ref[...] = part

        @pl.when(k > 0)
        def _():
            o_ref[...] += part


def mm(x, w, *, bm, bn, bk=None, out_dtype=F32):
    M, K = x.shape
    N = w.shape[1]
    bk = K if bk is None else bk
    nk = K // bk
    assert M % bm == 0 and N % bn == 0 and K % bk == 0
    assert nk == 1 or out_dtype == F32
    return pl.pallas_call(
        functools.partial(_mm_body, nk=nk),
        out_shape=jax.ShapeDtypeStruct((M, N), out_dtype),
        grid=(M // bm, N // bn, nk),
        in_specs=[pl.BlockSpec((bm, bk), lambda i, j, k: (i, k)),
                  pl.BlockSpec((bk, bn), lambda i, j, k: (k, j))],
        out_specs=pl.BlockSpec((bm, bn), lambda i, j, k: (i, j)),
        compiler_params=_params("parallel", "parallel", "arbitrary"),
        name="mm",
    )(x, w)


def _pick_bn(n, cands=(1024, 896, 768, 640, 512, 384, 256, 128)):
    for c in cands:
        if n % c == 0:
            return c
    return n


def mm_auto(x, w, out_dtype=F32):
    M, K = x.shape
    N = w.shape[1]
    bm = min(M, 1024)
    return mm(x.astype(BF16), w.astype(BF16), bm=bm, bn=_pick_bn(N), out_dtype=out_dtype)


def _ffn_up_body(x_ref, w1_ref, w3_ref, o_ref):
    x = x_ref[...]
    a = jnp.dot(x, w1_ref[...], preferred_element_type=F32)
    b = jnp.dot(x, w3_ref[...], preferred_element_type=F32)
    o_ref[...] = (a * jax.nn.sigmoid(a) * b).astype(o_ref.dtype)


def ffn_up(x, w1, w3, *, bm=1024, bn=256):
    M, K = x.shape
    N = w1.shape[1]
    return pl.pallas_call(
        _ffn_up_body,
        out_shape=jax.ShapeDtypeStruct((M, N), BF16),
        grid=(M // bm, N // bn),
        in_specs=[pl.BlockSpec((bm, K), lambda i, j: (i, 0)),
                  pl.BlockSpec((K, bn), lambda i, j: (0, j)),
                  pl.BlockSpec((K, bn), lambda i, j: (0, j))],
        out_specs=pl.BlockSpec((bm, bn), lambda i, j: (i, j)),
        compiler_params=_params("parallel", "parallel"),
        name="ffn_up",
    )(x, w1, w3)


def _rec_body(*refs, n_row, delta, TB, P, passes, reverse):
    row_refs = refs[:n_row]
    v_ref, g_ref, h_ref, t0_ref, o_ref, tf_ref, s_ref, vb_ref, xo_ref = refs[n_row:]
    tb = pl.program_id(2)
    R = s_ref.shape[1]

    @pl.when(tb == 0)
    def _():
        s_ref[...] = t0_ref[0]

    G = g_ref[...][None]
    Hm = h_ref[...]

    def seg_sum(x, n_pass):
        x = x.reshape(P * R, LANES)
        hi = x.astype(BF16)
        acc = jnp.dot(hi, Hm, preferred_element_type=F32)
        rem = x
        for _ in range(n_pass - 1):
            rem = rem - hi.astype(F32)
            hi = rem.astype(BF16)
            acc = acc + jnp.dot(hi, Hm, preferred_element_type=F32)
        return acc.reshape(P, R, LANES)

    def unit_rows(tile, j):
        return jnp.stack([tile[j:j + 1, p * LANES:(p + 1) * LANES] for p in range(P)])

    sub_iota = lax.broadcasted_iota(jnp.int32, (SUBLANES, LANES), 0)

    def group(i, carry):
        gi = (TB // SUBLANES - 1 - i) if reverse else i
        t8 = pl.multiple_of(gi * SUBLANES, SUBLANES)
        tiles = [r[0, pl.ds(t8, SUBLANES), :] for r in row_refs]
        v_tile = v_ref[0, pl.ds(t8, SUBLANES), :]
        for j in range(SUBLANES):
            vb_ref[j] = seg_sum(G * unit_rows(v_tile, j), passes[0])
        S = s_ref[...]
        for jj in range(SUBLANES):
            j = SUBLANES - 1 - jj if reverse else jj
            rows = [unit_rows(tile, j) for tile in tiles]
            if delta:
                w, b, kd, r, kk = rows
                u = seg_sum(S * kk, passes[1])
                S = S * w - u * b + vb_ref[j] * kd
            else:
                f, r = rows
                S = S * f + vb_ref[j] * (1.0 - f)
            xo_ref[j] = S * r
        s_ref[...] = S
        o_tiles = [jnp.zeros((SUBLANES, LANES), F32) for _ in range(P)]
        for j in range(SUBLANES):
            orow = jnp.sum(G * seg_sum(xo_ref[j], passes[2]), axis=1, keepdims=True)
            o_tiles = [jnp.where(sub_iota == j, orow[p], o_tiles[p]) for p in range(P)]
        for p in range(P):
            o_ref[0, pl.ds(t8, SUBLANES), p * LANES:(p + 1) * LANES] = o_tiles[p]
        return carry

    lax.fori_loop(0, TB // SUBLANES, group, 0)

    @pl.when(tb == pl.num_programs(2) - 1)
    def _():
        tf_ref[0] = s_ref[...]


def linear_recurrence(rows, v, s0, *, delta, seg, reverse, passes, units_per_step=8, time_block=128):
    B, L, W = v.shape
    U = W // LANES
    P = units_per_step
    TB = time_block
    nT = L // TB
    assert U % P == 0 and L % TB == 0 and s0.shape == (B, U, seg, LANES)
    lane = jnp.arange(LANES)
    g = (lane[None, :] % seg == jnp.arange(seg)[:, None]).astype(F32)
    hm = (lane[:, None] // seg == lane[None, :] // seg).astype(BF16)
    body = functools.partial(_rec_body, n_row=len(rows), delta=delta, TB=TB, P=P, passes=passes, reverse=reverse)
    if reverse:
        row_spec = pl.BlockSpec((1, TB, P * LANES), lambda b, u, t: (b, nT - 1 - t, u))
    else:
        row_spec = pl.BlockSpec((1, TB, P * LANES), lambda b, u, t: (b, t, u))
    st_spec = pl.BlockSpec((1, P, seg, LANES), lambda b, u, t: (b, u, 0, 0))
    return pl.pallas_call(
        body,
        out_shape=(jax.ShapeDtypeStruct((B, L, W), F32), jax.ShapeDtypeStruct(s0.shape, F32)),
        grid=(B, U // P, nT),
        in_specs=[row_spec] * (len(rows) + 1) + [pl.BlockSpec((seg, LANES), lambda b, u, t: (0, 0)),
                                                 pl.BlockSpec((LANES, LANES), lambda b, u, t: (0, 0)), st_spec],
        out_specs=(row_spec, st_spec),
        scratch_shapes=[pltpu.VMEM((P, seg, LANES), F32), pltpu.VMEM((SUBLANES, P, seg, LANES), F32),
                        pltpu.VMEM((SUBLANES, P, seg, LANES), F32)],
        compiler_params=_params("parallel", "parallel", "arbitrary"),
        name="rec_delta" if delta else "rec_gated",
    )(*rows, v, g, hm, s0)


def _attn_body(q_ref, k_ref, v_ref, o_ref):
    k = k_ref[0]
    v = v_ref[0]
    scale = C_HEAD_DIM ** -0.5
    for g in range(C_GROUP):
        q = q_ref[0, :, g * C_HEAD_DIM:(g + 1) * C_HEAD_DIM]
        s = lax.dot_general(q, k, (((1,), (1,)), ((), ())), preferred_element_type=F32) * scale
        m = jnp.max(s, axis=-1, keepdims=True)
        e = jnp.exp(s - m)
        p = e / jnp.sum(e, axis=-1, keepdims=True)
        o = jnp.dot(p.astype(BF16), v, preferred_element_type=F32)
        o_ref[0, :, g * C_HEAD_DIM:(g + 1) * C_HEAD_DIM] = o


def block_attention(q, k, v):
    B, L, _ = q.shape
    Lk = k.shape[1]
    gw = C_GROUP * C_HEAD_DIM
    return pl.pallas_call(
        _attn_body,
        out_shape=jax.ShapeDtypeStruct((B, L, C_WIDTH), F32),
        grid=(B, C_KV_HEADS, L // Q_BLOCK),
        in_specs=[pl.BlockSpec((1, Q_BLOCK, gw), lambda b, h, i: (b, i, h)),
                  pl.BlockSpec((1, Lk, C_HEAD_DIM), lambda b, h, i: (b, 0, h)),
                  pl.BlockSpec((1, Lk, C_HEAD_DIM), lambda b, h, i: (b, 0, h))],
        out_specs=pl.BlockSpec((1, Q_BLOCK, gw), lambda b, h, i: (b, i, h)),
        compiler_params=_params("parallel", "parallel", "parallel"),
        name="attn",
    )(q, k, v)


def _dft_body(a_ref, u_ref, o_ref):
    o_ref[0] = jnp.dot(a_ref[...], u_ref[0], preferred_element_type=F32,
                       precision=lax.Precision.HIGHEST)


def dft_mm(a, u, *, bn=512):
    M, K = a.shape
    B, _, C = u.shape
    return pl.pallas_call(
        _dft_body,
        out_shape=jax.ShapeDtypeStruct((B, M, C), F32),
        grid=(B, C // bn),
        in_specs=[pl.BlockSpec((M, K), lambda b, j: (0, 0)),
                  pl.BlockSpec((1, K, bn), lambda b, j: (b, 0, j))],
        out_specs=pl.BlockSpec((1, M, bn), lambda b, j: (b, 0, j)),
        compiler_params=_params("parallel", "parallel"),
        name="dft_mm",
    )(a, u)


def _dft_matrix(L):
    N = 2 * L
    r = jnp.arange(N, dtype=jnp.int32)[:, None]
    s = jnp.arange(L, dtype=jnp.int32)[None, :]
    is_cos = r <= L
    f = jnp.where(is_cos, r, r - L)
    ang = (2.0 * math.pi / N) * ((f * s) % N).astype(F32)
    return jnp.where(is_cos, jnp.cos(ang), jnp.sin(ang))


def _centred_long_conv(u, h, bias, fmat, fmat_t):
    L = u.shape[1]
    N = 2 * L
    h_mod = jnp.concatenate([h[:1], 2.0 * h[1:]], axis=0)
    kf = dft_mm(fmat, h_mod[None])[0]
    r = jnp.arange(N)
    f = jnp.where(r <= L, r, r - L)
    herm = jnp.where((r == 0) | (r == L), 1.0, 2.0).astype(F32) / N
    scale = kf[f] * herm[:, None]
    uf = dft_mm(fmat, u)
    y = dft_mm(fmat_t, uf * scale[None])
    return y + u * bias


def _hyena_filters(L, w1, b1, w2, b2, w3, b3, w4, freq):
    t = jnp.linspace(0.0, 1.0, L, dtype=F32)[:, None]
    w_pos = 2.0 * math.pi * jnp.arange(L, dtype=F32)[:, None] / L
    bands = jnp.linspace(1e-4, HY_BANDS - 1, HY_BANDS, dtype=F32)[None]
    zpos = jnp.concatenate([t, jnp.cos(bands * w_pos), -jnp.sin(bands * w_pos)], axis=-1)
    f = freq.astype(F32)
    hdn = jnp.sin(f * (zpos @ w1 + b1))
    hdn = jnp.sin(f * (hdn @ w2 + b2))
    hdn = jnp.sin(f * (hdn @ w3 + b3))
    h = mm_auto(hdn, w4)
    deltas = jnp.abs(jnp.linspace(HY_MIN_DECAY, HY_MAX_DECAY, HY_ORDER * HY_W, dtype=F32))
    h = h * jnp.exp(-t * deltas)
    return h.reshape(L, HY_ORDER, HY_W)


def _rms(x, g, eps=NORM_EPS):
    return x * lax.rsqrt(jnp.mean(x * x, axis=-1, keepdims=True) + eps) * g


def _neighbours(u):
    up = jnp.pad(u, ((0, 0), (1, 1), (0, 0)))
    return up[:, :-2], up[:, 2:]


def _mm3(x, w):
    B, L, K = x.shape
    return mm_auto(x.reshape(B * L, K), w).reshape(B, L, -1)


def _even_core(z, s_hf, s_hb, s_rf, s_rb, p):
    B, L, _ = z.shape
    za, zb = z[..., :EV_A_IN], z[..., EV_A_IN:]
    hq, hff, hfb, hi, hg = jnp.split(za, 5, axis=-1)
    q = jax.nn.silu(hq) * A_DK ** -0.5

    def hgrn(fpre, lb, s0, rev):
        f = lb + (1.0 - lb) * jax.nn.sigmoid(fpre)
        o, s = linear_recurrence([f, q], hi, jnp.swapaxes(s0, -1, -2), delta=False, seg=A_DK, reverse=rev,
                                 passes=REC_PASSES)
        return o, jnp.swapaxes(s, -1, -2)

    oa_f, sf = hgrn(hff, p['lb_f'], s_hf, False)
    oa_b, sb = hgrn(hfb, p['lb_b'], s_hb, True)
    o_a = _rms(oa_f + oa_b, p['hg_g']) * jax.nn.silu(hg)

    prev, nxt = _neighbours(zb)
    zb = zb + p['mu'] * (0.5 * (prev + nxt) - zb)
    rr, kr, vr, wdf, wdb, adf, adb, gd = jnp.split(zb, B_SPLITS, axis=-1)
    kk = (kr * p['k_k']).reshape(B, L, B_HEADS, B_HEAD)
    kk = kk / jnp.maximum(jnp.sqrt(jnp.sum(kk * kk, axis=-1, keepdims=True)), 1e-12)
    kk = kk.reshape(B, L, B_WIDTH)
    n_pair = B_HEADS // 2

    def direction(wd, w0, w2, ad, a0, a2, s0, rev):
        logw = -math.exp(-0.5) * jax.nn.sigmoid(w0 + _mm3(jnp.tanh(wd), w2))
        a = jax.nn.sigmoid(a0 + _mm3(ad, a2))
        kd = kr * (1.0 + (a - 1.0) * p['k_a'])
        t0 = s0.reshape(B, n_pair, 2, B_HEAD, B_HEAD).transpose(0, 1, 3, 2, 4).reshape(B, n_pair, B_HEAD, 2 * B_HEAD)
        o, s = linear_recurrence([jnp.exp(logw), kk * a, kd, rr, kk], vr, t0, delta=True, seg=B_HEAD, reverse=rev,
                                 passes=REC_PASSES)
        s = s.reshape(B, n_pair, B_HEAD, 2, B_HEAD).transpose(0, 1, 3, 2, 4).reshape(B, B_HEADS, B_HEAD, B_HEAD)
        return o, s

    ob_f, rf = direction(wdf, p['w0f'], p['w2f'], adf, p['a0f'], p['a2f'], s_rf, False)
    ob_b, rb = direction(wdb, p['w0b'], p['w2b'], adb, p['a0b'], p['a2b'], s_rb, True)
    o = (ob_f + ob_b).reshape(B, L, B_HEADS, B_HEAD)
    mean = jnp.mean(o, axis=-1, keepdims=True)
    var = jnp.mean(jnp.square(o - mean), axis=-1, keepdims=True)
    o = ((o - mean) * lax.rsqrt(var + B_GN_EPS)).reshape(B, L, B_WIDTH) * p['ln_g'] + p['ln_b']
    r4 = rr.reshape(B, L, B_HEADS, B_HEAD)
    bonus = jnp.sum(r4 * kr.reshape(r4.shape) * p['r_k'], axis=-1, keepdims=True) * vr.reshape(r4.shape)
    g = _mm3(jax.nn.sigmoid(gd), p['g2'])
    o_b = (o + bonus.reshape(B, L, B_WIDTH)) * g
    return jnp.concatenate([o_a, o_b], axis=-1), (sf, sb, rf, rb)


def _axial_rope(x):
    L = x.shape[1]
    n_rows = L // GRID_W
    row = jnp.repeat(jnp.arange(n_rows), GRID_W).astype(F32)
    col = jnp.tile(jnp.arange(GRID_W), n_rows).astype(F32)
    half = ROPE_AXIS_DIM // 2
    inv = ROPE_THETA ** (-jnp.arange(half, dtype=F32) / half)
    ang = jnp.stack([row[:, None] * inv, col[:, None] * inv], axis=1)
    cos = jnp.cos(ang)[None, :, None]
    sin = jnp.sin(ang)[None, :, None]
    xr = x.reshape(x.shape[:3] + (2, 2, half))
    x1, x2 = xr[..., 0, :], xr[..., 1, :]
    out = jnp.stack([x1 * cos - x2 * sin, x1 * sin + x2 * cos], axis=-2)
    return out.reshape(x.shape)


def _odd_core(z, ctx_k, ctx_v, p):
    B, L, _ = z.shape
    q = _rms(z[..., :C_WIDTH].reshape(B, L, C_HEADS, C_HEAD_DIM), p['qn_g'])
    k = _rms(z[..., C_WIDTH:C_WIDTH + C_KV_WIDTH].reshape(B, L, C_KV_HEADS, C_HEAD_DIM), p['kn_g'])
    v = z[..., C_WIDTH + C_KV_WIDTH:C_IN].reshape(B, L, C_KV_HEADS, C_HEAD_DIM)
    if ctx_k is None:
        k_all, v_all = k, v
    else:
        q = _axial_rope(q)
        k = _axial_rope(k)
        k_all = jnp.concatenate([ctx_k, k], axis=1)
        v_all = jnp.concatenate([ctx_v, v], axis=1)
    Lk = k_all.shape[1]
    o_c = block_attention(q.reshape(B, L, C_WIDTH).astype(BF16), k_all.reshape(B, Lk, C_KV_WIDTH).astype(BF16),
                          v_all.reshape(B, Lk, C_KV_WIDTH).astype(BF16))
    u = z[..., C_IN:]
    prev, nxt = _neighbours(u)
    u = prev * p['conv_w'][0] + u * p['conv_w'][1] + nxt * p['conv_w'][2] + p['conv_b']
    hv, x1, x2 = jnp.split(u, 3, axis=-1)
    filt = _hyena_filters(L, p['f_w1'], p['f_b1'], p['f_w2'], p['f_b2'], p['f_w3'], p['f_b3'], p['f_w4'],
                          p['sin_freq'])
    fmat = _dft_matrix(L)
    fmat_t = fmat.T
    z1 = x1 * _centred_long_conv(hv, filt[:, 0], p['hy_bias'][0], fmat, fmat_t)
    o_d = x2 * _centred_long_conv(z1, filt[:, 1], p['hy_bias'][1], fmat, fmat_t)
    return jnp.concatenate([o_c, o_d], axis=-1), k, v


def kernel(x_prompt, x_sample, c, state_hgrn_fwd, state_hgrn_bwd, state_rwkv_fwd, state_rwkv_bwd, cache_k, cache_v, c_ctx, ada_w, ada_b, norm_g, ffn_w1, ffn_w3, ffn_w2, final_norm_g, ev_w_in, ev_w_out, hg_lb_fwd, hg_lb_bwd, hg_norm_g, rw_mu, rw_w0_f, rw_w2_f, rw_w0_b, rw_w2_b, rw_a0_f, rw_a2_f, rw_a0_b, rw_a2_b, rw_g2, rw_kk, rw_ka, rw_rk, rw_ln_g, rw_ln_b, od_w_in, od_w_out, at_qn_g, at_kn_g, hy_conv_w, hy_conv_b, hy_f_w1, hy_f_b1, hy_f_w2, hy_f_b2, hy_f_w3, hy_f_b3, hy_f_w4, hy_sin_freq, hy_bias):
    Bc, Lc, _ = x_prompt.shape
    Bl, Ll, _ = x_sample.shape
    n_ctx_tok = Bc * Lc
    n_lat_tok = Bl * Ll
    n_tok = n_ctx_tok + n_lat_tok
    grp = Ll
    assert n_ctx_tok % grp == 0
    n_grp = n_tok // grp
    grp_mod = jnp.array([0] * (n_ctx_tok // grp) + [1 + b for b in range(Bl)], jnp.int32)

    h = jnp.concatenate([x_prompt.reshape(n_ctx_tok, D_MODEL), x_sample.reshape(n_lat_tok, D_MODEL)], axis=0)
    h = h.reshape(n_grp, grp, D_MODEL)
    cvec = jnp.concatenate([c_ctx[None], c], axis=0)
    cvec = jnp.pad(jax.nn.silu(cvec), ((0, 8 - (1 + Bl)), (0, 0)))

    zeros_h = jnp.zeros((Bc, A_HEADS, A_DK, A_DV), F32)
    zeros_r = jnp.zeros((Bc, B_HEADS, B_HEAD, B_HEAD), F32)
    lb_fwd_all = jnp.cumsum(jax.nn.softmax(hg_lb_fwd, axis=0)[1:], axis=0)
    lb_bwd_all = jnp.cumsum(jax.nn.softmax(hg_lb_bwd, axis=0)[1:], axis=0)

    def ada_in(h, m, i, g):
        return (_rms(h, g) * (1.0 + m[:, None, 3 * i + 1]) + m[:, None, 3 * i]).astype(BF16).reshape(n_tok, D_MODEL)

    def ada_out(h, m, i, y, w):
        return h + w * m[:, None, 3 * i + 2] * y.reshape(h.shape)

    def ffn(x, l, s):
        mid = ffn_up(x, ffn_w1[l, s].astype(BF16), ffn_w3[l, s].astype(BF16))
        return mm(mid, ffn_w2[l, s].astype(BF16), bm=1024, bn=512, bk=D_FF // 2)

    new_hf, new_hb, new_rf, new_rb, new_k, new_v = [], [], [], [], [], []
    for l in range(DEPTH):
        m = mm(cvec.astype(BF16), ada_w[l].astype(BF16), bm=8, bn=1024) + ada_b[l]
        m = m[:1 + Bl].reshape(1 + Bl, N_MOD, D_MODEL)[grp_mod]
        h = ada_out(h, m, 0, ffn(ada_in(h, m, 0, norm_g[l, 0]), l, 0), 0.5)
        xn = ada_in(h, m, 1, norm_g[l, 1])
        j = l // 2
        if l % 2 == 0:
            p = dict(lb_f=lb_fwd_all[j], lb_b=lb_bwd_all[j], hg_g=hg_norm_g[j], mu=rw_mu[j],
                     w0f=rw_w0_f[j], w2f=rw_w2_f[j], w0b=rw_w0_b[j], w2b=rw_w2_b[j],
                     a0f=rw_a0_f[j], a2f=rw_a2_f[j], a0b=rw_a0_b[j], a2b=rw_a2_b[j], g2=rw_g2[j],
                     k_k=rw_kk[j], k_a=rw_ka[j], r_k=rw_rk[j], ln_g=rw_ln_g[j], ln_b=rw_ln_b[j])
            z = mm_auto(xn, ev_w_in[j])
            o_ctx, (s_hf, s_hb, s_rf, s_rb) = _even_core(z[:n_ctx_tok].reshape(Bc, Lc, -1),
                                                         zeros_h, zeros_h, zeros_r, zeros_r, p)
            o_lat, _ = _even_core(z[n_ctx_tok:].reshape(Bl, Ll, -1), state_hgrn_fwd[:, j], state_hgrn_bwd[:, j],
                                  state_rwkv_fwd[:, j], state_rwkv_bwd[:, j], p)
            new_hf.append(s_hf)
            new_hb.append(s_hb)
            new_rf.append(s_rf)
            new_rb.append(s_rb)
            w_out = ev_w_out[j]
        else:
            p = dict(qn_g=at_qn_g[j], kn_g=at_kn_g[j], conv_w=hy_conv_w[j], conv_b=hy_conv_b[j],
                     f_w1=hy_f_w1[j], f_b1=hy_f_b1[j], f_w2=hy_f_w2[j], f_b2=hy_f_b2[j], f_w3=hy_f_w3[j],
                     f_b3=hy_f_b3[j], f_w4=hy_f_w4[j], sin_freq=hy_sin_freq[j], hy_bias=hy_bias[j])
            z = mm_auto(xn, od_w_in[j])
            o_ctx, k_ctx, v_ctx = _odd_core(z[:n_ctx_tok].reshape(Bc, Lc, -1), None, None, p)
            o_lat, _, _ = _odd_core(z[n_ctx_tok:].reshape(Bl, Ll, -1), cache_k[:, j], cache_v[:, j], p)
            new_k.append(k_ctx)
            new_v.append(v_ctx)
            w_out = od_w_out[j]
        o = jnp.concatenate([o_ctx.reshape(n_ctx_tok, -1), o_lat.reshape(n_lat_tok, -1)], axis=0)
        h = ada_out(h, m, 1, mm_auto(o, w_out), 1.0)
        h = ada_out(h, m, 2, ffn(ada_in(h, m, 2, norm_g[l, 2]), l, 1), 0.5)
    y = _rms(h, final_norm_g).reshape(n_tok, D_MODEL)
    return (y[:n_ctx_tok].reshape(Bc, Lc, D_MODEL), y[n_ctx_tok:].reshape(Bl, Ll, D_MODEL),
            jnp.stack(new_hf, axis=1), jnp.stack(new_hb, axis=1), jnp.stack(new_rf, axis=1),
            jnp.stack(new_rb, axis=1), jnp.stack(new_k, axis=1), jnp.stack(new_v, axis=1))
```

```python
import functools
import math

import numpy as np
import jax
import jax.numpy as jnp
from jax import lax
from jax.experimental import pallas as pl
from jax.experimental.pallas import tpu as pltpu

F32 = jnp.float32
BF16 = jnp.bfloat16

D_MODEL = 4096
DEPTH = 2
GRID_W = 64
N_MOD = 9
D_FF = 11008
NORM_EPS = 1e-6
A_WIDTH = D_MODEL // 2
A_HEADS = 16
A_DK = A_WIDTH // A_HEADS
A_DV = A_WIDTH // A_HEADS
B_WIDTH = D_MODEL // 2
B_HEAD = 64
B_HEADS = B_WIDTH // B_HEAD
B_DECAY_LORA = max(32, int(round(1.8 * B_WIDTH ** 0.5 / 32)) * 32)
B_AAA_LORA = max(32, int(round(1.8 * B_WIDTH ** 0.5 / 32)) * 32)
B_GATE_LORA = max(32, int(round(0.6 * B_WIDTH ** 0.8 / 32)) * 32)
B_GN_EPS = 64e-5
EV_A_IN = 5 * A_WIDTH
EV_B_IN = 3 * B_WIDTH + 2 * B_DECAY_LORA + 2 * B_AAA_LORA + B_GATE_LORA
B_SPLITS = [int(s) for s in np.cumsum([B_WIDTH, B_WIDTH, B_WIDTH, B_DECAY_LORA, B_DECAY_LORA,
                                       B_AAA_LORA, B_AAA_LORA])]
C_HEAD_DIM = 128
C_HEADS = 16
C_KV_HEADS = 4
C_GROUP = C_HEADS // C_KV_HEADS
C_WIDTH = C_HEADS * C_HEAD_DIM
C_KV_WIDTH = C_KV_HEADS * C_HEAD_DIM
C_IN = C_WIDTH + 2 * C_KV_WIDTH
ROPE_AXIS_DIM = C_HEAD_DIM // 2
ROPE_THETA = 10000.0
Q_BLOCK = 128
HY_W = D_MODEL // 2
HY_ORDER = 2
HY_BANDS = 16
HY_TARGET = 1e-2
HY_MIN_DECAY = math.log(HY_TARGET) / 1.5
HY_MAX_DECAY = math.log(HY_TARGET) / 0.3

V7X_VMEM_LIMIT_BYTES = 56 * 1024 * 1024
LANES = 128
SUBLANES = 8
SEQ_BLOCK = 256
REC_UNIT = 128
REC_PASSES = (1, 2, 1)


def _params(*sem):
    return pltpu.CompilerParams(dimension_semantics=sem, vmem_limit_bytes=V7X_VMEM_LIMIT_BYTES)


def _mm_body(x_ref, w_ref, o_ref, *, nk):
    part = jnp.dot(x_ref[...], w_ref[...], preferred_element_type=F32)
    if nk == 1:
        o_ref[...] = part.astype(o_ref.dtype)
    else:
        k = pl.program_id(2)

        @pl.when(k == 0)
        def _():
            o_ref[...] = part

        @pl.when(k > 0)
        def _():
            o_ref[...] += part


def mm(x, w, *, bm, bn, bk=None, out_dtype=F32):
    M, K = x.shape
    N = w.shape[1]
    bk = K if bk is None else bk
    nk = K // bk
    assert M % bm == 0 and N % bn == 0 and K % bk == 0
    assert nk == 1 or out_dtype == F32
    return pl.pallas_call(
        functools.partial(_mm_body, nk=nk),
        out_shape=jax.ShapeDtypeStruct((M, N), out_dtype),
        grid=(M // bm, N // bn, nk),
        in_specs=[pl.BlockSpec((bm, bk), lambda i, j, k: (i, k)),
                  pl.BlockSpec((bk, bn), lambda i, j, k: (k, j))],
        out_specs=pl.BlockSpec((bm, bn), lambda i, j, k: (i, j)),
        compiler_params=_params("parallel", "parallel", "arbitrary"),
        name="mm",
    )(x, w)


def _pick_bn(n, cands=(1024, 896, 768, 640, 512, 384, 256, 128)):
    for c in cands:
        if n % c == 0:
            return c
    return n


def mm_auto(x, w, out_dtype=F32):
    M, K = x.shape
    N = w.shape[1]
    bm = min(M, 1024)
    return mm(x.astype(BF16), w.astype(BF16), bm=bm, bn=_pick_bn(N), out_dtype=out_dtype)


def _mm_ws_body(x_ref, w_ref, o_ref, wb_ref):
    @pl.when(pl.program_id(1) == 0)
    def _():
        wb_ref[...] = w_ref[...].astype(BF16)

    o_ref[...] = jnp.dot(x_ref[...], wb_ref[...], preferred_element_type=F32).astype(o_ref.dtype)


def mm_ws(x, w, *, bm, bn, out_dtype=F32):
    M, K = x.shape
    N = w.shape[1]
    assert M % bm == 0 and N % bn == 0
    return pl.pallas_call(
        _mm_ws_body,
        out_shape=jax.ShapeDtypeStruct((M, N), out_dtype),
        grid=(N // bn, M // bm),
        in_specs=[pl.BlockSpec((bm, K), lambda j, i: (i, 0)),
                  pl.BlockSpec((K, bn), lambda j, i: (0, j))],
        out_specs=pl.BlockSpec((bm, bn), lambda j, i: (i, j)),
        scratch_shapes=[pltpu.VMEM((K, bn), BF16)],
        compiler_params=_params("arbitrary", "arbitrary"),
        name="mm_ws",
    )(x, w)


def _mm_small_body(x_ref, w_ref, o_ref):
    o_ref[...] = jnp.dot(x_ref[...], w_ref[...].astype(BF16), preferred_element_type=F32)


def mm_small(x, w, *, bn):
    M, K = x.shape
    N = w.shape[1]
    return pl.pallas_call(
        _mm_small_body,
        out_shape=jax.ShapeDtypeStruct((M, N), F32),
        grid=(N // bn,),
        in_specs=[pl.BlockSpec((M, K), lambda j: (0, 0)),
                  pl.BlockSpec((K, bn), lambda j: (0, j))],
        out_specs=pl.BlockSpec((M, bn), lambda j: (0, j)),
        compiler_params=_params("parallel"),
        name="mm_small",
    )(x, w)


def _ffn_up_body(x_ref, w1_ref, w3_ref, o_ref, w1b_ref, w3b_ref):
    @pl.when(pl.program_id(1) == 0)
    def _():
        w1b_ref[...] = w1_ref[...].astype(BF16)
        w3b_ref[...] = w3_ref[...].astype(BF16)

    x = x_ref[...]
    a = jnp.dot(x, w1b_ref[...], preferred_element_type=F32)
    b = jnp.dot(x, w3b_ref[...], preferred_element_type=F32)
    o_ref[...] = (a * jax.nn.sigmoid(a) * b).astype(o_ref.dtype)


def ffn_up(x, w1, w3, *, bm=1024, bn=256):
    M, K = x.shape
    N = w1.shape[1]
    return pl.pallas_call(
        _ffn_up_body,
        out_shape=jax.ShapeDtypeStruct((M, N), BF16),
        grid=(N // bn, M // bm),
        in_specs=[pl.BlockSpec((bm, K), lambda j, i: (i, 0)),
                  pl.BlockSpec((K, bn), lambda j, i: (0, j)),
                  pl.BlockSpec((K, bn), lambda j, i: (0, j))],
        out_specs=pl.BlockSpec((bm, bn), lambda j, i: (i, j)),
        scratch_shapes=[pltpu.VMEM((K, bn), BF16), pltpu.VMEM((K, bn), BF16)],
        compiler_params=_params("arbitrary", "arbitrary"),
        name="ffn_up",
    )(x, w1, w3)


def _rwkv_body(w_ref, b_ref, kd_ref, r_ref, kk_ref, v_ref, g_ref, h_ref, t0_ref, o_ref, tf_ref,
               s_ref, vb_ref, xo_ref, *, TB, P, passes, reverse):
    row_refs = (w_ref, b_ref, kd_ref, r_ref, kk_ref)
    tb = pl.program_id(2)
    R = s_ref.shape[1]
    UW = REC_UNIT

    @pl.when(tb == 0)
    def _():
        s_ref[...] = t0_ref[0]

    G = g_ref[...][None]
    Hm = h_ref[...]

    def seg_sum(x, n_pass):
        x = x.reshape(P * R, UW)
        hi = x.astype(BF16)
        acc = jnp.dot(hi, Hm, preferred_element_type=F32)
        rem = x
        for _ in range(n_pass - 1):
            rem = rem - hi.astype(F32)
            hi = rem.astype(BF16)
            acc = acc + jnp.dot(hi, Hm, preferred_element_type=F32)
        return acc.reshape(P, R, UW)

    def unit_rows(tile, j):
        return jnp.stack([tile[j:j + 1, p * UW:(p + 1) * UW] for p in range(P)])

    sub_iota = lax.broadcasted_iota(jnp.int32, (SUBLANES, UW), 0)

    def group(i, carry):
        gi = (TB // SUBLANES - 1 - i) if reverse else i
        t8 = pl.multiple_of(gi * SUBLANES, SUBLANES)
        tiles = [r[0, pl.ds(t8, SUBLANES), :] for r in row_refs]
        v_tile = v_ref[0, pl.ds(t8, SUBLANES), :]
        for j in range(SUBLANES):
            vb_ref[j] = seg_sum(G * unit_rows(v_tile, j), passes[0])
        S = s_ref[...]
        for jj in range(SUBLANES):
            j = SUBLANES - 1 - jj if reverse else jj
            w, b, kd, r, kk = [unit_rows(tile, j) for tile in tiles]
            u = seg_sum(S * kk, passes[1])
            S = S * w - u * b + vb_ref[j] * kd
            xo_ref[j] = S * r
        s_ref[...] = S
        o_tiles = [jnp.zeros((SUBLANES, UW), F32) for _ in range(P)]
        for j in range(SUBLANES):
            orow = jnp.sum(G * seg_sum(xo_ref[j], passes[2]), axis=1, keepdims=True)
            o_tiles = [jnp.where(sub_iota == j, orow[p], o_tiles[p]) for p in range(P)]
        for p in range(P):
            o_ref[0, pl.ds(t8, SUBLANES), p * UW:(p + 1) * UW] = o_tiles[p]
        return carry

    lax.fori_loop(0, TB // SUBLANES, group, 0)

    @pl.when(tb == pl.num_programs(2) - 1)
    def _():
        tf_ref[0] = s_ref[...]


def rwkv7(rows, v, s0, *, reverse, passes, units_per_step=16, time_block=128):
    B, L, W = v.shape
    UW = REC_UNIT
    U = W // UW
    P = units_per_step
    TB = time_block
    nT = L // TB
    seg = B_HEAD
    assert U % P == 0 and L % TB == 0 and s0.shape == (B, U, seg, UW)
    lane = jnp.arange(UW)
    g = (lane[None, :] % seg == jnp.arange(seg)[:, None]).astype(F32)
    hm = (lane[:, None] // seg == lane[None, :] // seg).astype(BF16)
    body = functools.partial(_rwkv_body, TB=TB, P=P, passes=passes, reverse=reverse)
    if reverse:
        row_spec = pl.BlockSpec((1, TB, P * UW), lambda b, u, t: (b, nT - 1 - t, u))
    else:
        row_spec = pl.BlockSpec((1, TB, P * UW), lambda b, u, t: (b, t, u))
    st_spec = pl.BlockSpec((1, P, seg, UW), lambda b, u, t: (b, u, 0, 0))
    return pl.pallas_call(
        body,
        out_shape=(jax.ShapeDtypeStruct((B, L, W), F32), jax.ShapeDtypeStruct(s0.shape, F32)),
        grid=(B, U // P, nT),
        in_specs=[row_spec] * 6 + [pl.BlockSpec((seg, UW), lambda b, u, t: (0, 0)),
                                   pl.BlockSpec((UW, UW), lambda b, u, t: (0, 0)), st_spec],
        out_specs=(row_spec, st_spec),
        scratch_shapes=[pltpu.VMEM((P, seg, UW), F32), pltpu.VMEM((SUBLANES, P, seg, UW), F32),
                        pltpu.VMEM((SUBLANES, P, seg, UW), F32)],
        compiler_params=_params("parallel", "parallel", "arbitrary"),
        name="rwkv7",
    )(*rows, v, g, hm, s0)


HG_STEP = 16


def _hgrn_body(q_ref, f_ref, v_ref, lb_ref, s0_ref, o_ref, sf_ref, s_ref, *, TB, Hh, reverse):
    C = HG_STEP
    tb = pl.program_id(2)

    @pl.when(tb == 0)
    def _():
        s_ref[...] = s0_ref[0]

    row = lax.broadcasted_iota(jnp.int32, (C, LANES), 0)
    hp = lax.Precision.HIGHEST

    def shift(x, d):
        if d == 0:
            return x
        return pltpu.roll(x, (C - d) if reverse else d, axis=0)

    def has_source(d):
        return (row <= C - 1 - d) if reverse else (row >= d)

    def block(i, carry):
        ci = (TB // C - 1 - i) if reverse else i
        t0 = pl.multiple_of(ci * C, C)
        for h in range(Hh):
            sl = slice(h * LANES, (h + 1) * LANES)
            hq = q_ref[0, pl.ds(t0, C), sl]
            q = hq * jax.nn.sigmoid(hq) * A_DK ** -0.5
            lb = lb_ref[:, sl]
            f = lb + (1.0 - lb) * jax.nn.sigmoid(f_ref[0, pl.ds(t0, C), sl])
            v = v_ref[0, pl.ds(t0, C), sl]
            k = 1.0 - f
            b = jnp.log(f)
            s = 1
            while s < C:
                b = b + jnp.where(has_source(s), shift(b, s), 0.0)
                s *= 2
            st = s_ref[h]
            o = lax.dot_general(q * jnp.exp(b), st, (((1,), (1,)), ((), ())), preferred_element_type=F32,
                                precision=hp)
            for d in range(C):
                arg = jnp.where(has_source(d), b - shift(b, d), -1e30)
                a = jnp.sum(q * shift(k, d) * jnp.exp(arg), axis=1, keepdims=True)
                o = o + a * shift(v, d)
            o_ref[0, pl.ds(t0, C), sl] = o
            last = 0 if reverse else C - 1
            bl = b[last:last + 1, :]
            kt = k * jnp.exp(bl - b)
            s_ref[h] = st * jnp.exp(bl) + lax.dot_general(v, kt, (((0,), (0,)), ((), ())),
                                                         preferred_element_type=F32, precision=hp)
        return carry

    lax.fori_loop(0, TB // C, block, 0)

    @pl.when(tb == pl.num_programs(2) - 1)
    def _():
        sf_ref[0] = s_ref[...]


def hgrn2(zv, cols, lb, s0, *, B, L, row0, reverse, heads_per_step=4):
    TB = zv.shape[1]
    W = A_WIDTH
    H = W // LANES
    Hh = heads_per_step
    bw = Hh * LANES
    nT = L // TB
    assert H % Hh == 0 and L % TB == 0 and TB % HG_STEP == 0 and all(c % bw == 0 for c in cols)

    def tblock(b, t):
        return row0 + b * nT + ((nT - 1 - t) if reverse else t)

    def in_spec(col):
        return pl.BlockSpec((1, TB, bw), lambda b, u, t: (tblock(b, t), 0, col // bw + u))

    st_spec = pl.BlockSpec((1, Hh, LANES, LANES), lambda b, u, t: (b, u, 0, 0))
    o, s = pl.pallas_call(
        functools.partial(_hgrn_body, TB=TB, Hh=Hh, reverse=reverse),
        out_shape=(jax.ShapeDtypeStruct((B * nT, TB, W), F32), jax.ShapeDtypeStruct(s0.shape, F32)),
        grid=(B, H // Hh, nT),
        in_specs=[in_spec(cols[0]), in_spec(cols[1]), in_spec(cols[2]),
                  pl.BlockSpec((1, bw), lambda b, u, t: (0, u)), st_spec],
        out_specs=(pl.BlockSpec((1, TB, bw), lambda b, u, t: (tblock(b, t) - row0, 0, u)), st_spec),
        scratch_shapes=[pltpu.VMEM((Hh, LANES, LANES), F32)],
        compiler_params=_params("parallel", "parallel", "arbitrary"),
        name="hgrn2",
    )(zv, zv, zv, lb, s0)
    return o.reshape(B, L, W), s


def _attn_body(q_ref, k_ref, v_ref, o_ref):
    k = k_ref[0]
    v = v_ref[0]
    scale = C_HEAD_DIM ** -0.5
    for g in range(C_GROUP):
        q = q_ref[0, :, g * C_HEAD_DIM:(g + 1) * C_HEAD_DIM]
        s = lax.dot_general(q, k, (((1,), (1,)), ((), ())), preferred_element_type=F32) * scale
        m = jnp.max(s, axis=-1, keepdims=True)
        e = jnp.exp(s - m)
        p = e / jnp.sum(e, axis=-1, keepdims=True)
        o = jnp.dot(p.astype(BF16), v, preferred_element_type=F32)
        o_ref[0, :, g * C_HEAD_DIM:(g + 1) * C_HEAD_DIM] = o


def block_attention(q, k, v):
    B, L, _ = q.shape
    Lk = k.shape[1]
    gw = C_GROUP * C_HEAD_DIM
    return pl.pallas_call(
        _attn_body,
        out_shape=jax.ShapeDtypeStruct((B, L, C_WIDTH), F32),
        grid=(B, C_KV_HEADS, L // Q_BLOCK),
        in_specs=[pl.BlockSpec((1, Q_BLOCK, gw), lambda b, h, i: (b, i, h)),
                  pl.BlockSpec((1, Lk, C_HEAD_DIM), lambda b, h, i: (b, 0, h)),
                  pl.BlockSpec((1, Lk, C_HEAD_DIM), lambda b, h, i: (b, 0, h))],
        out_specs=pl.BlockSpec((1, Q_BLOCK, gw), lambda b, h, i: (b, i, h)),
        compiler_params=_params("parallel", "parallel", "parallel"),
        name="attn",
    )(q, k, v)


def _dft_body(a_ref, u_ref, o_ref):
    o_ref[0] = jnp.dot(a_ref[...], u_ref[0], preferred_element_type=F32,
                       precision=lax.Precision.HIGHEST)


def dft_mm(a, u, *, bn=512):
    M, K = a.shape
    B, _, C = u.shape
    return pl.pallas_call(
        _dft_body,
        out_shape=jax.ShapeDtypeStruct((B, M, C), F32),
        grid=(B, C // bn),
        in_specs=[pl.BlockSpec((M, K), lambda b, j: (0, 0)),
                  pl.BlockSpec((1, K, bn), lambda b, j: (b, 0, j))],
        out_specs=pl.BlockSpec((1, M, bn), lambda b, j: (b, 0, j)),
        compiler_params=_params("parallel", "parallel"),
        name="dft_mm",
    )(a, u)


def _dft_matrix(L):
    N = 2 * L
    r = jnp.arange(N, dtype=jnp.int32)[:, None]
    s = jnp.arange(L, dtype=jnp.int32)[None, :]
    is_cos = r <= L
    f = jnp.where(is_cos, r, r - L)
    ang = (2.0 * math.pi / N) * ((f * s) % N).astype(F32)
    return jnp.where(is_cos, jnp.cos(ang), jnp.sin(ang))


def _centred_long_conv(u, h, bias, fmat, fmat_t):
    L = u.shape[1]
    N = 2 * L
    h_mod = jnp.concatenate([h[:1], 2.0 * h[1:]], axis=0)
    kf = dft_mm(fmat, h_mod[None])[0]
    r = jnp.arange(N)
    f = jnp.where(r <= L, r, r - L)
    herm = jnp.where((r == 0) | (r == L), 1.0, 2.0).astype(F32) / N
    scale = kf[f] * herm[:, None]
    uf = dft_mm(fmat, u)
    y = dft_mm(fmat_t, uf * scale[None])
    return y + u * bias


def _hyena_filters(L, w1, b1, w2, b2, w3, b3, w4, freq):
    t = jnp.linspace(0.0, 1.0, L, dtype=F32)[:, None]
    w_pos = 2.0 * math.pi * jnp.arange(L, dtype=F32)[:, None] / L
    bands = jnp.linspace(1e-4, HY_BANDS - 1, HY_BANDS, dtype=F32)[None]
    zpos = jnp.concatenate([t, jnp.cos(bands * w_pos), -jnp.sin(bands * w_pos)], axis=-1)
    f = freq.astype(F32)
    hdn = jnp.sin(f * (zpos @ w1 + b1))
    hdn = jnp.sin(f * (hdn @ w2 + b2))
    hdn = jnp.sin(f * (hdn @ w3 + b3))
    h = mm_auto(hdn, w4)
    deltas = jnp.abs(jnp.linspace(HY_MIN_DECAY, HY_MAX_DECAY, HY_ORDER * HY_W, dtype=F32))
    h = h * jnp.exp(-t * deltas)
    return h.reshape(L, HY_ORDER, HY_W)


def _rms(x, g, eps=NORM_EPS):
    return x * lax.rsqrt(jnp.mean(x * x, axis=-1, keepdims=True) + eps) * g


def _neighbours(u):
    up = jnp.pad(u, ((0, 0), (1, 1), (0, 0)))
    return up[:, :-2], up[:, 2:]


def _mm3(x, w):
    B, L, K = x.shape
    return mm_auto(x.reshape(B * L, K), w).reshape(B, L, -1)


def _even_core(zv, row0, B, L, s_hf, s_hb, s_rf, s_rb, p):
    z = zv[row0:row0 + B * L // SEQ_BLOCK].reshape(B, L, -1)
    zb = z[..., EV_A_IN:]
    hg = z[..., 4 * A_WIDTH:EV_A_IN]

    def hgrn(fcol, lb, s0, rev):
        o, s = hgrn2(zv, (0, fcol, 3 * A_WIDTH), lb[None], jnp.swapaxes(s0, -1, -2), B=B, L=L, row0=row0,
                     reverse=rev)
        return o, jnp.swapaxes(s, -1, -2)

    oa_f, sf = hgrn(A_WIDTH, p['lb_f'], s_hf, False)
    oa_b, sb = hgrn(2 * A_WIDTH, p['lb_b'], s_hb, True)
    o_a = _rms(oa_f + oa_b, p['hg_g']) * jax.nn.silu(hg)

    prev, nxt = _neighbours(zb)
    zb = zb + p['mu'] * (0.5 * (prev + nxt) - zb)
    rr, kr, vr, wdf, wdb, adf, adb, gd = jnp.split(zb, B_SPLITS, axis=-1)
    kk = (kr * p['k_k']).reshape(B, L, B_HEADS, B_HEAD)
    kk = kk / jnp.maximum(jnp.sqrt(jnp.sum(kk * kk, axis=-1, keepdims=True)), 1e-12)
    kk = kk.reshape(B, L, B_WIDTH)
    hpu = REC_UNIT // B_HEAD
    n_unit = B_HEADS // hpu

    def direction(wd, w0, w2, ad, a0, a2, s0, rev):
        logw = -math.exp(-0.5) * jax.nn.sigmoid(w0 + _mm3(jnp.tanh(wd), w2))
        a = jax.nn.sigmoid(a0 + _mm3(ad, a2))
        kd = kr * (1.0 + (a - 1.0) * p['k_a'])
        t0 = s0.reshape(B, n_unit, hpu, B_HEAD, B_HEAD).transpose(0, 1, 3, 2, 4).reshape(B, n_unit, B_HEAD, REC_UNIT)
        o, s = rwkv7([jnp.exp(logw), kk * a, kd, rr, kk], vr, t0, reverse=rev, passes=REC_PASSES)
        s = s.reshape(B, n_unit, B_HEAD, hpu, B_HEAD).transpose(0, 1, 3, 2, 4).reshape(B, B_HEADS, B_HEAD, B_HEAD)
        return o, s

    ob_f, rf = direction(wdf, p['w0f'], p['w2f'], adf, p['a0f'], p['a2f'], s_rf, False)
    ob_b, rb = direction(wdb, p['w0b'], p['w2b'], adb, p['a0b'], p['a2b'], s_rb, True)
    o = (ob_f + ob_b).reshape(B, L, B_HEADS, B_HEAD)
    mean = jnp.mean(o, axis=-1, keepdims=True)
    var = jnp.mean(jnp.square(o - mean), axis=-1, keepdims=True)
    o = ((o - mean) * lax.rsqrt(var + B_GN_EPS)).reshape(B, L, B_WIDTH) * p['ln_g'] + p['ln_b']
    r4 = rr.reshape(B, L, B_HEADS, B_HEAD)
    bonus = jnp.sum(r4 * kr.reshape(r4.shape) * p['r_k'], axis=-1, keepdims=True) * vr.reshape(r4.shape)
    g = _mm3(jax.nn.sigmoid(gd), p['g2'])
    o_b = (o + bonus.reshape(B, L, B_WIDTH)) * g
    return jnp.concatenate([o_a, o_b], axis=-1), (sf, sb, rf, rb)


def _axial_rope(x):
    L = x.shape[1]
    n_rows = L // GRID_W
    row = jnp.repeat(jnp.arange(n_rows), GRID_W).astype(F32)
    col = jnp.tile(jnp.arange(GRID_W), n_rows).astype(F32)
    half = ROPE_AXIS_DIM // 2
    inv = ROPE_THETA ** (-jnp.arange(half, dtype=F32) / half)
    ang = jnp.stack([row[:, None] * inv, col[:, None] * inv], axis=1)
    cos = jnp.cos(ang)[None, :, None]
    sin = jnp.sin(ang)[None, :, None]
    xr = x.reshape(x.shape[:3] + (2, 2, half))
    x1, x2 = xr[..., 0, :], xr[..., 1, :]
    out = jnp.stack([x1 * cos - x2 * sin, x1 * sin + x2 * cos], axis=-2)
    return out.reshape(x.shape)


def _odd_core(z, ctx_k, ctx_v, p):
    B, L, _ = z.shape
    q = _rms(z[..., :C_WIDTH].reshape(B, L, C_HEADS, C_HEAD_DIM), p['qn_g'])
    k = _rms(z[..., C_WIDTH:C_WIDTH + C_KV_WIDTH].reshape(B, L, C_KV_HEADS, C_HEAD_DIM), p['kn_g'])
    v = z[..., C_WIDTH + C_KV_WIDTH:C_IN].reshape(B, L, C_KV_HEADS, C_HEAD_DIM)
    if ctx_k is None:
        k_all, v_all = k, v
    else:
        q = _axial_rope(q)
        k = _axial_rope(k)
        k_all = jnp.concatenate([ctx_k, k], axis=1)
        v_all = jnp.concatenate([ctx_v, v], axis=1)
    Lk = k_all.shape[1]
    o_c = block_attention(q.reshape(B, L, C_WIDTH).astype(BF16), k_all.reshape(B, Lk, C_KV_WIDTH).astype(BF16),
                          v_all.reshape(B, Lk, C_KV_WIDTH).astype(BF16))
    u = z[..., C_IN:]
    prev, nxt = _neighbours(u)
    u = prev * p['conv_w'][0] + u * p['conv_w'][1] + nxt * p['conv_w'][2] + p['conv_b']
    hv, x1, x2 = jnp.split(u, 3, axis=-1)
    filt = _hyena_filters(L, p['f_w1'], p['f_b1'], p['f_w2'], p['f_b2'], p['f_w3'], p['f_b3'], p['f_w4'],
                          p['sin_freq'])
    fmat = _dft_matrix(L)
    fmat_t = fmat.T
    z1 = x1 * _centred_long_conv(hv, filt[:, 0], p['hy_bias'][0], fmat, fmat_t)
    o_d = x2 * _centred_long_conv(z1, filt[:, 1], p['hy_bias'][1], fmat, fmat_t)
    return jnp.concatenate([o_c, o_d], axis=-1), k, v


def kernel(x_prompt, x_sample, c, state_hgrn_fwd, state_hgrn_bwd, state_rwkv_fwd, state_rwkv_bwd, cache_k, cache_v, c_ctx, ada_w, ada_b, norm_g, ffn_w1, ffn_w3, ffn_w2, final_norm_g, ev_w_in, ev_w_out, hg_lb_fwd, hg_lb_bwd, hg_norm_g, rw_mu, rw_w0_f, rw_w2_f, rw_w0_b, rw_w2_b, rw_a0_f, rw_a2_f, rw_a0_b, rw_a2_b, rw_g2, rw_kk, rw_ka, rw_rk, rw_ln_g, rw_ln_b, od_w_in, od_w_out, at_qn_g, at_kn_g, hy_conv_w, hy_conv_b, hy_f_w1, hy_f_b1, hy_f_w2, hy_f_b2, hy_f_w3, hy_f_b3, hy_f_w4, hy_sin_freq, hy_bias):
    Bc, Lc, _ = x_prompt.shape
    Bl, Ll, _ = x_sample.shape
    n_ctx_tok = Bc * Lc
    n_lat_tok = Bl * Ll
    n_tok = n_ctx_tok + n_lat_tok
    grp = Ll
    assert n_ctx_tok % grp == 0
    n_grp = n_tok // grp
    grp_mod = jnp.array([0] * (n_ctx_tok // grp) + [1 + b for b in range(Bl)], jnp.int32)

    h = jnp.concatenate([x_prompt.reshape(n_ctx_tok, D_MODEL), x_sample.reshape(n_lat_tok, D_MODEL)], axis=0)
    h = h.reshape(n_grp, grp, D_MODEL)
    cvec = jnp.concatenate([c_ctx[None], c], axis=0)
    cvec = jnp.pad(jax.nn.silu(cvec), ((0, 8 - (1 + Bl)), (0, 0)))

    zeros_h = jnp.zeros((Bc, A_HEADS, A_DK, A_DV), F32)
    zeros_r = jnp.zeros((Bc, B_HEADS, B_HEAD, B_HEAD), F32)
    lb_fwd_all = jnp.cumsum(jax.nn.softmax(hg_lb_fwd, axis=0)[1:], axis=0)
    lb_bwd_all = jnp.cumsum(jax.nn.softmax(hg_lb_bwd, axis=0)[1:], axis=0)

    def ada_in(h, m, i, g):
        return (_rms(h, g) * (1.0 + m[:, None, 3 * i + 1]) + m[:, None, 3 * i]).astype(BF16).reshape(n_tok, D_MODEL)

    def ada_out(h, m, i, y, w):
        return h + w * m[:, None, 3 * i + 2] * y.reshape(h.shape)

    def ffn(x, l, s):
        mid = ffn_up(x, ffn_w1[l, s], ffn_w3[l, s])
        return mm(mid, ffn_w2[l, s].astype(BF16), bm=1024, bn=512, bk=D_FF // 2)

    new_hf, new_hb, new_rf, new_rb, new_k, new_v = [], [], [], [], [], []
    for l in range(DEPTH):
        m = mm_small(cvec.astype(BF16), ada_w[l], bn=1024) + ada_b[l]
        m = m[:1 + Bl].reshape(1 + Bl, N_MOD, D_MODEL)[grp_mod]
        h = ada_out(h, m, 0, ffn(ada_in(h, m, 0, norm_g[l, 0]), l, 0), 0.5)
        xn = ada_in(h, m, 1, norm_g[l, 1])
        j = l // 2
        if l % 2 == 0:
            p = dict(lb_f=lb_fwd_all[j], lb_b=lb_bwd_all[j], hg_g=hg_norm_g[j], mu=rw_mu[j],
                     w0f=rw_w0_f[j], w2f=rw_w2_f[j], w0b=rw_w0_b[j], w2b=rw_w2_b[j],
                     a0f=rw_a0_f[j], a2f=rw_a2_f[j], a0b=rw_a0_b[j], a2b=rw_a2_b[j], g2=rw_g2[j],
                     k_k=rw_kk[j], k_a=rw_ka[j], r_k=rw_rk[j], ln_g=rw_ln_g[j], ln_b=rw_ln_b[j])
            z = mm_ws(xn, ev_w_in[j], bm=512, bn=896)
            zv = z.reshape(n_tok // SEQ_BLOCK, SEQ_BLOCK, -1)
            o_ctx, (s_hf, s_hb, s_rf, s_rb) = _even_core(zv, 0, Bc, Lc, zeros_h, zeros_h, zeros_r, zeros_r, p)
            o_lat, _ = _even_core(zv, n_ctx_tok // SEQ_BLOCK, Bl, Ll, state_hgrn_fwd[:, j], state_hgrn_bwd[:, j],
                                  state_rwkv_fwd[:, j], state_rwkv_bwd[:, j], p)
            new_hf.append(s_hf)
            new_hb.append(s_hb)
            new_rf.append(s_rf)
            new_rb.append(s_rb)
            w_out = ev_w_out[j]
        else:
            p = dict(qn_g=at_qn_g[j], kn_g=at_kn_g[j], conv_w=hy_conv_w[j], conv_b=hy_conv_b[j],
                     f_w1=hy_f_w1[j], f_b1=hy_f_b1[j], f_w2=hy_f_w2[j], f_b2=hy_f_b2[j], f_w3=hy_f_w3[j],
                     f_b3=hy_f_b3[j], f_w4=hy_f_w4[j], sin_freq=hy_sin_freq[j], hy_bias=hy_bias[j])
            z = mm_ws(xn, od_w_in[j], bm=1024, bn=512)
            o_ctx, k_ctx, v_ctx = _odd_core(z[:n_ctx_tok].reshape(Bc, Lc, -1), None, None, p)
            o_lat, _, _ = _odd_core(z[n_ctx_tok:].reshape(Bl, Ll, -1), cache_k[:, j], cache_v[:, j], p)
            new_k.append(k_ctx)
            new_v.append(v_ctx)
            w_out = od_w_out[j]
        o = jnp.concatenate([o_ctx.reshape(n_ctx_tok, -1), o_lat.reshape(n_lat_tok, -1)], axis=0)
        h = ada_out(h, m, 1, mm_ws(o.astype(BF16), w_out, bm=1024, bn=512), 1.0)
        h = ada_out(h, m, 2, ffn(ada_in(h, m, 2, norm_g[l, 2]), l, 1), 0.5)
    y = _rms(h, final_norm_g).reshape(n_tok, D_MODEL)
    return (y[:n_ctx_tok].reshape(Bc, Lc, D_MODEL), y[n_ctx_tok:].reshape(Bl, Ll, D_MODEL),
            jnp.stack(new_hf, axis=1), jnp.stack(new_hb, axis=1), jnp.stack(new_rf, axis=1),
            jnp.stack(new_rb, axis=1), jnp.stack(new_k, axis=1), jnp.stack(new_v, axis=1))
```

```python
import functools
import math
from typing import NamedTuple

import numpy as np
import jax
import jax.numpy as jnp
from jax import lax
from jax.experimental import pallas as pl
from jax.experimental.pallas import tpu as pltpu

F32 = jnp.float32
BF16 = jnp.bfloat16

D_MODEL = 4096
DEPTH = 2
GRID_W = 64
N_MOD = 9
D_FF = 11008
NORM_EPS = 1e-6
A_WIDTH = D_MODEL // 2
A_HEADS = 16
A_DK = A_WIDTH // A_HEADS
A_DV = A_WIDTH // A_HEADS
B_WIDTH = D_MODEL // 2
B_HEAD = 64
B_HEADS = B_WIDTH // B_HEAD
B_DECAY_LORA = max(32, int(round(1.8 * B_WIDTH ** 0.5 / 32)) * 32)
B_AAA_LORA = max(32, int(round(1.8 * B_WIDTH ** 0.5 / 32)) * 32)
B_GATE_LORA = max(32, int(round(0.6 * B_WIDTH ** 0.8 / 32)) * 32)
B_TAIL = 2 * B_DECAY_LORA + 2 * B_AAA_LORA + B_GATE_LORA
B_GN_EPS = 64e-5
EV_A_IN = 5 * A_WIDTH
EV_B_IN = 3 * B_WIDTH + B_TAIL
C_HEAD_DIM = 128
C_HEADS = 16
C_KV_HEADS = 4
C_GROUP = C_HEADS // C_KV_HEADS
C_WIDTH = C_HEADS * C_HEAD_DIM
C_KV_WIDTH = C_KV_HEADS * C_HEAD_DIM
C_IN = C_WIDTH + 2 * C_KV_WIDTH
ROPE_AXIS_DIM = C_HEAD_DIM // 2
ROPE_THETA = 10000.0
Q_BLOCK = 128
HY_W = D_MODEL // 2
HY_ORDER = 2
HY_BANDS = 16
HY_TARGET = 1e-2
HY_MIN_DECAY = math.log(HY_TARGET) / 1.5
HY_MAX_DECAY = math.log(HY_TARGET) / 0.3

V7X_VMEM_LIMIT_BYTES = 56 * 1024 * 1024
LANES = 128
SUBLANES = 8
ROW_BLOCK = 256
REC_TIME_BLOCK = 128
REC_PASSES = (1, 2, 1)
REC_SUB_UNITS = 4
PREP_COLS = 512


def _params(*sem):
    return pltpu.CompilerParams(dimension_semantics=sem, vmem_limit_bytes=V7X_VMEM_LIMIT_BYTES)


class Seqs(NamedTuple):
    Bc: int
    Lc: int
    Bl: int
    Ll: int

    @property
    def n_tok(self):
        return self.Bc * self.Lc + self.Bl * self.Ll

    def locate(self, j, blk):
        n_ctx = self.Bc * self.Lc // blk
        cb, lb = self.Lc // blk, self.Ll // blk
        is_ctx = j < n_ctx
        jl = jnp.maximum(j - n_ctx, 0)
        pos = jnp.where(is_ctx, j % cb, jl % lb)
        nblk = jnp.where(is_ctx, cb, lb)
        sid = jnp.where(is_ctx, j // cb, self.Bc + jl // lb)
        return pos, nblk, sid


def _mm_body(x_ref, w_ref, o_ref, *, nk):
    part = jnp.dot(x_ref[...], w_ref[...], preferred_element_type=F32)
    if nk == 1:
        o_ref[...] = part.astype(o_ref.dtype)
    else:
        k = pl.program_id(2)

        @pl.when(k == 0)
        def _():
            o_ref[...] = part

        @pl.when(k > 0)
        def _():
            o_ref[...] += part


def mm(x, w, *, bm, bn, bk=None, out_dtype=F32):
    M, K = x.shape
    N = w.shape[1]
    bk = K if bk is None else bk
    nk = K // bk
    assert M % bm == 0 and N % bn == 0 and K % bk == 0
    assert nk == 1 or out_dtype == F32
    return pl.pallas_call(
        functools.partial(_mm_body, nk=nk),
        out_shape=jax.ShapeDtypeStruct((M, N), out_dtype),
        grid=(M // bm, N // bn, nk),
        in_specs=[pl.BlockSpec((bm, bk), lambda i, j, k: (i, k)),
                  pl.BlockSpec((bk, bn), lambda i, j, k: (k, j))],
        out_specs=pl.BlockSpec((bm, bn), lambda i, j, k: (i, j)),
        compiler_params=_params("parallel", "parallel", "arbitrary"),
        name="mm",
    )(x, w)


def _mm_ws_body(x_ref, w_ref, o_ref, wb_ref):
    @pl.when(pl.program_id(1) == 0)
    def _():
        wb_ref[...] = w_ref[...].astype(BF16)

    o_ref[...] = jnp.dot(x_ref[...], wb_ref[...], preferred_element_type=F32).astype(o_ref.dtype)


def mm_ws(x, w, *, bm, bn, out_dtype=F32):
    M, K = x.shape
    N = w.shape[1]
    assert M % bm == 0 and N % bn == 0
    return pl.pallas_call(
        _mm_ws_body,
        out_shape=jax.ShapeDtypeStruct((M, N), out_dtype),
        grid=(N // bn, M // bm),
        in_specs=[pl.BlockSpec((bm, K), lambda j, i: (i, 0)),
                  pl.BlockSpec((K, bn), lambda j, i: (0, j))],
        out_specs=pl.BlockSpec((bm, bn), lambda j, i: (i, j)),
        scratch_shapes=[pltpu.VMEM((K, bn), BF16)],
        compiler_params=_params("arbitrary", "arbitrary"),
        name="mm_ws",
    )(x, w)


def _mm_small_body(x_ref, w_ref, o_ref):
    o_ref[...] = jnp.dot(x_ref[...], w_ref[...].astype(BF16), preferred_element_type=F32)


def mm_small(x, w, *, bn):
    M, K = x.shape
    N = w.shape[1]
    return pl.pallas_call(
        _mm_small_body,
        out_shape=jax.ShapeDtypeStruct((M, N), F32),
        grid=(N // bn,),
        in_specs=[pl.BlockSpec((M, K), lambda j: (0, 0)),
                  pl.BlockSpec((K, bn), lambda j: (0, j))],
        out_specs=pl.BlockSpec((M, bn), lambda j: (0, j)),
        compiler_params=_params("parallel"),
        name="mm_small",
    )(x, w)


def _ffn_up_body(x_ref, w1_ref, w3_ref, o_ref, w1b_ref, w3b_ref):
    @pl.when(pl.program_id(1) == 0)
    def _():
        w1b_ref[...] = w1_ref[...].astype(BF16)
        w3b_ref[...] = w3_ref[...].astype(BF16)

    x = x_ref[...]
    a = jnp.dot(x, w1b_ref[...], preferred_element_type=F32)
    b = jnp.dot(x, w3b_ref[...], preferred_element_type=F32)
    o_ref[...] = (a * jax.nn.sigmoid(a) * b).astype(o_ref.dtype)


def ffn_up(x, w1, w3, *, bm=1024, bn=256):
    M, K = x.shape
    N = w1.shape[1]
    return pl.pallas_call(
        _ffn_up_body,
        out_shape=jax.ShapeDtypeStruct((M, N), BF16),
        grid=(N // bn, M // bm),
        in_specs=[pl.BlockSpec((bm, K), lambda j, i: (i, 0)),
                  pl.BlockSpec((K, bn), lambda j, i: (0, j)),
                  pl.BlockSpec((K, bn), lambda j, i: (0, j))],
        out_specs=pl.BlockSpec((bm, bn), lambda j, i: (i, j)),
        scratch_shapes=[pltpu.VMEM((K, bn), BF16), pltpu.VMEM((K, bn), BF16)],
        compiler_params=_params("arbitrary", "arbitrary"),
        name="ffn_up",
    )(x, w1, w3)


def _ada_body(*refs, has_y, has_mod, i_out, w_out, i_in):
    refs = list(refs)
    h = refs.pop(0)[...]
    if has_y:
        y_ref = refs.pop(0)
        mo_ref = refs.pop(0)
        h = h + w_out * mo_ref[0, 3 * i_out + 2:3 * i_out + 3, :] * y_ref[...]
    g_ref = refs.pop(0)
    x = h * lax.rsqrt(jnp.mean(h * h, axis=-1, keepdims=True) + NORM_EPS) * g_ref[...]
    if has_mod:
        mi_ref = refs.pop(0)
        x = x * (1.0 + mi_ref[0, 3 * i_in + 1:3 * i_in + 2, :]) + mi_ref[0, 3 * i_in:3 * i_in + 1, :]
    if has_y:
        refs.pop(0)[...] = h
    x_ref = refs.pop(0)
    x_ref[...] = x.astype(x_ref.dtype)


def ada_step(h, g, *, mod_row, y=None, m_out=None, i_out=0, w_out=1.0, m_in=None, i_in=0, out_dtype=BF16):
    T, D = h.shape
    R = ROW_BLOCK
    has_y, has_mod = y is not None, m_in is not None
    row_spec = pl.BlockSpec((R, D), lambda i: (i, 0))
    mod_spec = pl.BlockSpec((1, N_MOD, D), lambda i: (mod_row(i), 0, 0))
    args, specs = [h], [row_spec]
    if has_y:
        args += [y, m_out]
        specs += [row_spec, mod_spec]
    args.append(g.reshape(1, D))
    specs.append(pl.BlockSpec((1, D), lambda i: (0, 0)))
    if has_mod:
        args.append(m_in)
        specs.append(mod_spec)
    out_shape, out_specs = [jax.ShapeDtypeStruct((T, D), out_dtype)], [row_spec]
    if has_y:
        out_shape.insert(0, jax.ShapeDtypeStruct((T, D), F32))
        out_specs.insert(0, row_spec)
    res = pl.pallas_call(
        functools.partial(_ada_body, has_y=has_y, has_mod=has_mod, i_out=i_out, w_out=w_out, i_in=i_in),
        out_shape=tuple(out_shape), grid=(T // R,), in_specs=specs, out_specs=tuple(out_specs),
        compiler_params=_params("parallel"), name="ada_step",
    )(*args)
    return res if has_y else res[0]


def _head_sums(x, hseg, n_pass=2):
    outs = []
    for c in range(x.shape[1] // LANES):
        rem = x[:, c * LANES:(c + 1) * LANES]
        acc = None
        for _ in range(n_pass):
            hi = rem.astype(BF16)
            part = jnp.dot(hi, hseg, preferred_element_type=F32)
            acc = part if acc is None else acc + part
            rem = rem - hi.astype(F32)
        outs.append(acc)
    return outs[0] if len(outs) == 1 else jnp.concatenate(outs, axis=1)


def _head_matrix():
    lane = jnp.arange(LANES)
    return (lane[:, None] // B_HEAD == lane[None, :] // B_HEAD).astype(BF16)


_TAIL_OFF = dict(wdf=0, wdb=B_DECAY_LORA, adf=2 * B_DECAY_LORA, adb=2 * B_DECAY_LORA + B_AAA_LORA,
                 gd=2 * B_DECAY_LORA + 2 * B_AAA_LORA)


def _prep_body(r_ref, k_ref, v_ref, t_ref, rp_ref, kp_ref, vp_ref, tp_ref, rn_ref, kn_ref, vn_ref, tn_ref,
               mu_r_ref, mu_k_ref, mu_v_ref, mu_t_ref, w2f_ref, w2b_ref, a2f_ref, a2b_ref, g2_ref, rowp_ref, hseg_ref,
               wf_ref, bf_ref, kdf_ref, wb_ref, bb_ref, kdb_ref, ro_ref, kko_ref, vo_ref, go_ref, bonus_ref,
               *, seqs):
    R = ROW_BLOCK
    pos, nblk, _ = seqs.locate(pl.program_id(0), R)
    has_prev = (pos != 0).astype(F32)
    has_next = (pos != nblk - 1).astype(F32)

    def shifted(cur_ref, p_ref, n_ref, mu_ref):
        x = cur_ref[0]
        row = lax.broadcasted_iota(jnp.int32, x.shape, 0)
        prev = jnp.where(row == 0, p_ref[0, SUBLANES - 1:SUBLANES, :] * has_prev, pltpu.roll(x, 1, axis=0))
        nxt = jnp.where(row == R - 1, n_ref[0, 0:1, :] * has_next, pltpu.roll(x, R - 1, axis=0))
        return x + mu_ref[...] * (0.5 * (prev + nxt) - x)

    rr = shifted(r_ref, rp_ref, rn_ref, mu_r_ref)
    kr = shifted(k_ref, kp_ref, kn_ref, mu_k_ref)
    vr = shifted(v_ref, vp_ref, vn_ref, mu_v_ref)
    tail = shifted(t_ref, tp_ref, tn_ref, mu_t_ref)
    t_tanh = jnp.tanh(tail).astype(BF16)
    t_lin = tail.astype(BF16)
    t_sig = jax.nn.sigmoid(tail).astype(BF16)

    def dot(a, w_ref):
        return jnp.dot(a, w_ref[...], preferred_element_type=F32)

    w0f, w0b, a0f, a0b, k_k, k_a, r_k = [rowp_ref[i:i + 1, :] for i in range(7)]
    hseg = hseg_ref[...]
    kk = kr * k_k
    kk = kk / jnp.maximum(jnp.sqrt(_head_sums(kk * kk, hseg)), 1e-12)
    for w0, a0, w2_ref, a2_ref, w_out, b_out, kd_out in ((w0f, a0f, w2f_ref, a2f_ref, wf_ref, bf_ref, kdf_ref),
                                                        (w0b, a0b, w2b_ref, a2b_ref, wb_ref, bb_ref, kdb_ref)):
        logw = -math.exp(-0.5) * jax.nn.sigmoid(w0 + dot(t_tanh, w2_ref))
        a = jax.nn.sigmoid(a0 + dot(t_lin, a2_ref))
        w_out[0] = jnp.exp(logw)
        b_out[0] = kk * a
        kd_out[0] = kr * (1.0 + (a - 1.0) * k_a)
    ro_ref[0] = rr
    kko_ref[0] = kk
    vo_ref[0] = vr
    go_ref[0] = dot(t_sig, g2_ref)
    bonus_ref[0] = _head_sums(rr * kr * r_k, hseg) * vr


def rwkv_prep(z, seqs, p):
    T = z.shape[0]
    R, CW = ROW_BLOCK, PREP_COLS
    NB, n8 = T // R, T // SUBLANES
    per = R // SUBLANES
    tail0 = EV_A_IN + 3 * B_WIDTH
    ztail = z[:, tail0:]
    zv, z8 = z.reshape(NB, R, -1), z.reshape(n8, SUBLANES, -1)
    tv, t8 = ztail.reshape(NB, R, B_TAIL), ztail.reshape(n8, SUBLANES, B_TAIL)
    base = [(EV_A_IN + i * B_WIDTH) // CW for i in range(3)]

    def cur(b):
        return pl.BlockSpec((1, R, CW), lambda i, c: (i, 0, b + c))

    def prev(b):
        return pl.BlockSpec((1, SUBLANES, CW), lambda i, c: (jnp.maximum(i * per - 1, 0), 0, b + c))

    def nxt(b):
        return pl.BlockSpec((1, SUBLANES, CW), lambda i, c: (jnp.minimum((i + 1) * per, n8 - 1), 0, b + c))

    t_cur = pl.BlockSpec((1, R, B_TAIL), lambda i, c: (i, 0, 0))
    t_prev = pl.BlockSpec((1, SUBLANES, B_TAIL), lambda i, c: (jnp.maximum(i * per - 1, 0), 0, 0))
    t_next = pl.BlockSpec((1, SUBLANES, B_TAIL), lambda i, c: (jnp.minimum((i + 1) * per, n8 - 1), 0, 0))

    mu = p['mu']
    mu_main = mu[:3 * B_WIDTH].reshape(1, 3 * B_WIDTH)
    mu_t = mu[3 * B_WIDTH:].reshape(1, B_TAIL)

    def mu_spec(i):
        return pl.BlockSpec((1, CW), lambda r, c: (0, i * (B_WIDTH // CW) + c))

    def padded(w, off):
        return jnp.zeros((B_TAIL, B_WIDTH), BF16).at[off:off + w.shape[0]].set(w.astype(BF16))

    lora = [padded(p['w2f'], _TAIL_OFF['wdf']), padded(p['w2b'], _TAIL_OFF['wdb']),
            padded(p['a2f'], _TAIL_OFF['adf']), padded(p['a2b'], _TAIL_OFF['adb']), padded(p['g2'], _TAIL_OFF['gd'])]
    lora_spec = pl.BlockSpec((B_TAIL, CW), lambda i, c: (0, c))
    rowp = jnp.stack([p['w0f'], p['w0b'], p['a0f'], p['a0b'], p['k_k'], p['k_a'], p['r_k'].reshape(-1),
                      jnp.zeros((B_WIDTH,), F32)])
    out_spec = pl.BlockSpec((1, R, CW), lambda i, c: (i, 0, c))
    names = ('w_f', 'b_f', 'kd_f', 'w_b', 'b_b', 'kd_b', 'r', 'kk', 'v', 'g', 'bonus')
    outs = pl.pallas_call(
        functools.partial(_prep_body, seqs=seqs),
        out_shape=tuple(jax.ShapeDtypeStruct((NB, R, B_WIDTH), F32) for _ in names),
        grid=(NB, B_WIDTH // CW),
        in_specs=[cur(base[0]), cur(base[1]), cur(base[2]), t_cur,
                  prev(base[0]), prev(base[1]), prev(base[2]), t_prev,
                  nxt(base[0]), nxt(base[1]), nxt(base[2]), t_next,
                  mu_spec(0), mu_spec(1), mu_spec(2), pl.BlockSpec((1, B_TAIL), lambda i, c: (0, 0)),
                  lora_spec, lora_spec, lora_spec, lora_spec, lora_spec,
                  pl.BlockSpec((SUBLANES, CW), lambda i, c: (0, c)),
                  pl.BlockSpec((LANES, LANES), lambda i, c: (0, 0))],
        out_specs=tuple(out_spec for _ in names),
        compiler_params=_params("parallel", "parallel"),
        name="rwkv_prep",
    )(zv, zv, zv, tv, z8, z8, z8, t8, z8, z8, z8, t8, mu_main, mu_main, mu_main, mu_t, *lora, rowp, _head_matrix())
    return dict(zip(names, outs))


def _rwkv_body(w_ref, b_ref, kd_ref, r_ref, kk_ref, v_ref, g_ref, h_ref, t0_ref, o_ref, tf_ref,
               s_ref, vb_ref, xo_ref, *, seqs, NB, passes, reverse):
    row_refs = (w_ref, b_ref, kd_ref, r_ref, kk_ref)
    TB = REC_TIME_BLOCK
    P, R, UW = s_ref.shape
    jb = pl.program_id(0)
    jb = (NB - 1 - jb) if reverse else jb
    pos, nblk, _ = seqs.locate(jb, TB)
    first = pos == ((nblk - 1) if reverse else 0)
    last = pos == (0 if reverse else (nblk - 1))

    @pl.when(first)
    def _():
        s_ref[...] = t0_ref[0]

    G = g_ref[...][None]
    Hm = h_ref[...]

    Q = REC_SUB_UNITS
    subs = [slice(q * Q, (q + 1) * Q) for q in range(P // Q)]

    def seg_sum(x, n_pass):
        x = x.reshape(Q * R, UW)
        hi = x.astype(BF16)
        acc = jnp.dot(hi, Hm, preferred_element_type=F32)
        rem = x
        for _ in range(n_pass - 1):
            rem = rem - hi.astype(F32)
            hi = rem.astype(BF16)
            acc = acc + jnp.dot(hi, Hm, preferred_element_type=F32)
        return acc.reshape(Q, R, UW)

    def unit_rows(tile, j, qs):
        return jnp.stack([tile[j:j + 1, p * UW:(p + 1) * UW] for p in range(qs.start, qs.stop)])

    sub_iota = lax.broadcasted_iota(jnp.int32, (SUBLANES, UW), 0)

    def group(i, carry):
        gi = (TB // SUBLANES - 1 - i) if reverse else i
        t8 = pl.multiple_of(gi * SUBLANES, SUBLANES)
        tiles = [r[0, pl.ds(t8, SUBLANES), :] for r in row_refs]
        v_tile = v_ref[0, pl.ds(t8, SUBLANES), :]
        for j in range(SUBLANES):
            for qs in subs:
                vb_ref[j, qs] = seg_sum(G * unit_rows(v_tile, j, qs), passes[0])
        for jj in range(SUBLANES):
            j = SUBLANES - 1 - jj if reverse else jj
            for qs in subs:
                w, b, kd, r, kk = [unit_rows(tile, j, qs) for tile in tiles]
                S = s_ref[qs]
                u = seg_sum(S * kk, passes[1])
                S = S * w - u * b + vb_ref[j, qs] * kd
                s_ref[qs] = S
                xo_ref[j, qs] = S * r
        for qs in subs:
            o_tiles = [jnp.zeros((SUBLANES, UW), F32) for _ in range(Q)]
            for j in range(SUBLANES):
                orow = jnp.sum(G * seg_sum(xo_ref[j, qs], passes[2]), axis=1, keepdims=True)
                o_tiles = [jnp.where(sub_iota == j, orow[p], o_tiles[p]) for p in range(Q)]
            for p in range(Q):
                lo = (qs.start + p) * UW
                o_ref[0, pl.ds(t8, SUBLANES), lo:lo + UW] = o_tiles[p]
        return carry

    lax.fori_loop(0, TB // SUBLANES, group, 0)

    @pl.when(last)
    def _():
        tf_ref[0] = s_ref[...]


def rwkv7(rows, v, s0, seqs, *, reverse, passes):
    T = v.shape[0] * v.shape[1]
    TB, UW, seg = REC_TIME_BLOCK, LANES, B_HEAD
    NB = T // TB
    hpu = UW // seg
    P = B_HEADS // hpu
    n_seq = s0.shape[0]
    t0 = s0.reshape(n_seq, P, hpu, seg, seg).transpose(0, 1, 3, 2, 4).reshape(n_seq, P, seg, UW)
    lane = jnp.arange(UW)
    g = (lane[None, :] % seg == jnp.arange(seg)[:, None]).astype(F32)

    def blk(j):
        return (NB - 1 - j) if reverse else j

    row_spec = pl.BlockSpec((1, TB, B_WIDTH), lambda j: (blk(j), 0, 0))
    st_spec = pl.BlockSpec((1, P, seg, UW), lambda j: (seqs.locate(blk(j), TB)[2], 0, 0, 0))
    o, s = pl.pallas_call(
        functools.partial(_rwkv_body, seqs=seqs, NB=NB, passes=passes, reverse=reverse),
        out_shape=(jax.ShapeDtypeStruct((NB, TB, B_WIDTH), F32), jax.ShapeDtypeStruct(t0.shape, F32)),
        grid=(NB,),
        in_specs=[row_spec] * 6 + [pl.BlockSpec((seg, UW), lambda j: (0, 0)),
                                   pl.BlockSpec((UW, UW), lambda j: (0, 0)), st_spec],
        out_specs=(row_spec, st_spec),
        scratch_shapes=[pltpu.VMEM((P, seg, UW), F32), pltpu.VMEM((SUBLANES, P, seg, UW), F32),
                        pltpu.VMEM((SUBLANES, P, seg, UW), F32)],
        compiler_params=_params("arbitrary"),
        name="rwkv7",
    )(*[x.reshape(NB, TB, B_WIDTH) for x in rows], v.reshape(NB, TB, B_WIDTH), g, _head_matrix(), t0)
    s = s.reshape(n_seq, P, seg, hpu, seg).transpose(0, 1, 3, 2, 4).reshape(s0.shape)
    return o.reshape(T, B_WIDTH), s


HG_STEP = 16


def _hgrn_body(q_ref, f_ref, v_ref, lb_ref, s0_ref, o_ref, sf_ref, s_ref, *, seqs, NB, Hh, reverse):
    C = HG_STEP
    TB = ROW_BLOCK
    jb = pl.program_id(1)
    jb = (NB - 1 - jb) if reverse else jb
    pos, nblk, _ = seqs.locate(jb, TB)
    first = pos == ((nblk - 1) if reverse else 0)
    last = pos == (0 if reverse else (nblk - 1))

    @pl.when(first)
    def _():
        s_ref[...] = s0_ref[0]

    row = lax.broadcasted_iota(jnp.int32, (C, LANES), 0)
    hp = lax.Precision.HIGHEST

    def shift(x, d):
        if d == 0:
            return x
        return pltpu.roll(x, (C - d) if reverse else d, axis=0)

    def has_source(d):
        return (row <= C - 1 - d) if reverse else (row >= d)

    def block(i, carry):
        ci = (TB // C - 1 - i) if reverse else i
        t0 = pl.multiple_of(ci * C, C)
        for h in range(Hh):
            sl = slice(h * LANES, (h + 1) * LANES)
            hq = q_ref[0, pl.ds(t0, C), sl]
            q = hq * jax.nn.sigmoid(hq) * A_DK ** -0.5
            lb = lb_ref[:, sl]
            f = lb + (1.0 - lb) * jax.nn.sigmoid(f_ref[0, pl.ds(t0, C), sl])
            v = v_ref[0, pl.ds(t0, C), sl]
            k = 1.0 - f
            b = jnp.log(f)
            s = 1
            while s < C:
                b = b + jnp.where(has_source(s), shift(b, s), 0.0)
                s *= 2
            st = s_ref[h]
            o = lax.dot_general(q * jnp.exp(b), st, (((1,), (1,)), ((), ())), preferred_element_type=F32,
                                precision=hp)
            for d in range(C):
                arg = jnp.where(has_source(d), b - shift(b, d), -1e30)
                a = jnp.sum(q * shift(k, d) * jnp.exp(arg), axis=1, keepdims=True)
                o = o + a * shift(v, d)
            o_ref[0, pl.ds(t0, C), sl] = o
            end = 0 if reverse else C - 1
            bl = b[end:end + 1, :]
            kt = k * jnp.exp(bl - b)
            s_ref[h] = st * jnp.exp(bl) + lax.dot_general(v, kt, (((0,), (0,)), ((), ())),
                                                         preferred_element_type=F32, precision=hp)
        return carry

    lax.fori_loop(0, TB // C, block, 0)

    @pl.when(last)
    def _():
        sf_ref[0] = s_ref[...]


def hgrn2(zv, cols, lb, s0, seqs, *, reverse, heads_per_step=4):
    NB, TB, _ = zv.shape
    W = A_WIDTH
    H = W // LANES
    Hh = heads_per_step
    bw = Hh * LANES
    assert TB == ROW_BLOCK and H % Hh == 0 and all(c % bw == 0 for c in cols)

    def blk(j):
        return (NB - 1 - j) if reverse else j

    def in_spec(col):
        return pl.BlockSpec((1, TB, bw), lambda u, j: (blk(j), 0, col // bw + u))

    st_spec = pl.BlockSpec((1, Hh, LANES, LANES), lambda u, j: (seqs.locate(blk(j), TB)[2], u, 0, 0))
    st0 = jnp.swapaxes(s0, -1, -2)
    o, s = pl.pallas_call(
        functools.partial(_hgrn_body, seqs=seqs, NB=NB, Hh=Hh, reverse=reverse),
        out_shape=(jax.ShapeDtypeStruct((NB, TB, W), F32), jax.ShapeDtypeStruct(st0.shape, F32)),
        grid=(H // Hh, NB),
        in_specs=[in_spec(cols[0]), in_spec(cols[1]), in_spec(cols[2]),
                  pl.BlockSpec((1, bw), lambda u, j: (0, u)), st_spec],
        out_specs=(pl.BlockSpec((1, TB, bw), lambda u, j: (blk(j), 0, u)), st_spec),
        scratch_shapes=[pltpu.VMEM((Hh, LANES, LANES), F32)],
        compiler_params=_params("parallel", "arbitrary"),
        name="hgrn2",
    )(zv, zv, zv, lb, st0)
    return o.reshape(NB * TB, W), jnp.swapaxes(s, -1, -2)


def _even_post_body(oaf_ref, oab_ref, hg_ref, obf_ref, obb_ref, bonus_ref, g_ref, rowp_ref, hseg_ref, o_ref):
    hg_g, ln_g, ln_b = [rowp_ref[i:i + 1, :] for i in range(3)]
    oa = oaf_ref[0] + oab_ref[0]
    hg = hg_ref[0]
    oa = oa * lax.rsqrt(jnp.mean(oa * oa, axis=-1, keepdims=True) + NORM_EPS) * hg_g * (hg * jax.nn.sigmoid(hg))
    o_ref[0, :, :A_WIDTH] = oa.astype(o_ref.dtype)
    hseg = hseg_ref[...]
    ob = obf_ref[0] + obb_ref[0]
    cen = ob - _head_sums(ob, hseg) * (1.0 / B_HEAD)
    var = _head_sums(cen * cen, hseg) * (1.0 / B_HEAD)
    ob = (cen * lax.rsqrt(var + B_GN_EPS) * ln_g + ln_b + bonus_ref[0]) * g_ref[0]
    o_ref[0, :, A_WIDTH:] = ob.astype(o_ref.dtype)


def even_post(oa_f, oa_b, zv, ob_f, ob_b, bonus, g, p):
    NB, R, _ = zv.shape
    T = NB * R
    W = A_WIDTH
    assert W == B_WIDTH and (4 * A_WIDTH) % W == 0
    spec = pl.BlockSpec((1, R, W), lambda i: (i, 0, 0))
    rowp = jnp.stack([p['hg_g'], p['ln_g'], p['ln_b']] + [jnp.zeros((W,), F32)] * 5)
    v3 = lambda x: x.reshape(NB, R, W)
    out = pl.pallas_call(
        _even_post_body,
        out_shape=jax.ShapeDtypeStruct((NB, R, 2 * W), BF16),
        grid=(NB,),
        in_specs=[spec, spec, pl.BlockSpec((1, R, W), lambda i: (i, 0, 4 * A_WIDTH // W)), spec, spec, spec, spec,
                  pl.BlockSpec((SUBLANES, W), lambda i: (0, 0)), pl.BlockSpec((LANES, LANES), lambda i: (0, 0))],
        out_specs=pl.BlockSpec((1, R, 2 * W), lambda i: (i, 0, 0)),
        compiler_params=_params("parallel"),
        name="even_post",
    )(v3(oa_f), v3(oa_b), zv, v3(ob_f), v3(ob_b), bonus, g, rowp, _head_matrix())
    return out.reshape(T, 2 * W)


def even_mixer(z, seqs, st, p):
    T = z.shape[0]
    zv = z.reshape(T // ROW_BLOCK, ROW_BLOCK, -1)
    oa_f, sf = hgrn2(zv, (0, A_WIDTH, 3 * A_WIDTH), p['lb_f'][None], st['hf'], seqs, reverse=False)
    oa_b, sb = hgrn2(zv, (0, 2 * A_WIDTH, 3 * A_WIDTH), p['lb_b'][None], st['hb'], seqs, reverse=True)
    q = rwkv_prep(z, seqs, p)
    ob_f, rf = rwkv7((q['w_f'], q['b_f'], q['kd_f'], q['r'], q['kk']), q['v'], st['rf'], seqs, reverse=False,
                     passes=REC_PASSES)
    ob_b, rb = rwkv7((q['w_b'], q['b_b'], q['kd_b'], q['r'], q['kk']), q['v'], st['rb'], seqs, reverse=True,
                     passes=REC_PASSES)
    return even_post(oa_f, oa_b, zv, ob_f, ob_b, q['bonus'], q['g'], p), (sf, sb, rf, rb)


def _attn_body(q_ref, k_ref, v_ref, o_ref):
    k = k_ref[0]
    v = v_ref[0]
    scale = C_HEAD_DIM ** -0.5
    for g in range(C_GROUP):
        q = q_ref[0, :, g * C_HEAD_DIM:(g + 1) * C_HEAD_DIM]
        s = lax.dot_general(q, k, (((1,), (1,)), ((), ())), preferred_element_type=F32) * scale
        m = jnp.max(s, axis=-1, keepdims=True)
        e = jnp.exp(s - m)
        p = e / jnp.sum(e, axis=-1, keepdims=True)
        o = jnp.dot(p.astype(BF16), v, preferred_element_type=F32)
        o_ref[0, :, g * C_HEAD_DIM:(g + 1) * C_HEAD_DIM] = o


def block_attention(q, k, v):
    B, L, _ = q.shape
    Lk = k.shape[1]
    gw = C_GROUP * C_HEAD_DIM
    return pl.pallas_call(
        _attn_body,
        out_shape=jax.ShapeDtypeStruct((B, L, C_WIDTH), F32),
        grid=(B, C_KV_HEADS, L // Q_BLOCK),
        in_specs=[pl.BlockSpec((1, Q_BLOCK, gw), lambda b, h, i: (b, i, h)),
                  pl.BlockSpec((1, Lk, C_HEAD_DIM), lambda b, h, i: (b, 0, h)),
                  pl.BlockSpec((1, Lk, C_HEAD_DIM), lambda b, h, i: (b, 0, h))],
        out_specs=pl.BlockSpec((1, Q_BLOCK, gw), lambda b, h, i: (b, i, h)),
        compiler_params=_params("parallel", "parallel", "parallel"),
        name="attn",
    )(q, k, v)


def _dft_body(a_ref, u_ref, o_ref):
    o_ref[0] = jnp.dot(a_ref[...], u_ref[0], preferred_element_type=F32,
                       precision=lax.Precision.HIGHEST)


def dft_mm(a, u, *, bn=512):
    M, K = a.shape
    B, _, C = u.shape
    return pl.pallas_call(
        _dft_body,
        out_shape=jax.ShapeDtypeStruct((B, M, C), F32),
        grid=(B, C // bn),
        in_specs=[pl.BlockSpec((M, K), lambda b, j: (0, 0)),
                  pl.BlockSpec((1, K, bn), lambda b, j: (b, 0, j))],
        out_specs=pl.BlockSpec((1, M, bn), lambda b, j: (b, 0, j)),
        compiler_params=_params("parallel", "parallel"),
        name="dft_mm",
    )(a, u)


def _dft_matrix(L):
    N = 2 * L
    r = jnp.arange(N, dtype=jnp.int32)[:, None]
    s = jnp.arange(L, dtype=jnp.int32)[None, :]
    is_cos = r <= L
    f = jnp.where(is_cos, r, r - L)
    ang = (2.0 * math.pi / N) * ((f * s) % N).astype(F32)
    return jnp.where(is_cos, jnp.cos(ang), jnp.sin(ang))


def _centred_long_conv(u, h, bias, fmat, fmat_t):
    L = u.shape[1]
    N = 2 * L
    h_mod = jnp.concatenate([h[:1], 2.0 * h[1:]], axis=0)
    kf = dft_mm(fmat, h_mod[None])[0]
    r = jnp.arange(N)
    f = jnp.where(r <= L, r, r - L)
    herm = jnp.where((r == 0) | (r == L), 1.0, 2.0).astype(F32) / N
    scale = kf[f] * herm[:, None]
    uf = dft_mm(fmat, u)
    y = dft_mm(fmat_t, uf * scale[None])
    return y + u * bias


def _hyena_filters(L, w1, b1, w2, b2, w3, b3, w4, freq):
    t = jnp.linspace(0.0, 1.0, L, dtype=F32)[:, None]
    w_pos = 2.0 * math.pi * jnp.arange(L, dtype=F32)[:, None] / L
    bands = jnp.linspace(1e-4, HY_BANDS - 1, HY_BANDS, dtype=F32)[None]
    zpos = jnp.concatenate([t, jnp.cos(bands * w_pos), -jnp.sin(bands * w_pos)], axis=-1)
    f = freq.astype(F32)
    hdn = jnp.sin(f * (zpos @ w1 + b1))
    hdn = jnp.sin(f * (hdn @ w2 + b2))
    hdn = jnp.sin(f * (hdn @ w3 + b3))
    h = mm(hdn.astype(BF16), w4.astype(BF16), bm=L, bn=1024)
    deltas = jnp.abs(jnp.linspace(HY_MIN_DECAY, HY_MAX_DECAY, HY_ORDER * HY_W, dtype=F32))
    h = h * jnp.exp(-t * deltas)
    return h.reshape(L, HY_ORDER, HY_W)


def _rms(x, g, eps=NORM_EPS):
    return x * lax.rsqrt(jnp.mean(x * x, axis=-1, keepdims=True) + eps) * g


def _neighbours(u):
    up = jnp.pad(u, ((0, 0), (1, 1), (0, 0)))
    return up[:, :-2], up[:, 2:]


def _axial_rope(x):
    L = x.shape[1]
    n_rows = L // GRID_W
    row = jnp.repeat(jnp.arange(n_rows), GRID_W).astype(F32)
    col = jnp.tile(jnp.arange(GRID_W), n_rows).astype(F32)
    half = ROPE_AXIS_DIM // 2
    inv = ROPE_THETA ** (-jnp.arange(half, dtype=F32) / half)
    ang = jnp.stack([row[:, None] * inv, col[:, None] * inv], axis=1)
    cos = jnp.cos(ang)[None, :, None]
    sin = jnp.sin(ang)[None, :, None]
    xr = x.reshape(x.shape[:3] + (2, 2, half))
    x1, x2 = xr[..., 0, :], xr[..., 1, :]
    out = jnp.stack([x1 * cos - x2 * sin, x1 * sin + x2 * cos], axis=-2)
    return out.reshape(x.shape)


def _odd_core(z, ctx_k, ctx_v, p):
    B, L, _ = z.shape
    q = _rms(z[..., :C_WIDTH].reshape(B, L, C_HEADS, C_HEAD_DIM), p['qn_g'])
    k = _rms(z[..., C_WIDTH:C_WIDTH + C_KV_WIDTH].reshape(B, L, C_KV_HEADS, C_HEAD_DIM), p['kn_g'])
    v = z[..., C_WIDTH + C_KV_WIDTH:C_IN].reshape(B, L, C_KV_HEADS, C_HEAD_DIM)
    if ctx_k is None:
        k_all, v_all = k, v
    else:
        q = _axial_rope(q)
        k = _axial_rope(k)
        k_all = jnp.concatenate([ctx_k, k], axis=1)
        v_all = jnp.concatenate([ctx_v, v], axis=1)
    Lk = k_all.shape[1]
    o_c = block_attention(q.reshape(B, L, C_WIDTH).astype(BF16), k_all.reshape(B, Lk, C_KV_WIDTH).astype(BF16),
                          v_all.reshape(B, Lk, C_KV_WIDTH).astype(BF16))
    u = z[..., C_IN:]
    prev, nxt = _neighbours(u)
    u = prev * p['conv_w'][0] + u * p['conv_w'][1] + nxt * p['conv_w'][2] + p['conv_b']
    hv, x1, x2 = jnp.split(u, 3, axis=-1)
    filt = _hyena_filters(L, p['f_w1'], p['f_b1'], p['f_w2'], p['f_b2'], p['f_w3'], p['f_b3'], p['f_w4'],
                          p['sin_freq'])
    fmat = _dft_matrix(L)
    fmat_t = fmat.T
    z1 = x1 * _centred_long_conv(hv, filt[:, 0], p['hy_bias'][0], fmat, fmat_t)
    o_d = x2 * _centred_long_conv(z1, filt[:, 1], p['hy_bias'][1], fmat, fmat_t)
    return jnp.concatenate([o_c, o_d], axis=-1).astype(BF16), k, v


def kernel(x_prompt, x_sample, c, state_hgrn_fwd, state_hgrn_bwd, state_rwkv_fwd, state_rwkv_bwd, cache_k, cache_v, c_ctx, ada_w, ada_b, norm_g, ffn_w1, ffn_w3, ffn_w2, final_norm_g, ev_w_in, ev_w_out, hg_lb_fwd, hg_lb_bwd, hg_norm_g, rw_mu, rw_w0_f, rw_w2_f, rw_w0_b, rw_w2_b, rw_a0_f, rw_a2_f, rw_a0_b, rw_a2_b, rw_g2, rw_kk, rw_ka, rw_rk, rw_ln_g, rw_ln_b, od_w_in, od_w_out, at_qn_g, at_kn_g, hy_conv_w, hy_conv_b, hy_f_w1, hy_f_b1, hy_f_w2, hy_f_b2, hy_f_w3, hy_f_b3, hy_f_w4, hy_sin_freq, hy_bias):
    Bc, Lc, _ = x_prompt.shape
    Bl, Ll, _ = x_sample.shape
    seqs = Seqs(Bc, Lc, Bl, Ll)
    n_ctx_tok = Bc * Lc
    n_tok = seqs.n_tok
    assert Lc % ROW_BLOCK == 0 and Ll % ROW_BLOCK == 0 and n_ctx_tok % Ll == 0

    def mod_row(i):
        return jnp.maximum(i * ROW_BLOCK // Ll - (n_ctx_tok // Ll - 1), 0)

    h = jnp.concatenate([x_prompt.reshape(n_ctx_tok, D_MODEL), x_sample.reshape(Bl * Ll, D_MODEL)], axis=0)
    cvec = jnp.concatenate([c_ctx[None], c], axis=0)
    cvec = jnp.pad(jax.nn.silu(cvec), ((0, SUBLANES - (1 + Bl)), (0, 0))).astype(BF16)
    mods = [(mm_small(cvec, ada_w[l], bn=1024) + ada_b[l])[:1 + Bl].reshape(1 + Bl, N_MOD, D_MODEL)
            for l in range(DEPTH)]

    lb_fwd_all = jnp.cumsum(jax.nn.softmax(hg_lb_fwd, axis=0)[1:], axis=0)
    lb_bwd_all = jnp.cumsum(jax.nn.softmax(hg_lb_bwd, axis=0)[1:], axis=0)

    def with_ctx_zeros(s):
        return jnp.concatenate([jnp.zeros((Bc,) + s.shape[1:], F32), s], axis=0)

    def ffn(x, l, s):
        mid = ffn_up(x, ffn_w1[l, s], ffn_w3[l, s])
        return mm(mid, ffn_w2[l, s].astype(BF16), bm=1024, bn=512, bk=D_FF // 2)

    new_hf, new_hb, new_rf, new_rb, new_k, new_v = [], [], [], [], [], []
    x = ada_step(h, norm_g[0, 0], mod_row=mod_row, m_in=mods[0], i_in=0)
    y_out = None
    for l in range(DEPTH):
        m = mods[l]
        h, xn = ada_step(h, norm_g[l, 1], mod_row=mod_row, y=ffn(x, l, 0), m_out=m, i_out=0, w_out=0.5,
                         m_in=m, i_in=1)
        j = l // 2
        if l % 2 == 0:
            p = dict(lb_f=lb_fwd_all[j], lb_b=lb_bwd_all[j], hg_g=hg_norm_g[j], mu=rw_mu[j],
                     w0f=rw_w0_f[j], w2f=rw_w2_f[j], w0b=rw_w0_b[j], w2b=rw_w2_b[j],
                     a0f=rw_a0_f[j], a2f=rw_a2_f[j], a0b=rw_a0_b[j], a2b=rw_a2_b[j], g2=rw_g2[j],
                     k_k=rw_kk[j], k_a=rw_ka[j], r_k=rw_rk[j], ln_g=rw_ln_g[j], ln_b=rw_ln_b[j])
            st = dict(hf=with_ctx_zeros(state_hgrn_fwd[:, j]), hb=with_ctx_zeros(state_hgrn_bwd[:, j]),
                      rf=with_ctx_zeros(state_rwkv_fwd[:, j]), rb=with_ctx_zeros(state_rwkv_bwd[:, j]))
            z = mm_ws(xn, ev_w_in[j], bm=512, bn=896)
            o, (s_hf, s_hb, s_rf, s_rb) = even_mixer(z, seqs, st, p)
            new_hf.append(s_hf[:Bc])
            new_hb.append(s_hb[:Bc])
            new_rf.append(s_rf[:Bc])
            new_rb.append(s_rb[:Bc])
            w_out = ev_w_out[j]
        else:
            p = dict(qn_g=at_qn_g[j], kn_g=at_kn_g[j], conv_w=hy_conv_w[j], conv_b=hy_conv_b[j],
                     f_w1=hy_f_w1[j], f_b1=hy_f_b1[j], f_w2=hy_f_w2[j], f_b2=hy_f_b2[j], f_w3=hy_f_w3[j],
                     f_b3=hy_f_b3[j], f_w4=hy_f_w4[j], sin_freq=hy_sin_freq[j], hy_bias=hy_bias[j])
            z = mm_ws(xn, od_w_in[j], bm=1024, bn=512)
            o_ctx, k_ctx, v_ctx = _odd_core(z[:n_ctx_tok].reshape(Bc, Lc, -1), None, None, p)
            o_lat, _, _ = _odd_core(z[n_ctx_tok:].reshape(Bl, Ll, -1), cache_k[:, j], cache_v[:, j], p)
            new_k.append(k_ctx)
            new_v.append(v_ctx)
            o = jnp.concatenate([o_ctx.reshape(n_ctx_tok, -1), o_lat.reshape(Bl * Ll, -1)], axis=0)
            w_out = od_w_out[j]
        h, x = ada_step(h, norm_g[l, 2], mod_row=mod_row, y=mm_ws(o, w_out, bm=1024, bn=512), m_out=m, i_out=1,
                        w_out=1.0, m_in=m, i_in=2)
        y = ffn(x, l, 1)
        if l + 1 < DEPTH:
            h, x = ada_step(h, norm_g[l + 1, 0], mod_row=mod_row, y=y, m_out=m, i_out=2, w_out=0.5,
                            m_in=mods[l + 1], i_in=0)
        else:
            h, y_out = ada_step(h, final_norm_g, mod_row=mod_row, y=y, m_out=m, i_out=2, w_out=0.5, out_dtype=F32)
    return (y_out[:n_ctx_tok].reshape(Bc, Lc, D_MODEL), y_out[n_ctx_tok:].reshape(Bl, Ll, D_MODEL),
            jnp.stack(new_hf, axis=1), jnp.stack(new_hb, axis=1), jnp.stack(new_rf, axis=1),
            jnp.stack(new_rb, axis=1), jnp.stack(new_k, axis=1), jnp.stack(new_v, axis=1))
```

```python
import functools
import math
from typing import NamedTuple

import numpy as np
import jax
import jax.numpy as jnp
from jax import lax
from jax.experimental import pallas as pl
from jax.experimental.pallas import tpu as pltpu

F32 = jnp.float32
BF16 = jnp.bfloat16

D_MODEL = 4096
DEPTH = 2
GRID_W = 64
N_MOD = 9
D_FF = 11008
NORM_EPS = 1e-6
A_WIDTH = D_MODEL // 2
A_HEADS = 16
A_DK = A_WIDTH // A_HEADS
A_DV = A_WIDTH // A_HEADS
B_WIDTH = D_MODEL // 2
B_HEAD = 64
B_HEADS = B_WIDTH // B_HEAD
B_DECAY_LORA = max(32, int(round(1.8 * B_WIDTH ** 0.5 / 32)) * 32)
B_AAA_LORA = max(32, int(round(1.8 * B_WIDTH ** 0.5 / 32)) * 32)
B_GATE_LORA = max(32, int(round(0.6 * B_WIDTH ** 0.8 / 32)) * 32)
B_TAIL = 2 * B_DECAY_LORA + 2 * B_AAA_LORA + B_GATE_LORA
B_GN_EPS = 64e-5
EV_A_IN = 5 * A_WIDTH
EV_B_IN = 3 * B_WIDTH + B_TAIL
C_HEAD_DIM = 128
C_HEADS = 16
C_KV_HEADS = 4
C_GROUP = C_HEADS // C_KV_HEADS
C_WIDTH = C_HEADS * C_HEAD_DIM
C_KV_WIDTH = C_KV_HEADS * C_HEAD_DIM
C_IN = C_WIDTH + 2 * C_KV_WIDTH
ROPE_AXIS_DIM = C_HEAD_DIM // 2
ROPE_THETA = 10000.0
Q_BLOCK = 128
HY_W = D_MODEL // 2
HY_ORDER = 2
HY_BANDS = 16
HY_TARGET = 1e-2
HY_MIN_DECAY = math.log(HY_TARGET) / 1.5
HY_MAX_DECAY = math.log(HY_TARGET) / 0.3

V7X_VMEM_LIMIT_BYTES = 56 * 1024 * 1024
LANES = 128
SUBLANES = 8
ROW_BLOCK = 256
REC_TIME_BLOCK = 128
REC_PASSES = (1, 1, 1)
REC_SUB_UNITS = 4
PREP_COLS = 512


def _params(*sem):
    return pltpu.CompilerParams(dimension_semantics=sem, vmem_limit_bytes=V7X_VMEM_LIMIT_BYTES)


class Seqs(NamedTuple):
    Bc: int
    Lc: int
    Bl: int
    Ll: int

    @property
    def n_tok(self):
        return self.Bc * self.Lc + self.Bl * self.Ll

    def locate(self, j, blk):
        n_ctx = self.Bc * self.Lc // blk
        cb, lb = self.Lc // blk, self.Ll // blk
        is_ctx = j < n_ctx
        jl = jnp.maximum(j - n_ctx, 0)
        pos = jnp.where(is_ctx, j % cb, jl % lb)
        nblk = jnp.where(is_ctx, cb, lb)
        sid = jnp.where(is_ctx, j // cb, self.Bc + jl // lb)
        return pos, nblk, sid


def _mm_body(x_ref, w_ref, o_ref, *, nk):
    part = jnp.dot(x_ref[...], w_ref[...], preferred_element_type=F32)
    if nk == 1:
        o_ref[...] = part.astype(o_ref.dtype)
    else:
        k = pl.program_id(2)

        @pl.when(k == 0)
        def _():
            o_ref[...] = part

        @pl.when(k > 0)
        def _():
            o_ref[...] += part


def mm(x, w, *, bm, bn, bk=None, out_dtype=F32):
    M, K = x.shape
    N = w.shape[1]
    bk = K if bk is None else bk
    nk = K // bk
    assert M % bm == 0 and N % bn == 0 and K % bk == 0
    assert nk == 1 or out_dtype == F32
    return pl.pallas_call(
        functools.partial(_mm_body, nk=nk),
        out_shape=jax.ShapeDtypeStruct((M, N), out_dtype),
        grid=(M // bm, N // bn, nk),
        in_specs=[pl.BlockSpec((bm, bk), lambda i, j, k: (i, k)),
                  pl.BlockSpec((bk, bn), lambda i, j, k: (k, j))],
        out_specs=pl.BlockSpec((bm, bn), lambda i, j, k: (i, j)),
        compiler_params=_params("parallel", "parallel", "arbitrary"),
        name="mm",
    )(x, w)


def _mm_ws_body(x_ref, w_ref, o_ref, wb_ref):
    @pl.when(pl.program_id(1) == 0)
    def _():
        wb_ref[...] = w_ref[...].astype(BF16)

    o_ref[...] = jnp.dot(x_ref[...], wb_ref[...], preferred_element_type=F32).astype(o_ref.dtype)


def mm_ws(x, w, *, bm, bn, out_dtype=F32):
    M, K = x.shape
    N = w.shape[1]
    assert M % bm == 0 and N % bn == 0
    return pl.pallas_call(
        _mm_ws_body,
        out_shape=jax.ShapeDtypeStruct((M, N), out_dtype),
        grid=(N // bn, M // bm),
        in_specs=[pl.BlockSpec((bm, K), lambda j, i: (i, 0)),
                  pl.BlockSpec((K, bn), lambda j, i: (0, j))],
        out_specs=pl.BlockSpec((bm, bn), lambda j, i: (i, j)),
        scratch_shapes=[pltpu.VMEM((K, bn), BF16)],
        compiler_params=_params("arbitrary", "arbitrary"),
        name="mm_ws",
    )(x, w)


def _mm_small_body(x_ref, w_ref, o_ref):
    o_ref[...] = jnp.dot(x_ref[...], w_ref[...].astype(BF16), preferred_element_type=F32)


def mm_small(x, w, *, bn):
    M, K = x.shape
    N = w.shape[1]
    return pl.pallas_call(
        _mm_small_body,
        out_shape=jax.ShapeDtypeStruct((M, N), F32),
        grid=(N // bn,),
        in_specs=[pl.BlockSpec((M, K), lambda j: (0, 0)),
                  pl.BlockSpec((K, bn), lambda j: (0, j))],
        out_specs=pl.BlockSpec((M, bn), lambda j: (0, j)),
        compiler_params=_params("parallel"),
        name="mm_small",
    )(x, w)


def _ffn_up_body(x_ref, w1_ref, w3_ref, o_ref, w1b_ref, w3b_ref):
    @pl.when(pl.program_id(1) == 0)
    def _():
        w1b_ref[...] = w1_ref[...].astype(BF16)
        w3b_ref[...] = w3_ref[...].astype(BF16)

    x = x_ref[...]
    a = jnp.dot(x, w1b_ref[...], preferred_element_type=F32)
    b = jnp.dot(x, w3b_ref[...], preferred_element_type=F32)
    o_ref[...] = (a * jax.nn.sigmoid(a) * b).astype(o_ref.dtype)


def ffn_up(x, w1, w3, *, bm=1024, bn=256):
    M, K = x.shape
    N = w1.shape[1]
    return pl.pallas_call(
        _ffn_up_body,
        out_shape=jax.ShapeDtypeStruct((M, N), BF16),
        grid=(N // bn, M // bm),
        in_specs=[pl.BlockSpec((bm, K), lambda j, i: (i, 0)),
                  pl.BlockSpec((K, bn), lambda j, i: (0, j)),
                  pl.BlockSpec((K, bn), lambda j, i: (0, j))],
        out_specs=pl.BlockSpec((bm, bn), lambda j, i: (i, j)),
        scratch_shapes=[pltpu.VMEM((K, bn), BF16), pltpu.VMEM((K, bn), BF16)],
        compiler_params=_params("arbitrary", "arbitrary"),
        name="ffn_up",
    )(x, w1, w3)


def _ada_body(*refs, has_y, has_mod, i_out, w_out, i_in):
    refs = list(refs)
    h = refs.pop(0)[...]
    if has_y:
        y_ref = refs.pop(0)
        mo_ref = refs.pop(0)
        h = h + w_out * mo_ref[0, 3 * i_out + 2:3 * i_out + 3, :] * y_ref[...]
    g_ref = refs.pop(0)
    x = h * lax.rsqrt(jnp.mean(h * h, axis=-1, keepdims=True) + NORM_EPS) * g_ref[...]
    if has_mod:
        mi_ref = refs.pop(0)
        x = x * (1.0 + mi_ref[0, 3 * i_in + 1:3 * i_in + 2, :]) + mi_ref[0, 3 * i_in:3 * i_in + 1, :]
    if has_y:
        refs.pop(0)[...] = h
    x_ref = refs.pop(0)
    x_ref[...] = x.astype(x_ref.dtype)


def ada_step(h, g, *, mod_row, y=None, m_out=None, i_out=0, w_out=1.0, m_in=None, i_in=0, out_dtype=BF16):
    T, D = h.shape
    R = ROW_BLOCK
    has_y, has_mod = y is not None, m_in is not None
    row_spec = pl.BlockSpec((R, D), lambda i: (i, 0))
    mod_spec = pl.BlockSpec((1, N_MOD, D), lambda i: (mod_row(i), 0, 0))
    args, specs = [h], [row_spec]
    if has_y:
        args += [y, m_out]
        specs += [row_spec, mod_spec]
    args.append(g.reshape(1, D))
    specs.append(pl.BlockSpec((1, D), lambda i: (0, 0)))
    if has_mod:
        args.append(m_in)
        specs.append(mod_spec)
    out_shape, out_specs = [jax.ShapeDtypeStruct((T, D), out_dtype)], [row_spec]
    if has_y:
        out_shape.insert(0, jax.ShapeDtypeStruct((T, D), F32))
        out_specs.insert(0, row_spec)
    res = pl.pallas_call(
        functools.partial(_ada_body, has_y=has_y, has_mod=has_mod, i_out=i_out, w_out=w_out, i_in=i_in),
        out_shape=tuple(out_shape), grid=(T // R,), in_specs=specs, out_specs=tuple(out_specs),
        compiler_params=_params("parallel"), name="ada_step",
    )(*args)
    return res if has_y else res[0]


def _head_sums(x, hseg, n_pass=2):
    outs = []
    for c in range(x.shape[1] // LANES):
        rem = x[:, c * LANES:(c + 1) * LANES]
        acc = None
        for _ in range(n_pass):
            hi = rem.astype(BF16)
            part = jnp.dot(hi, hseg, preferred_element_type=F32)
            acc = part if acc is None else acc + part
            rem = rem - hi.astype(F32)
        outs.append(acc)
    return outs[0] if len(outs) == 1 else jnp.concatenate(outs, axis=1)


def _head_matrix():
    lane = jnp.arange(LANES)
    return (lane[:, None] // B_HEAD == lane[None, :] // B_HEAD).astype(BF16)


_TAIL_OFF = dict(wdf=0, wdb=B_DECAY_LORA, adf=2 * B_DECAY_LORA, adb=2 * B_DECAY_LORA + B_AAA_LORA,
                 gd=2 * B_DECAY_LORA + 2 * B_AAA_LORA)


def _prep_body(r_ref, k_ref, v_ref, t_ref, rp_ref, kp_ref, vp_ref, tp_ref, rn_ref, kn_ref, vn_ref, tn_ref,
               mu_r_ref, mu_k_ref, mu_v_ref, mu_t_ref, w2f_ref, w2b_ref, a2f_ref, a2b_ref, g2_ref, rowp_ref, hseg_ref,
               wf_ref, bf_ref, kdf_ref, wb_ref, bb_ref, kdb_ref, ro_ref, kko_ref, vo_ref, go_ref, bonus_ref,
               *, seqs):
    R = ROW_BLOCK
    pos, nblk, _ = seqs.locate(pl.program_id(0), R)
    has_prev = (pos != 0).astype(F32)
    has_next = (pos != nblk - 1).astype(F32)

    def shifted(cur_ref, p_ref, n_ref, mu_ref):
        x = cur_ref[0]
        row = lax.broadcasted_iota(jnp.int32, x.shape, 0)
        prev = jnp.where(row == 0, p_ref[0, SUBLANES - 1:SUBLANES, :] * has_prev, pltpu.roll(x, 1, axis=0))
        nxt = jnp.where(row == R - 1, n_ref[0, 0:1, :] * has_next, pltpu.roll(x, R - 1, axis=0))
        return x + mu_ref[...] * (0.5 * (prev + nxt) - x)

    rr = shifted(r_ref, rp_ref, rn_ref, mu_r_ref)
    kr = shifted(k_ref, kp_ref, kn_ref, mu_k_ref)
    vr = shifted(v_ref, vp_ref, vn_ref, mu_v_ref)
    tail = shifted(t_ref, tp_ref, tn_ref, mu_t_ref)
    t_tanh = jnp.tanh(tail).astype(BF16)
    t_lin = tail.astype(BF16)
    t_sig = jax.nn.sigmoid(tail).astype(BF16)

    def dot(a, w_ref):
        return jnp.dot(a, w_ref[...], preferred_element_type=F32)

    w0f, w0b, a0f, a0b, k_k, k_a, r_k = [rowp_ref[i:i + 1, :] for i in range(7)]
    hseg = hseg_ref[...]
    kk = kr * k_k
    kk = kk / jnp.maximum(jnp.sqrt(_head_sums(kk * kk, hseg)), 1e-12)
    for w0, a0, w2_ref, a2_ref, w_out, b_out, kd_out in ((w0f, a0f, w2f_ref, a2f_ref, wf_ref, bf_ref, kdf_ref),
                                                        (w0b, a0b, w2b_ref, a2b_ref, wb_ref, bb_ref, kdb_ref)):
        logw = -math.exp(-0.5) * jax.nn.sigmoid(w0 + dot(t_tanh, w2_ref))
        a = jax.nn.sigmoid(a0 + dot(t_lin, a2_ref))
        w_out[0] = jnp.exp(logw)
        b_out[0] = kk * a
        kd_out[0] = kr * (1.0 + (a - 1.0) * k_a)
    ro_ref[0] = rr
    kko_ref[0] = kk
    vo_ref[0] = vr
    go_ref[0] = dot(t_sig, g2_ref)
    bonus_ref[0] = _head_sums(rr * kr * r_k, hseg) * vr


def rwkv_prep(z, seqs, p):
    T = z.shape[0]
    R, CW = ROW_BLOCK, PREP_COLS
    NB, n8 = T // R, T // SUBLANES
    per = R // SUBLANES
    tail0 = EV_A_IN + 3 * B_WIDTH
    ztail = z[:, tail0:]
    zv, z8 = z.reshape(NB, R, -1), z.reshape(n8, SUBLANES, -1)
    tv, t8 = ztail.reshape(NB, R, B_TAIL), ztail.reshape(n8, SUBLANES, B_TAIL)
    base = [(EV_A_IN + i * B_WIDTH) // CW for i in range(3)]

    def cur(b):
        return pl.BlockSpec((1, R, CW), lambda i, c: (i, 0, b + c))

    def prev(b):
        return pl.BlockSpec((1, SUBLANES, CW), lambda i, c: (jnp.maximum(i * per - 1, 0), 0, b + c))

    def nxt(b):
        return pl.BlockSpec((1, SUBLANES, CW), lambda i, c: (jnp.minimum((i + 1) * per, n8 - 1), 0, b + c))

    t_cur = pl.BlockSpec((1, R, B_TAIL), lambda i, c: (i, 0, 0))
    t_prev = pl.BlockSpec((1, SUBLANES, B_TAIL), lambda i, c: (jnp.maximum(i * per - 1, 0), 0, 0))
    t_next = pl.BlockSpec((1, SUBLANES, B_TAIL), lambda i, c: (jnp.minimum((i + 1) * per, n8 - 1), 0, 0))

    mu = p['mu']
    mu_main = mu[:3 * B_WIDTH].reshape(1, 3 * B_WIDTH)
    mu_t = mu[3 * B_WIDTH:].reshape(1, B_TAIL)

    def mu_spec(i):
        return pl.BlockSpec((1, CW), lambda r, c: (0, i * (B_WIDTH // CW) + c))

    def padded(w, off):
        return jnp.zeros((B_TAIL, B_WIDTH), BF16).at[off:off + w.shape[0]].set(w.astype(BF16))

    lora = [padded(p['w2f'], _TAIL_OFF['wdf']), padded(p['w2b'], _TAIL_OFF['wdb']),
            padded(p['a2f'], _TAIL_OFF['adf']), padded(p['a2b'], _TAIL_OFF['adb']), padded(p['g2'], _TAIL_OFF['gd'])]
    lora_spec = pl.BlockSpec((B_TAIL, CW), lambda i, c: (0, c))
    rowp = jnp.stack([p['w0f'], p['w0b'], p['a0f'], p['a0b'], p['k_k'], p['k_a'], p['r_k'].reshape(-1),
                      jnp.zeros((B_WIDTH,), F32)])
    out_spec = pl.BlockSpec((1, R, CW), lambda i, c: (i, 0, c))
    names = ('w_f', 'b_f', 'kd_f', 'w_b', 'b_b', 'kd_b', 'r', 'kk', 'v', 'g', 'bonus')
    outs = pl.pallas_call(
        functools.partial(_prep_body, seqs=seqs),
        out_shape=tuple(jax.ShapeDtypeStruct((NB, R, B_WIDTH), F32) for _ in names),
        grid=(NB, B_WIDTH // CW),
        in_specs=[cur(base[0]), cur(base[1]), cur(base[2]), t_cur,
                  prev(base[0]), prev(base[1]), prev(base[2]), t_prev,
                  nxt(base[0]), nxt(base[1]), nxt(base[2]), t_next,
                  mu_spec(0), mu_spec(1), mu_spec(2), pl.BlockSpec((1, B_TAIL), lambda i, c: (0, 0)),
                  lora_spec, lora_spec, lora_spec, lora_spec, lora_spec,
                  pl.BlockSpec((SUBLANES, CW), lambda i, c: (0, c)),
                  pl.BlockSpec((LANES, LANES), lambda i, c: (0, 0))],
        out_specs=tuple(out_spec for _ in names),
        compiler_params=_params("parallel", "parallel"),
        name="rwkv_prep",
    )(zv, zv, zv, tv, z8, z8, z8, t8, z8, z8, z8, t8, mu_main, mu_main, mu_main, mu_t, *lora, rowp, _head_matrix())
    return dict(zip(names, outs))


def _rwkv_body(w_ref, b_ref, kd_ref, r_ref, kk_ref, v_ref, g_ref, h_ref, t0_ref, o_ref, tf_ref,
               s_ref, vb_ref, xo_ref, *, seqs, NB, passes, reverse):
    row_refs = (w_ref, b_ref, kd_ref, r_ref, kk_ref)
    TB = REC_TIME_BLOCK
    P, R, UW = s_ref.shape
    jb = pl.program_id(0)
    jb = (NB - 1 - jb) if reverse else jb
    pos, nblk, _ = seqs.locate(jb, TB)
    first = pos == ((nblk - 1) if reverse else 0)
    last = pos == (0 if reverse else (nblk - 1))

    @pl.when(first)
    def _():
        s_ref[...] = t0_ref[0]

    G = g_ref[...][None]
    Hm = h_ref[...]

    Q = REC_SUB_UNITS
    subs = [slice(q * Q, (q + 1) * Q) for q in range(P // Q)]

    def seg_sum(x, n_pass):
        x = x.reshape(Q * R, UW)
        hi = x.astype(BF16)
        acc = jnp.dot(hi, Hm, preferred_element_type=F32)
        rem = x
        for _ in range(n_pass - 1):
            rem = rem - hi.astype(F32)
            hi = rem.astype(BF16)
            acc = acc + jnp.dot(hi, Hm, preferred_element_type=F32)
        return acc.reshape(Q, R, UW)

    def unit_rows(tile, j, qs):
        return jnp.stack([tile[j:j + 1, p * UW:(p + 1) * UW] for p in range(qs.start, qs.stop)])

    sub_iota = lax.broadcasted_iota(jnp.int32, (SUBLANES, UW), 0)

    def group(i, carry):
        gi = (TB // SUBLANES - 1 - i) if reverse else i
        t8 = pl.multiple_of(gi * SUBLANES, SUBLANES)
        tiles = [r[0, pl.ds(t8, SUBLANES), :] for r in row_refs]
        v_tile = v_ref[0, pl.ds(t8, SUBLANES), :]
        for j in range(SUBLANES):
            for qs in subs:
                vb_ref[j, qs] = seg_sum(G * unit_rows(v_tile, j, qs), passes[0])
        for jj in range(SUBLANES):
            j = SUBLANES - 1 - jj if reverse else jj
            for qs in subs:
                w, b, kd, r, kk = [unit_rows(tile, j, qs) for tile in tiles]
                S = s_ref[qs]
                u = seg_sum(S * kk, passes[1])
                S = S * w - u * b + vb_ref[j, qs] * kd
                s_ref[qs] = S
                xo_ref[j, qs] = S * r
        for qs in subs:
            o_tiles = [jnp.zeros((SUBLANES, UW), F32) for _ in range(Q)]
            for j in range(SUBLANES):
                orow = jnp.sum(G * seg_sum(xo_ref[j, qs], passes[2]), axis=1, keepdims=True)
                o_tiles = [jnp.where(sub_iota == j, orow[p], o_tiles[p]) for p in range(Q)]
            for p in range(Q):
                lo = (qs.start + p) * UW
                o_ref[0, pl.ds(t8, SUBLANES), lo:lo + UW] = o_tiles[p]
        return carry

    lax.fori_loop(0, TB // SUBLANES, group, 0)

    @pl.when(last)
    def _():
        tf_ref[0] = s_ref[...]


def rwkv7(rows, v, s0, seqs, *, reverse, passes):
    T = v.shape[0] * v.shape[1]
    TB, UW, seg = REC_TIME_BLOCK, LANES, B_HEAD
    NB = T // TB
    hpu = UW // seg
    P = B_HEADS // hpu
    n_seq = s0.shape[0]
    t0 = s0.reshape(n_seq, P, hpu, seg, seg).transpose(0, 1, 3, 2, 4).reshape(n_seq, P, seg, UW)
    lane = jnp.arange(UW)
    g = (lane[None, :] % seg == jnp.arange(seg)[:, None]).astype(F32)

    def blk(j):
        return (NB - 1 - j) if reverse else j

    row_spec = pl.BlockSpec((1, TB, B_WIDTH), lambda j: (blk(j), 0, 0))
    st_spec = pl.BlockSpec((1, P, seg, UW), lambda j: (seqs.locate(blk(j), TB)[2], 0, 0, 0))
    o, s = pl.pallas_call(
        functools.partial(_rwkv_body, seqs=seqs, NB=NB, passes=passes, reverse=reverse),
        out_shape=(jax.ShapeDtypeStruct((NB, TB, B_WIDTH), F32), jax.ShapeDtypeStruct(t0.shape, F32)),
        grid=(NB,),
        in_specs=[row_spec] * 6 + [pl.BlockSpec((seg, UW), lambda j: (0, 0)),
                                   pl.BlockSpec((UW, UW), lambda j: (0, 0)), st_spec],
        out_specs=(row_spec, st_spec),
        scratch_shapes=[pltpu.VMEM((P, seg, UW), F32), pltpu.VMEM((SUBLANES, P, seg, UW), F32),
                        pltpu.VMEM((SUBLANES, P, seg, UW), F32)],
        compiler_params=_params("arbitrary"),
        name="rwkv7",
    )(*[x.reshape(NB, TB, B_WIDTH) for x in rows], v.reshape(NB, TB, B_WIDTH), g, _head_matrix(), t0)
    s = s.reshape(n_seq, P, seg, hpu, seg).transpose(0, 1, 3, 2, 4).reshape(s0.shape)
    return o.reshape(T, B_WIDTH), s


HG_STEP = 8


def _hgrn_body(q_ref, f_ref, v_ref, lb_ref, s0_ref, o_ref, sf_ref, s_ref, *, seqs, NB, Hh, reverse):
    C = HG_STEP
    TB = ROW_BLOCK
    jb = pl.program_id(1)
    jb = (NB - 1 - jb) if reverse else jb
    pos, nblk, _ = seqs.locate(jb, TB)
    first = pos == ((nblk - 1) if reverse else 0)
    last = pos == (0 if reverse else (nblk - 1))

    @pl.when(first)
    def _():
        s_ref[...] = s0_ref[0]

    row = lax.broadcasted_iota(jnp.int32, (C, LANES), 0)
    def shift(x, d):
        if d == 0:
            return x
        return pltpu.roll(x, (C - d) if reverse else d, axis=0)

    def has_source(d):
        return (row <= C - 1 - d) if reverse else (row >= d)

    def block(i, carry):
        ci = (TB // C - 1 - i) if reverse else i
        t0 = pl.multiple_of(ci * C, C)
        for h in range(Hh):
            sl = slice(h * LANES, (h + 1) * LANES)
            hq = q_ref[0, pl.ds(t0, C), sl]
            q = hq * jax.nn.sigmoid(hq) * A_DK ** -0.5
            lb = lb_ref[:, sl]
            f = lb + (1.0 - lb) * jax.nn.sigmoid(f_ref[0, pl.ds(t0, C), sl])
            v = v_ref[0, pl.ds(t0, C), sl]
            k = 1.0 - f
            b = jnp.log(f)
            s = 1
            while s < C:
                b = b + jnp.where(has_source(s), shift(b, s), 0.0)
                s *= 2
            st = s_ref[h]
            stb = st.astype(BF16)
            o = lax.dot_general((q * jnp.exp(b)).astype(BF16), stb, (((1,), (1,)), ((), ())),
                                preferred_element_type=F32)
            for d in range(C):
                arg = jnp.where(has_source(d), b - shift(b, d), -1e30)
                a = jnp.sum(q * shift(k, d) * jnp.exp(arg), axis=1, keepdims=True)
                o = o + a * shift(v, d)
            o_ref[0, pl.ds(t0, C), sl] = o
            end = 0 if reverse else C - 1
            bl = b[end:end + 1, :]
            kt = k * jnp.exp(bl - b)
            s_ref[h] = st * jnp.exp(bl) + lax.dot_general(v.astype(BF16), kt.astype(BF16), (((0,), (0,)), ((), ())),
                                                         preferred_element_type=F32)
        return carry

    lax.fori_loop(0, TB // C, block, 0)

    @pl.when(last)
    def _():
        sf_ref[0] = s_ref[...]


def hgrn2(zv, cols, lb, s0, seqs, *, reverse, heads_per_step=8):
    NB, TB, _ = zv.shape
    W = A_WIDTH
    H = W // LANES
    Hh = heads_per_step
    bw = Hh * LANES
    assert TB == ROW_BLOCK and H % Hh == 0 and all(c % bw == 0 for c in cols)

    def blk(j):
        return (NB - 1 - j) if reverse else j

    def in_spec(col):
        return pl.BlockSpec((1, TB, bw), lambda u, j: (blk(j), 0, col // bw + u))

    st_spec = pl.BlockSpec((1, Hh, LANES, LANES), lambda u, j: (seqs.locate(blk(j), TB)[2], u, 0, 0))
    st0 = jnp.swapaxes(s0, -1, -2)
    o, s = pl.pallas_call(
        functools.partial(_hgrn_body, seqs=seqs, NB=NB, Hh=Hh, reverse=reverse),
        out_shape=(jax.ShapeDtypeStruct((NB, TB, W), F32), jax.ShapeDtypeStruct(st0.shape, F32)),
        grid=(H // Hh, NB),
        in_specs=[in_spec(cols[0]), in_spec(cols[1]), in_spec(cols[2]),
                  pl.BlockSpec((1, bw), lambda u, j: (0, u)), st_spec],
        out_specs=(pl.BlockSpec((1, TB, bw), lambda u, j: (blk(j), 0, u)), st_spec),
        scratch_shapes=[pltpu.VMEM((Hh, LANES, LANES), F32)],
        compiler_params=_params("parallel", "arbitrary"),
        name="hgrn2",
    )(zv, zv, zv, lb, st0)
    return o.reshape(NB * TB, W), jnp.swapaxes(s, -1, -2)


def _even_post_body(oaf_ref, oab_ref, hg_ref, obf_ref, obb_ref, bonus_ref, g_ref, rowp_ref, hseg_ref, o_ref):
    hg_g, ln_g, ln_b = [rowp_ref[i:i + 1, :] for i in range(3)]
    oa = oaf_ref[0] + oab_ref[0]
    hg = hg_ref[0]
    oa = oa * lax.rsqrt(jnp.mean(oa * oa, axis=-1, keepdims=True) + NORM_EPS) * hg_g * (hg * jax.nn.sigmoid(hg))
    o_ref[0, :, :A_WIDTH] = oa.astype(o_ref.dtype)
    hseg = hseg_ref[...]
    ob = obf_ref[0] + obb_ref[0]
    cen = ob - _head_sums(ob, hseg) * (1.0 / B_HEAD)
    var = _head_sums(cen * cen, hseg) * (1.0 / B_HEAD)
    ob = (cen * lax.rsqrt(var + B_GN_EPS) * ln_g + ln_b + bonus_ref[0]) * g_ref[0]
    o_ref[0, :, A_WIDTH:] = ob.astype(o_ref.dtype)


def even_post(oa_f, oa_b, zv, ob_f, ob_b, bonus, g, p):
    NB, R, _ = zv.shape
    T = NB * R
    W = A_WIDTH
    assert W == B_WIDTH and (4 * A_WIDTH) % W == 0
    spec = pl.BlockSpec((1, R, W), lambda i: (i, 0, 0))
    rowp = jnp.stack([p['hg_g'], p['ln_g'], p['ln_b']] + [jnp.zeros((W,), F32)] * 5)
    v3 = lambda x: x.reshape(NB, R, W)
    out = pl.pallas_call(
        _even_post_body,
        out_shape=jax.ShapeDtypeStruct((NB, R, 2 * W), BF16),
        grid=(NB,),
        in_specs=[spec, spec, pl.BlockSpec((1, R, W), lambda i: (i, 0, 4 * A_WIDTH // W)), spec, spec, spec, spec,
                  pl.BlockSpec((SUBLANES, W), lambda i: (0, 0)), pl.BlockSpec((LANES, LANES), lambda i: (0, 0))],
        out_specs=pl.BlockSpec((1, R, 2 * W), lambda i: (i, 0, 0)),
        compiler_params=_params("parallel"),
        name="even_post",
    )(v3(oa_f), v3(oa_b), zv, v3(ob_f), v3(ob_b), bonus, g, rowp, _head_matrix())
    return out.reshape(T, 2 * W)


def even_mixer(z, seqs, st, p):
    T = z.shape[0]
    zv = z.reshape(T // ROW_BLOCK, ROW_BLOCK, -1)
    oa_f, sf = hgrn2(zv, (0, A_WIDTH, 3 * A_WIDTH), p['lb_f'][None], st['hf'], seqs, reverse=False)
    oa_b, sb = hgrn2(zv, (0, 2 * A_WIDTH, 3 * A_WIDTH), p['lb_b'][None], st['hb'], seqs, reverse=True)
    q = rwkv_prep(z, seqs, p)
    ob_f, rf = rwkv7((q['w_f'], q['b_f'], q['kd_f'], q['r'], q['kk']), q['v'], st['rf'], seqs, reverse=False,
                     passes=REC_PASSES)
    ob_b, rb = rwkv7((q['w_b'], q['b_b'], q['kd_b'], q['r'], q['kk']), q['v'], st['rb'], seqs, reverse=True,
                     passes=REC_PASSES)
    return even_post(oa_f, oa_b, zv, ob_f, ob_b, q['bonus'], q['g'], p), (sf, sb, rf, rb)


def _swap_halves(x):
    lane = lax.broadcasted_iota(jnp.int32, x.shape, 1)
    quarter = ROPE_AXIS_DIM // 2
    return jnp.where(lane % ROPE_AXIS_DIM < quarter, pltpu.roll(x, LANES - quarter, axis=1),
                     pltpu.roll(x, quarter, axis=1))


def _attn_body(*refs, has_ctx):
    if has_ctx:
        (q_ref, kn_ref, vn_ref, ck_ref, cv_ref, qc_ref, qs_ref, kc_ref, ks_ref, qg_ref, kg_ref,
         o_ref, k_sc, v_sc) = refs
    else:
        q_ref, kn_ref, vn_ref, qg_ref, kg_ref, o_ref, ko_ref, vo_ref, k_sc, v_sc = refs

    def norm(x, g_ref):
        return x * lax.rsqrt(jnp.mean(x * x, axis=-1, keepdims=True) + NORM_EPS) * g_ref[...]

    @pl.when(pl.program_id(2) == 0)
    def _():
        k = norm(kn_ref[0], kg_ref)
        v = vn_ref[0]
        if has_ctx:
            n_past = ck_ref.shape[1]
            k = k * kc_ref[...] + _swap_halves(k) * ks_ref[...]
            k_sc[:n_past] = ck_ref[0].astype(BF16)
            v_sc[:n_past] = cv_ref[0].astype(BF16)
            k_sc[n_past:] = k.astype(BF16)
            v_sc[n_past:] = v.astype(BF16)
        else:
            ko_ref[0] = k
            vo_ref[0] = v
            k_sc[...] = k.astype(BF16)
            v_sc[...] = v.astype(BF16)

    k = k_sc[...]
    v = v_sc[...]
    scale = C_HEAD_DIM ** -0.5
    for g in range(C_GROUP):
        sl = slice(g * C_HEAD_DIM, (g + 1) * C_HEAD_DIM)
        q = norm(q_ref[0, :, sl], qg_ref)
        if has_ctx:
            q = q * qc_ref[...] + _swap_halves(q) * qs_ref[...]
        s = lax.dot_general(q.astype(BF16), k, (((1,), (1,)), ((), ())), preferred_element_type=F32) * scale
        m = jnp.max(s, axis=-1, keepdims=True)
        e = jnp.exp(s - m)
        p = e / jnp.sum(e, axis=-1, keepdims=True)
        o_ref[0, :, sl] = jnp.dot(p.astype(BF16), v, preferred_element_type=F32).astype(o_ref.dtype)


def _rope_tables(L):
    n_rows = L // GRID_W
    row = jnp.repeat(jnp.arange(n_rows), GRID_W).astype(F32)
    col = jnp.tile(jnp.arange(GRID_W), n_rows).astype(F32)
    half = ROPE_AXIS_DIM // 2
    inv = ROPE_THETA ** (-jnp.arange(half, dtype=F32) / half)
    ar, ac = row[:, None] * inv, col[:, None] * inv
    cos_t = jnp.concatenate([jnp.cos(ar), jnp.cos(ar), jnp.cos(ac), jnp.cos(ac)], axis=1)
    sin_t = jnp.concatenate([-jnp.sin(ar), jnp.sin(ar), -jnp.sin(ac), jnp.sin(ac)], axis=1)
    return cos_t, sin_t


def attention(z, row0, B, L, qn_g, kn_g, ctx_k=None, ctx_v=None):
    T = z.shape[0]
    assert row0 % L == 0 and L % Q_BLOCK == 0
    has_ctx = ctx_k is not None
    gw = C_GROUP * C_HEAD_DIM
    zq = z.reshape(T // Q_BLOCK, Q_BLOCK, -1)
    zl = z.reshape(T // L, L, -1)
    qb0, lb0 = row0 // Q_BLOCK, row0 // L
    nq = L // Q_BLOCK
    kcol, vcol = C_WIDTH // C_HEAD_DIM, (C_WIDTH + C_KV_WIDTH) // C_HEAD_DIM
    g_spec = pl.BlockSpec((1, C_HEAD_DIM), lambda b, h, i: (0, 0))
    in_specs = [pl.BlockSpec((1, Q_BLOCK, gw), lambda b, h, i: (qb0 + b * nq + i, 0, h)),
                pl.BlockSpec((1, L, C_HEAD_DIM), lambda b, h, i: (lb0 + b, 0, kcol + h)),
                pl.BlockSpec((1, L, C_HEAD_DIM), lambda b, h, i: (lb0 + b, 0, vcol + h))]
    args = [zq, zl, zl]
    o_spec = pl.BlockSpec((1, Q_BLOCK, gw), lambda b, h, i: (b, i, h))
    o_shape = jax.ShapeDtypeStruct((B, L, C_WIDTH), BF16)
    if has_ctx:
        n_past = ctx_k.shape[1]
        c_spec = pl.BlockSpec((1, n_past, C_HEAD_DIM), lambda b, h, i: (b, 0, h))
        cos_t, sin_t = _rope_tables(L)
        q_tab = pl.BlockSpec((Q_BLOCK, C_HEAD_DIM), lambda b, h, i: (i, 0))
        k_tab = pl.BlockSpec((L, C_HEAD_DIM), lambda b, h, i: (0, 0))
        in_specs += [c_spec, c_spec, q_tab, q_tab, k_tab, k_tab]
        args += [ctx_k.reshape(B, n_past, C_KV_WIDTH), ctx_v.reshape(B, n_past, C_KV_WIDTH), cos_t, sin_t, cos_t, sin_t]
        out_shape, out_specs = o_shape, o_spec
        Lk = n_past + L
    else:
        kv_spec = pl.BlockSpec((1, L, C_HEAD_DIM), lambda b, h, i: (b, 0, h))
        kv_shape = jax.ShapeDtypeStruct((B, L, C_KV_WIDTH), F32)
        out_shape, out_specs = (o_shape, kv_shape, kv_shape), (o_spec, kv_spec, kv_spec)
        Lk = L
    in_specs += [g_spec, g_spec]
    args += [qn_g.reshape(1, C_HEAD_DIM), kn_g.reshape(1, C_HEAD_DIM)]
    res = pl.pallas_call(
        functools.partial(_attn_body, has_ctx=has_ctx),
        out_shape=out_shape, grid=(B, C_KV_HEADS, nq), in_specs=in_specs, out_specs=out_specs,
        scratch_shapes=[pltpu.VMEM((Lk, C_HEAD_DIM), BF16), pltpu.VMEM((Lk, C_HEAD_DIM), BF16)],
        compiler_params=_params("parallel", "parallel", "arbitrary"),
        name="attn",
    )(*args)
    if has_ctx:
        return res, None, None
    o, k, v = res
    return o, k.reshape(B, L, C_KV_HEADS, C_HEAD_DIM), v.reshape(B, L, C_KV_HEADS, C_HEAD_DIM)


def _dft_body(ah_ref, al_ref, u_ref, o_ref):
    u = u_ref[0]
    uh = u.astype(BF16)
    ul = (u - uh.astype(F32)).astype(BF16)
    ah = ah_ref[...]
    o_ref[0] = (jnp.dot(ah, uh, preferred_element_type=F32) + jnp.dot(ah, ul, preferred_element_type=F32)
                + jnp.dot(al_ref[...], uh, preferred_element_type=F32))


def dft_mm(a, u, *, bn=512):
    ah, al = a
    M, K = ah.shape
    B, _, C = u.shape
    return pl.pallas_call(
        _dft_body,
        out_shape=jax.ShapeDtypeStruct((B, M, C), F32),
        grid=(B, C // bn),
        in_specs=[pl.BlockSpec((M, K), lambda b, j: (0, 0)), pl.BlockSpec((M, K), lambda b, j: (0, 0)),
                  pl.BlockSpec((1, K, bn), lambda b, j: (b, 0, j))],
        out_specs=pl.BlockSpec((1, M, bn), lambda b, j: (b, 0, j)),
        compiler_params=_params("parallel", "parallel"),
        name="dft_mm",
    )(ah, al, u)


def _split_bf16(a):
    hi = a.astype(BF16)
    return hi, (a - hi.astype(F32)).astype(BF16)


def _dft_matrix(L):
    N = 2 * L
    r = jnp.arange(N, dtype=jnp.int32)[:, None]
    s = jnp.arange(L, dtype=jnp.int32)[None, :]
    is_cos = r <= L
    f = jnp.where(is_cos, r, r - L)
    ang = (2.0 * math.pi / N) * ((f * s) % N).astype(F32)
    return jnp.where(is_cos, jnp.cos(ang), jnp.sin(ang))


def _centred_long_conv(u, h, bias, fmat, fmat_t):
    L = u.shape[1]
    N = 2 * L
    h_mod = jnp.concatenate([h[:1], 2.0 * h[1:]], axis=0)
    kf = dft_mm(fmat, h_mod[None])[0]
    r = jnp.arange(N)
    f = jnp.where(r <= L, r, r - L)
    herm = jnp.where((r == 0) | (r == L), 1.0, 2.0).astype(F32) / N
    scale = kf[f] * herm[:, None]
    uf = dft_mm(fmat, u)
    y = dft_mm(fmat_t, uf * scale[None])
    return y + u * bias


def _hyena_filters(L, w1, b1, w2, b2, w3, b3, w4, freq):
    t = jnp.linspace(0.0, 1.0, L, dtype=F32)[:, None]
    w_pos = 2.0 * math.pi * jnp.arange(L, dtype=F32)[:, None] / L
    bands = jnp.linspace(1e-4, HY_BANDS - 1, HY_BANDS, dtype=F32)[None]
    zpos = jnp.concatenate([t, jnp.cos(bands * w_pos), -jnp.sin(bands * w_pos)], axis=-1)
    f = freq.astype(F32)
    hdn = jnp.sin(f * (zpos @ w1 + b1))
    hdn = jnp.sin(f * (hdn @ w2 + b2))
    hdn = jnp.sin(f * (hdn @ w3 + b3))
    h = mm(hdn.astype(BF16), w4.astype(BF16), bm=L, bn=1024)
    deltas = jnp.abs(jnp.linspace(HY_MIN_DECAY, HY_MAX_DECAY, HY_ORDER * HY_W, dtype=F32))
    h = h * jnp.exp(-t * deltas)
    return h.reshape(L, HY_ORDER, HY_W)


def _neighbours(u):
    up = jnp.pad(u, ((0, 0), (1, 1), (0, 0)))
    return up[:, :-2], up[:, 2:]


def _odd_core(z, row0, B, L, ctx_k, ctx_v, p):
    o_c, k, v = attention(z, row0, B, L, p['qn_g'], p['kn_g'], ctx_k, ctx_v)
    u = z[row0:row0 + B * L, C_IN:].reshape(B, L, -1)
    prev, nxt = _neighbours(u)
    u = prev * p['conv_w'][0] + u * p['conv_w'][1] + nxt * p['conv_w'][2] + p['conv_b']
    hv, x1, x2 = jnp.split(u, 3, axis=-1)
    filt = _hyena_filters(L, p['f_w1'], p['f_b1'], p['f_w2'], p['f_b2'], p['f_w3'], p['f_b3'], p['f_w4'],
                          p['sin_freq'])
    fmat = _dft_matrix(L)
    fmat, fmat_t = _split_bf16(fmat), _split_bf16(fmat.T)
    z1 = x1 * _centred_long_conv(hv, filt[:, 0], p['hy_bias'][0], fmat, fmat_t)
    o_d = x2 * _centred_long_conv(z1, filt[:, 1], p['hy_bias'][1], fmat, fmat_t)
    return jnp.concatenate([o_c, o_d.astype(BF16)], axis=-1), k, v


def kernel(x_prompt, x_sample, c, state_hgrn_fwd, state_hgrn_bwd, state_rwkv_fwd, state_rwkv_bwd, cache_k, cache_v, c_ctx, ada_w, ada_b, norm_g, ffn_w1, ffn_w3, ffn_w2, final_norm_g, ev_w_in, ev_w_out, hg_lb_fwd, hg_lb_bwd, hg_norm_g, rw_mu, rw_w0_f, rw_w2_f, rw_w0_b, rw_w2_b, rw_a0_f, rw_a2_f, rw_a0_b, rw_a2_b, rw_g2, rw_kk, rw_ka, rw_rk, rw_ln_g, rw_ln_b, od_w_in, od_w_out, at_qn_g, at_kn_g, hy_conv_w, hy_conv_b, hy_f_w1, hy_f_b1, hy_f_w2, hy_f_b2, hy_f_w3, hy_f_b3, hy_f_w4, hy_sin_freq, hy_bias):
    Bc, Lc, _ = x_prompt.shape
    Bl, Ll, _ = x_sample.shape
    seqs = Seqs(Bc, Lc, Bl, Ll)
    n_ctx_tok = Bc * Lc
    n_tok = seqs.n_tok
    assert Lc % ROW_BLOCK == 0 and Ll % ROW_BLOCK == 0 and n_ctx_tok % Ll == 0

    def mod_row(i):
        return jnp.maximum(i * ROW_BLOCK // Ll - (n_ctx_tok // Ll - 1), 0)

    h = jnp.concatenate([x_prompt.reshape(n_ctx_tok, D_MODEL), x_sample.reshape(Bl * Ll, D_MODEL)], axis=0)
    cvec = jnp.concatenate([c_ctx[None], c], axis=0)
    cvec = jnp.pad(jax.nn.silu(cvec), ((0, SUBLANES - (1 + Bl)), (0, 0))).astype(BF16)
    mods = [(mm_small(cvec, ada_w[l], bn=1024) + ada_b[l])[:1 + Bl].reshape(1 + Bl, N_MOD, D_MODEL)
            for l in range(DEPTH)]

    lb_fwd_all = jnp.cumsum(jax.nn.softmax(hg_lb_fwd, axis=0)[1:], axis=0)
    lb_bwd_all = jnp.cumsum(jax.nn.softmax(hg_lb_bwd, axis=0)[1:], axis=0)

    def with_ctx_zeros(s):
        return jnp.concatenate([jnp.zeros((Bc,) + s.shape[1:], F32), s], axis=0)

    def ffn(x, l, s):
        mid = ffn_up(x, ffn_w1[l, s], ffn_w3[l, s])
        return mm(mid, ffn_w2[l, s].astype(BF16), bm=1024, bn=512, bk=D_FF // 2)

    new_hf, new_hb, new_rf, new_rb, new_k, new_v = [], [], [], [], [], []
    x = ada_step(h, norm_g[0, 0], mod_row=mod_row, m_in=mods[0], i_in=0)
    y_out = None
    for l in range(DEPTH):
        m = mods[l]
        h, xn = ada_step(h, norm_g[l, 1], mod_row=mod_row, y=ffn(x, l, 0), m_out=m, i_out=0, w_out=0.5,
                         m_in=m, i_in=1)
        j = l // 2
        if l % 2 == 0:
            p = dict(lb_f=lb_fwd_all[j], lb_b=lb_bwd_all[j], hg_g=hg_norm_g[j], mu=rw_mu[j],
                     w0f=rw_w0_f[j], w2f=rw_w2_f[j], w0b=rw_w0_b[j], w2b=rw_w2_b[j],
                     a0f=rw_a0_f[j], a2f=rw_a2_f[j], a0b=rw_a0_b[j], a2b=rw_a2_b[j], g2=rw_g2[j],
                     k_k=rw_kk[j], k_a=rw_ka[j], r_k=rw_rk[j], ln_g=rw_ln_g[j], ln_b=rw_ln_b[j])
            st = dict(hf=with_ctx_zeros(state_hgrn_fwd[:, j]), hb=with_ctx_zeros(state_hgrn_bwd[:, j]),
                      rf=with_ctx_zeros(state_rwkv_fwd[:, j]), rb=with_ctx_zeros(state_rwkv_bwd[:, j]))
            z = mm_ws(xn, ev_w_in[j], bm=512, bn=896)
            o, (s_hf, s_hb, s_rf, s_rb) = even_mixer(z, seqs, st, p)
            new_hf.append(s_hf[:Bc])
            new_hb.append(s_hb[:Bc])
            new_rf.append(s_rf[:Bc])
            new_rb.append(s_rb[:Bc])
            w_out = ev_w_out[j]
        else:
            p = dict(qn_g=at_qn_g[j], kn_g=at_kn_g[j], conv_w=hy_conv_w[j], conv_b=hy_conv_b[j],
                     f_w1=hy_f_w1[j], f_b1=hy_f_b1[j], f_w2=hy_f_w2[j], f_b2=hy_f_b2[j], f_w3=hy_f_w3[j],
                     f_b3=hy_f_b3[j], f_w4=hy_f_w4[j], sin_freq=hy_sin_freq[j], hy_bias=hy_bias[j])
            z = mm_ws(xn, od_w_in[j], bm=1024, bn=512)
            o_ctx, k_ctx, v_ctx = _odd_core(z, 0, Bc, Lc, None, None, p)
            o_lat, _, _ = _odd_core(z, n_ctx_tok, Bl, Ll, cache_k[:, j], cache_v[:, j], p)
            new_k.append(k_ctx)
            new_v.append(v_ctx)
            o = jnp.concatenate([o_ctx.reshape(n_ctx_tok, -1), o_lat.reshape(Bl * Ll, -1)], axis=0)
            w_out = od_w_out[j]
        h, x = ada_step(h, norm_g[l, 2], mod_row=mod_row, y=mm_ws(o, w_out, bm=1024, bn=512), m_out=m, i_out=1,
                        w_out=1.0, m_in=m, i_in=2)
        y = ffn(x, l, 1)
        if l + 1 < DEPTH:
            h, x = ada_step(h, norm_g[l + 1, 0], mod_row=mod_row, y=y, m_out=m, i_out=2, w_out=0.5,
                            m_in=mods[l + 1], i_in=0)
        else:
            h, y_out = ada_step(h, final_norm_g, mod_row=mod_row, y=y, m_out=m, i_out=2, w_out=0.5, out_dtype=F32)
    return (y_out[:n_ctx_tok].reshape(Bc, Lc, D_MODEL), y_out[n_ctx_tok:].reshape(Bl, Ll, D_MODEL),
            jnp.stack(new_hf, axis=1), jnp.stack(new_hb, axis=1), jnp.stack(new_rf, axis=1),
            jnp.stack(new_rb, axis=1), jnp.stack(new_k, axis=1), jnp.stack(new_v, axis=1))
```

```python
import functools
import math
from typing import NamedTuple

import numpy as np
import jax
import jax.numpy as jnp
from jax import lax
from jax.experimental import pallas as pl
from jax.experimental.pallas import tpu as pltpu

F32 = jnp.float32
BF16 = jnp.bfloat16

D_MODEL = 4096
DEPTH = 2
GRID_W = 64
N_MOD = 9
D_FF = 11008
NORM_EPS = 1e-6
A_WIDTH = D_MODEL // 2
A_HEADS = 16
A_DK = A_WIDTH // A_HEADS
A_DV = A_WIDTH // A_HEADS
B_WIDTH = D_MODEL // 2
B_HEAD = 64
B_HEADS = B_WIDTH // B_HEAD
B_DECAY_LORA = max(32, int(round(1.8 * B_WIDTH ** 0.5 / 32)) * 32)
B_AAA_LORA = max(32, int(round(1.8 * B_WIDTH ** 0.5 / 32)) * 32)
B_GATE_LORA = max(32, int(round(0.6 * B_WIDTH ** 0.8 / 32)) * 32)
B_TAIL = 2 * B_DECAY_LORA + 2 * B_AAA_LORA + B_GATE_LORA
B_GN_EPS = 64e-5
EV_A_IN = 5 * A_WIDTH
EV_B_IN = 3 * B_WIDTH + B_TAIL
EV_MAIN = EV_A_IN + 3 * B_WIDTH
C_HEAD_DIM = 128
C_HEADS = 16
C_KV_HEADS = 4
C_GROUP = C_HEADS // C_KV_HEADS
C_WIDTH = C_HEADS * C_HEAD_DIM
C_KV_WIDTH = C_KV_HEADS * C_HEAD_DIM
C_IN = C_WIDTH + 2 * C_KV_WIDTH
ROPE_AXIS_DIM = C_HEAD_DIM // 2
ROPE_THETA = 10000.0
Q_BLOCK = 128
HY_W = D_MODEL // 2
HY_ORDER = 2
HY_BANDS = 16
HY_TARGET = 1e-2
HY_MIN_DECAY = math.log(HY_TARGET) / 1.5
HY_MAX_DECAY = math.log(HY_TARGET) / 0.3

V7X_VMEM_LIMIT_BYTES = 56 * 1024 * 1024
LANES = 128
SUBLANES = 8
ROW_BLOCK = 256
REC_TIME_BLOCK = 128
REC_PASSES = (1, 1, 1)
REC_SUB_UNITS = 4
PREP_COLS = 512


def _params(*sem):
    return pltpu.CompilerParams(dimension_semantics=sem, vmem_limit_bytes=V7X_VMEM_LIMIT_BYTES)


class Seqs(NamedTuple):
    Bc: int
    Lc: int
    Bl: int
    Ll: int

    @property
    def n_tok(self):
        return self.Bc * self.Lc + self.Bl * self.Ll

    def locate(self, j, blk):
        n_ctx = self.Bc * self.Lc // blk
        cb, lb = self.Lc // blk, self.Ll // blk
        is_ctx = j < n_ctx
        jl = jnp.maximum(j - n_ctx, 0)
        pos = jnp.where(is_ctx, j % cb, jl % lb)
        nblk = jnp.where(is_ctx, cb, lb)
        sid = jnp.where(is_ctx, j // cb, self.Bc + jl // lb)
        return pos, nblk, sid


def _mm_body(x_ref, w_ref, o_ref, *, nk):
    part = jnp.dot(x_ref[...], w_ref[...], preferred_element_type=F32)
    if nk == 1:
        o_ref[...] = part.astype(o_ref.dtype)
    else:
        k = pl.program_id(2)

        @pl.when(k == 0)
        def _():
            o_ref[...] = part

        @pl.when(k > 0)
        def _():
            o_ref[...] += part


def _stacked(w_index):
    return (None,) * len(w_index)


def mm(x, w, *, bm, bn, bk=None, w_index=(), out_dtype=F32):
    M, K = x.shape
    N = w.shape[-1]
    bk = K if bk is None else bk
    nk = K // bk
    assert M % bm == 0 and N % bn == 0 and K % bk == 0
    assert nk == 1 or out_dtype == F32
    return pl.pallas_call(
        functools.partial(_mm_body, nk=nk),
        out_shape=jax.ShapeDtypeStruct((M, N), out_dtype),
        grid=(M // bm, N // bn, nk),
        in_specs=[pl.BlockSpec((bm, bk), lambda i, j, k: (i, k)),
                  pl.BlockSpec(_stacked(w_index) + (bk, bn), lambda i, j, k: w_index + (k, j))],
        out_specs=pl.BlockSpec((bm, bn), lambda i, j, k: (i, j)),
        compiler_params=_params("parallel", "parallel", "arbitrary"),
        name="mm",
    )(x, w)


def _mm_ws_body(x_ref, w_ref, o_ref, wb_ref):
    @pl.when(pl.program_id(1) == 0)
    def _():
        wb_ref[...] = w_ref[...].astype(BF16)

    o_ref[...] = jnp.dot(x_ref[...], wb_ref[...], preferred_element_type=F32).astype(o_ref.dtype)


def mm_ws(x, w, *, bm, bn, w_index=(), n_cols=None, out_dtype=F32):
    M, K = x.shape
    N = w.shape[-1] if n_cols is None else n_cols
    assert M % bm == 0 and N % bn == 0
    return pl.pallas_call(
        _mm_ws_body,
        out_shape=jax.ShapeDtypeStruct((M, N), out_dtype),
        grid=(N // bn, M // bm),
        in_specs=[pl.BlockSpec((bm, K), lambda j, i: (i, 0)),
                  pl.BlockSpec(_stacked(w_index) + (K, bn), lambda j, i: w_index + (0, j))],
        out_specs=pl.BlockSpec((bm, bn), lambda j, i: (i, j)),
        scratch_shapes=[pltpu.VMEM((K, bn), BF16)],
        compiler_params=_params("arbitrary", "arbitrary"),
        name="mm_ws",
    )(x, w)


def _mm_small_body(x_ref, w_ref, o_ref):
    o_ref[...] = jnp.dot(x_ref[...], w_ref[...].astype(BF16), preferred_element_type=F32)


def mm_small(x, w, *, bn, w_index=()):
    M, K = x.shape
    N = w.shape[-1]
    return pl.pallas_call(
        _mm_small_body,
        out_shape=jax.ShapeDtypeStruct((M, N), F32),
        grid=(N // bn,),
        in_specs=[pl.BlockSpec((M, K), lambda j: (0, 0)),
                  pl.BlockSpec(_stacked(w_index) + (K, bn), lambda j: w_index + (0, j))],
        out_specs=pl.BlockSpec((M, bn), lambda j: (0, j)),
        compiler_params=_params("parallel"),
        name="mm_small",
    )(x, w)


def _ffn_up_body(x_ref, w1_ref, w3_ref, o_ref, w1b_ref, w3b_ref):
    @pl.when(pl.program_id(1) == 0)
    def _():
        w1b_ref[...] = w1_ref[...].astype(BF16)
        w3b_ref[...] = w3_ref[...].astype(BF16)

    x = x_ref[...]
    a = jnp.dot(x, w1b_ref[...], preferred_element_type=F32)
    b = jnp.dot(x, w3b_ref[...], preferred_element_type=F32)
    o_ref[...] = (a * jax.nn.sigmoid(a) * b).astype(o_ref.dtype)


def ffn_up(x, w1, w3, *, w_index=(), bm=1024, bn=256):
    M, K = x.shape
    N = w1.shape[-1]
    w_spec = pl.BlockSpec(_stacked(w_index) + (K, bn), lambda j, i: w_index + (0, j))
    return pl.pallas_call(
        _ffn_up_body,
        out_shape=jax.ShapeDtypeStruct((M, N), BF16),
        grid=(N // bn, M // bm),
        in_specs=[pl.BlockSpec((bm, K), lambda j, i: (i, 0)), w_spec, w_spec],
        out_specs=pl.BlockSpec((bm, bn), lambda j, i: (i, j)),
        scratch_shapes=[pltpu.VMEM((K, bn), BF16), pltpu.VMEM((K, bn), BF16)],
        compiler_params=_params("arbitrary", "arbitrary"),
        name="ffn_up",
    )(x, w1, w3)


def _ada_body(*refs, has_y, has_mod, i_out, w_out, i_in):
    refs = list(refs)
    h = refs.pop(0)[...]
    if has_y:
        y_ref = refs.pop(0)
        mo_ref = refs.pop(0)
        h = h + w_out * mo_ref[0, 3 * i_out + 2:3 * i_out + 3, :] * y_ref[...]
    g_ref = refs.pop(0)
    x = h * lax.rsqrt(jnp.mean(h * h, axis=-1, keepdims=True) + NORM_EPS) * g_ref[...]
    if has_mod:
        mi_ref = refs.pop(0)
        x = x * (1.0 + mi_ref[0, 3 * i_in + 1:3 * i_in + 2, :]) + mi_ref[0, 3 * i_in:3 * i_in + 1, :]
    if has_y:
        refs.pop(0)[...] = h
    x_ref = refs.pop(0)
    x_ref[...] = x.astype(x_ref.dtype)


def ada_step(h, g, *, mod_row, y=None, m_out=None, i_out=0, w_out=1.0, m_in=None, i_in=0, out_dtype=BF16):
    T, D = h.shape
    R = ROW_BLOCK
    has_y, has_mod = y is not None, m_in is not None
    row_spec = pl.BlockSpec((R, D), lambda i: (i, 0))
    mod_spec = pl.BlockSpec((1, N_MOD, D), lambda i: (mod_row(i), 0, 0))
    args, specs = [h], [row_spec]
    if has_y:
        args += [y, m_out]
        specs += [row_spec, mod_spec]
    args.append(g.reshape(1, D))
    specs.append(pl.BlockSpec((1, D), lambda i: (0, 0)))
    if has_mod:
        args.append(m_in)
        specs.append(mod_spec)
    out_shape, out_specs = [jax.ShapeDtypeStruct((T, D), out_dtype)], [row_spec]
    if has_y:
        out_shape.insert(0, jax.ShapeDtypeStruct((T, D), F32))
        out_specs.insert(0, row_spec)
    res = pl.pallas_call(
        functools.partial(_ada_body, has_y=has_y, has_mod=has_mod, i_out=i_out, w_out=w_out, i_in=i_in),
        out_shape=tuple(out_shape), grid=(T // R,), in_specs=specs, out_specs=tuple(out_specs),
        compiler_params=_params("parallel"), name="ada_step",
    )(*args)
    return res if has_y else res[0]


def _head_sums(x, hseg, n_pass=2):
    outs = []
    for c in range(x.shape[1] // LANES):
        rem = x[:, c * LANES:(c + 1) * LANES]
        acc = None
        for _ in range(n_pass):
            hi = rem.astype(BF16)
            part = jnp.dot(hi, hseg, preferred_element_type=F32)
            acc = part if acc is None else acc + part
            rem = rem - hi.astype(F32)
        outs.append(acc)
    return outs[0] if len(outs) == 1 else jnp.concatenate(outs, axis=1)


def _head_matrix():
    lane = jnp.arange(LANES)
    return (lane[:, None] // B_HEAD == lane[None, :] // B_HEAD).astype(BF16)


_TAIL_OFF = dict(wdf=0, wdb=B_DECAY_LORA, adf=2 * B_DECAY_LORA, adb=2 * B_DECAY_LORA + B_AAA_LORA,
                 gd=2 * B_DECAY_LORA + 2 * B_AAA_LORA)


def _prep_body(r_ref, k_ref, v_ref, t_ref, rp_ref, kp_ref, vp_ref, tp_ref, rn_ref, kn_ref, vn_ref, tn_ref,
               mu_r_ref, mu_k_ref, mu_v_ref, mu_t_ref, w2f_ref, w2b_ref, a2f_ref, a2b_ref, g2_ref, rowp_ref, hseg_ref,
               wf_ref, bf_ref, kdf_ref, wb_ref, bb_ref, kdb_ref, ro_ref, kko_ref, vo_ref, go_ref, bonus_ref,
               *, seqs):
    R = ROW_BLOCK
    pos, nblk, _ = seqs.locate(pl.program_id(0), R)
    has_prev = (pos != 0).astype(F32)
    has_next = (pos != nblk - 1).astype(F32)

    def shifted(cur_ref, p_ref, n_ref, mu_ref):
        x = cur_ref[0]
        row = lax.broadcasted_iota(jnp.int32, x.shape, 0)
        prev = jnp.where(row == 0, p_ref[0, SUBLANES - 1:SUBLANES, :] * has_prev, pltpu.roll(x, 1, axis=0))
        nxt = jnp.where(row == R - 1, n_ref[0, 0:1, :] * has_next, pltpu.roll(x, R - 1, axis=0))
        return x + mu_ref[...] * (0.5 * (prev + nxt) - x)

    rr = shifted(r_ref, rp_ref, rn_ref, mu_r_ref)
    kr = shifted(k_ref, kp_ref, kn_ref, mu_k_ref)
    vr = shifted(v_ref, vp_ref, vn_ref, mu_v_ref)
    tail = shifted(t_ref, tp_ref, tn_ref, mu_t_ref)
    t_tanh = jnp.tanh(tail).astype(BF16)
    t_lin = tail.astype(BF16)
    t_sig = jax.nn.sigmoid(tail).astype(BF16)

    def dot(a, w_ref):
        return jnp.dot(a, w_ref[...], preferred_element_type=F32)

    w0f, w0b, a0f, a0b, k_k, k_a, r_k = [rowp_ref[i:i + 1, :] for i in range(7)]
    hseg = hseg_ref[...]
    kk = kr * k_k
    kk = kk / jnp.maximum(jnp.sqrt(_head_sums(kk * kk, hseg)), 1e-12)
    for w0, a0, w2_ref, a2_ref, w_out, b_out, kd_out in ((w0f, a0f, w2f_ref, a2f_ref, wf_ref, bf_ref, kdf_ref),
                                                        (w0b, a0b, w2b_ref, a2b_ref, wb_ref, bb_ref, kdb_ref)):
        logw = -math.exp(-0.5) * jax.nn.sigmoid(w0 + dot(t_tanh, w2_ref))
        a = jax.nn.sigmoid(a0 + dot(t_lin, a2_ref))
        w_out[0] = jnp.exp(logw)
        b_out[0] = kk * a
        kd_out[0] = kr * (1.0 + (a - 1.0) * k_a)
    ro_ref[0] = rr
    kko_ref[0] = kk
    vo_ref[0] = vr
    go_ref[0] = dot(t_sig, g2_ref)
    bonus_ref[0] = _head_sums(rr * kr * r_k, hseg) * vr


def rwkv_prep(z, ztail, seqs, p):
    T = z.shape[0]
    R, CW = ROW_BLOCK, PREP_COLS
    NB, n8 = T // R, T // SUBLANES
    per = R // SUBLANES
    zv, z8 = z.reshape(NB, R, -1), z.reshape(n8, SUBLANES, -1)
    tv, t8 = ztail.reshape(NB, R, B_TAIL), ztail.reshape(n8, SUBLANES, B_TAIL)
    base = [(EV_A_IN + i * B_WIDTH) // CW for i in range(3)]

    def cur(b):
        return pl.BlockSpec((1, R, CW), lambda i, c: (i, 0, b + c))

    def prev(b):
        return pl.BlockSpec((1, SUBLANES, CW), lambda i, c: (jnp.maximum(i * per - 1, 0), 0, b + c))

    def nxt(b):
        return pl.BlockSpec((1, SUBLANES, CW), lambda i, c: (jnp.minimum((i + 1) * per, n8 - 1), 0, b + c))

    t_cur = pl.BlockSpec((1, R, B_TAIL), lambda i, c: (i, 0, 0))
    t_prev = pl.BlockSpec((1, SUBLANES, B_TAIL), lambda i, c: (jnp.maximum(i * per - 1, 0), 0, 0))
    t_next = pl.BlockSpec((1, SUBLANES, B_TAIL), lambda i, c: (jnp.minimum((i + 1) * per, n8 - 1), 0, 0))

    mu = p['mu']
    mu_main = mu[:3 * B_WIDTH].reshape(1, 3 * B_WIDTH)
    mu_t = mu[3 * B_WIDTH:].reshape(1, B_TAIL)

    def mu_spec(i):
        return pl.BlockSpec((1, CW), lambda r, c: (0, i * (B_WIDTH // CW) + c))

    def padded(w, off):
        return jnp.zeros((B_TAIL, B_WIDTH), BF16).at[off:off + w.shape[0]].set(w.astype(BF16))

    lora = [padded(p['w2f'], _TAIL_OFF['wdf']), padded(p['w2b'], _TAIL_OFF['wdb']),
            padded(p['a2f'], _TAIL_OFF['adf']), padded(p['a2b'], _TAIL_OFF['adb']), padded(p['g2'], _TAIL_OFF['gd'])]
    lora_spec = pl.BlockSpec((B_TAIL, CW), lambda i, c: (0, c))
    rowp = jnp.stack([p['w0f'], p['w0b'], p['a0f'], p['a0b'], p['k_k'], p['k_a'], p['r_k'].reshape(-1),
                      jnp.zeros((B_WIDTH,), F32)])
    out_spec = pl.BlockSpec((1, R, CW), lambda i, c: (i, 0, c))
    names = ('w_f', 'b_f', 'kd_f', 'w_b', 'b_b', 'kd_b', 'r', 'kk', 'v', 'g', 'bonus')
    outs = pl.pallas_call(
        functools.partial(_prep_body, seqs=seqs),
        out_shape=tuple(jax.ShapeDtypeStruct((NB, R, B_WIDTH), F32) for _ in names),
        grid=(NB, B_WIDTH // CW),
        in_specs=[cur(base[0]), cur(base[1]), cur(base[2]), t_cur,
                  prev(base[0]), prev(base[1]), prev(base[2]), t_prev,
                  nxt(base[0]), nxt(base[1]), nxt(base[2]), t_next,
                  mu_spec(0), mu_spec(1), mu_spec(2), pl.BlockSpec((1, B_TAIL), lambda i, c: (0, 0)),
                  lora_spec, lora_spec, lora_spec, lora_spec, lora_spec,
                  pl.BlockSpec((SUBLANES, CW), lambda i, c: (0, c)),
                  pl.BlockSpec((LANES, LANES), lambda i, c: (0, 0))],
        out_specs=tuple(out_spec for _ in names),
        compiler_params=_params("parallel", "parallel"),
        name="rwkv_prep",
    )(zv, zv, zv, tv, z8, z8, z8, t8, z8, z8, z8, t8, mu_main, mu_main, mu_main, mu_t, *lora, rowp, _head_matrix())
    return dict(zip(names, outs))


def _rwkv_body(w_ref, b_ref, kd_ref, r_ref, kk_ref, v_ref, g_ref, h_ref, t0_ref, o_ref, tf_ref,
               s_ref, vb_ref, xo_ref, *, seqs, NB, passes, reverse):
    row_refs = (w_ref, b_ref, kd_ref, r_ref, kk_ref)
    TB = REC_TIME_BLOCK
    P, R, UW = s_ref.shape
    jb = pl.program_id(0)
    jb = (NB - 1 - jb) if reverse else jb
    pos, nblk, _ = seqs.locate(jb, TB)
    first = pos == ((nblk - 1) if reverse else 0)
    last = pos == (0 if reverse else (nblk - 1))

    @pl.when(first)
    def _():
        s_ref[...] = t0_ref[0]

    G = g_ref[...][None]
    Gb = G.astype(BF16)
    Hm = h_ref[...]

    Q = REC_SUB_UNITS
    subs = [slice(q * Q, (q + 1) * Q) for q in range(P // Q)]

    def seg_sum(x, n_pass):
        x = x.reshape(Q * R, UW)
        hi = x.astype(BF16)
        acc = jnp.dot(hi, Hm, preferred_element_type=F32)
        rem = x
        for _ in range(n_pass - 1):
            rem = rem - hi.astype(F32)
            hi = rem.astype(BF16)
            acc = acc + jnp.dot(hi, Hm, preferred_element_type=F32)
        return acc.reshape(Q, R, UW)

    def unit_rows(tile, j, qs):
        return jnp.stack([tile[j:j + 1, p * UW:(p + 1) * UW] for p in range(qs.start, qs.stop)])

    sub_iota = lax.broadcasted_iota(jnp.int32, (SUBLANES, UW), 0)

    def group(i, carry):
        gi = (TB // SUBLANES - 1 - i) if reverse else i
        t8 = pl.multiple_of(gi * SUBLANES, SUBLANES)
        tiles = [r[0, pl.ds(t8, SUBLANES), :] for r in row_refs]
        v_tile = v_ref[0, pl.ds(t8, SUBLANES), :]
        v_tile_b = v_tile.astype(BF16)
        for j in range(SUBLANES):
            for qs in subs:
                if passes[0] == 1:
                    vb = jnp.dot((Gb * unit_rows(v_tile_b, j, qs)).reshape(Q * R, UW), Hm,
                                 preferred_element_type=F32).reshape(Q, R, UW)
                else:
                    vb = seg_sum(G * unit_rows(v_tile, j, qs), passes[0])
                vb_ref[j, qs] = vb
        for jj in range(SUBLANES):
            j = SUBLANES - 1 - jj if reverse else jj
            for qs in subs:
                w, b, kd, r, kk = [unit_rows(tile, j, qs) for tile in tiles]
                S = s_ref[qs]
                u = seg_sum(S * kk, passes[1])
                S = S * w - u * b + vb_ref[j, qs] * kd
                s_ref[qs] = S
                xo_ref[j, qs] = S * r
        for qs in subs:
            o_tiles = [jnp.zeros((SUBLANES, UW), F32) for _ in range(Q)]
            for j in range(SUBLANES):
                orow = jnp.sum(G * seg_sum(xo_ref[j, qs], passes[2]), axis=1, keepdims=True)
                o_tiles = [jnp.where(sub_iota == j, orow[p], o_tiles[p]) for p in range(Q)]
            for p in range(Q):
                lo = (qs.start + p) * UW
                o_ref[0, pl.ds(t8, SUBLANES), lo:lo + UW] = o_tiles[p]
        return carry

    lax.fori_loop(0, TB // SUBLANES, group, 0)

    @pl.when(last)
    def _():
        tf_ref[0] = s_ref[...]


def rwkv7(rows, v, s0, seqs, *, reverse, passes):
    T = v.shape[0] * v.shape[1]
    TB, UW, seg = REC_TIME_BLOCK, LANES, B_HEAD
    NB = T // TB
    hpu = UW // seg
    P = B_HEADS // hpu
    n_seq = s0.shape[0]
    t0 = s0.reshape(n_seq, P, hpu, seg, seg).transpose(0, 1, 3, 2, 4).reshape(n_seq, P, seg, UW)
    lane = jnp.arange(UW)
    g = (lane[None, :] % seg == jnp.arange(seg)[:, None]).astype(F32)

    def blk(j):
        return (NB - 1 - j) if reverse else j

    row_spec = pl.BlockSpec((1, TB, B_WIDTH), lambda j: (blk(j), 0, 0))
    st_spec = pl.BlockSpec((1, P, seg, UW), lambda j: (seqs.locate(blk(j), TB)[2], 0, 0, 0))
    o, s = pl.pallas_call(
        functools.partial(_rwkv_body, seqs=seqs, NB=NB, passes=passes, reverse=reverse),
        out_shape=(jax.ShapeDtypeStruct((NB, TB, B_WIDTH), F32), jax.ShapeDtypeStruct(t0.shape, F32)),
        grid=(NB,),
        in_specs=[row_spec] * 6 + [pl.BlockSpec((seg, UW), lambda j: (0, 0)),
                                   pl.BlockSpec((UW, UW), lambda j: (0, 0)), st_spec],
        out_specs=(row_spec, st_spec),
        scratch_shapes=[pltpu.VMEM((P, seg, UW), F32), pltpu.VMEM((SUBLANES, P, seg, UW), F32),
                        pltpu.VMEM((SUBLANES, P, seg, UW), F32)],
        compiler_params=_params("arbitrary"),
        name="rwkv7",
    )(*[x.reshape(NB, TB, B_WIDTH) for x in rows], v.reshape(NB, TB, B_WIDTH), g, _head_matrix(), t0)
    s = s.reshape(n_seq, P, seg, hpu, seg).transpose(0, 1, 3, 2, 4).reshape(s0.shape)
    return o.reshape(T, B_WIDTH), s


HG_STEP = 8


def _hgrn_body(q_ref, f_ref, v_ref, lb_ref, s0_ref, o_ref, sf_ref, s_ref, *, seqs, NB, Hh, reverse):
    C = HG_STEP
    TB = ROW_BLOCK
    jb = pl.program_id(1)
    jb = (NB - 1 - jb) if reverse else jb
    pos, nblk, _ = seqs.locate(jb, TB)
    first = pos == ((nblk - 1) if reverse else 0)
    last = pos == (0 if reverse else (nblk - 1))

    @pl.when(first)
    def _():
        s_ref[...] = s0_ref[0]

    row = lax.broadcasted_iota(jnp.int32, (C, LANES), 0)
    def shift(x, d):
        if d == 0:
            return x
        return pltpu.roll(x, (C - d) if reverse else d, axis=0)

    def has_source(d):
        return (row <= C - 1 - d) if reverse else (row >= d)

    def block(i, carry):
        ci = (TB // C - 1 - i) if reverse else i
        t0 = pl.multiple_of(ci * C, C)
        for h in range(Hh):
            sl = slice(h * LANES, (h + 1) * LANES)
            hq = q_ref[0, pl.ds(t0, C), sl]
            q = hq * jax.nn.sigmoid(hq) * A_DK ** -0.5
            lb = lb_ref[:, sl]
            f = lb + (1.0 - lb) * jax.nn.sigmoid(f_ref[0, pl.ds(t0, C), sl])
            v = v_ref[0, pl.ds(t0, C), sl]
            k = 1.0 - f
            b = jnp.log(f)
            s = 1
            while s < C:
                b = b + jnp.where(has_source(s), shift(b, s), 0.0)
                s *= 2
            st = s_ref[h]
            stb = st.astype(BF16)
            o = lax.dot_general((q * jnp.exp(b)).astype(BF16), stb, (((1,), (1,)), ((), ())),
                                preferred_element_type=F32)
            for d in range(C):
                arg = jnp.where(has_source(d), b - shift(b, d), -1e30)
                a = jnp.sum(q * shift(k, d) * jnp.exp(arg), axis=1, keepdims=True)
                o = o + a * shift(v, d)
            o_ref[0, pl.ds(t0, C), sl] = o
            end = 0 if reverse else C - 1
            bl = b[end:end + 1, :]
            kt = k * jnp.exp(bl - b)
            s_ref[h] = st * jnp.exp(bl) + lax.dot_general(v.astype(BF16), kt.astype(BF16), (((0,), (0,)), ((), ())),
                                                         preferred_element_type=F32)
        return carry

    lax.fori_loop(0, TB // C, block, 0)

    @pl.when(last)
    def _():
        sf_ref[0] = s_ref[...]


def hgrn2(zv, cols, lb, s0, seqs, *, reverse, heads_per_step=8):
    NB, TB, _ = zv.shape
    W = A_WIDTH
    H = W // LANES
    Hh = heads_per_step
    bw = Hh * LANES
    assert TB == ROW_BLOCK and H % Hh == 0 and all(c % bw == 0 for c in cols)

    def blk(j):
        return (NB - 1 - j) if reverse else j

    def in_spec(col):
        return pl.BlockSpec((1, TB, bw), lambda u, j: (blk(j), 0, col // bw + u))

    st_spec = pl.BlockSpec((1, Hh, LANES, LANES), lambda u, j: (seqs.locate(blk(j), TB)[2], u, 0, 0))
    st0 = jnp.swapaxes(s0, -1, -2)
    o, s = pl.pallas_call(
        functools.partial(_hgrn_body, seqs=seqs, NB=NB, Hh=Hh, reverse=reverse),
        out_shape=(jax.ShapeDtypeStruct((NB, TB, W), F32), jax.ShapeDtypeStruct(st0.shape, F32)),
        grid=(H // Hh, NB),
        in_specs=[in_spec(cols[0]), in_spec(cols[1]), in_spec(cols[2]),
                  pl.BlockSpec((1, bw), lambda u, j: (0, u)), st_spec],
        out_specs=(pl.BlockSpec((1, TB, bw), lambda u, j: (blk(j), 0, u)), st_spec),
        scratch_shapes=[pltpu.VMEM((Hh, LANES, LANES), F32)],
        compiler_params=_params("parallel", "arbitrary"),
        name="hgrn2",
    )(zv, zv, zv, lb, st0)
    return o.reshape(NB * TB, W), jnp.swapaxes(s, -1, -2)


def _even_post_body(oaf_ref, oab_ref, hg_ref, obf_ref, obb_ref, bonus_ref, g_ref, rowp_ref, hseg_ref, o_ref):
    hg_g, ln_g, ln_b = [rowp_ref[i:i + 1, :] for i in range(3)]
    oa = oaf_ref[0] + oab_ref[0]
    hg = hg_ref[0]
    oa = oa * lax.rsqrt(jnp.mean(oa * oa, axis=-1, keepdims=True) + NORM_EPS) * hg_g * (hg * jax.nn.sigmoid(hg))
    o_ref[0, :, :A_WIDTH] = oa.astype(o_ref.dtype)
    hseg = hseg_ref[...]
    ob = obf_ref[0] + obb_ref[0]
    cen = ob - _head_sums(ob, hseg) * (1.0 / B_HEAD)
    var = _head_sums(cen * cen, hseg) * (1.0 / B_HEAD)
    ob = (cen * lax.rsqrt(var + B_GN_EPS) * ln_g + ln_b + bonus_ref[0]) * g_ref[0]
    o_ref[0, :, A_WIDTH:] = ob.astype(o_ref.dtype)


def even_post(oa_f, oa_b, zv, ob_f, ob_b, bonus, g, p):
    NB, R, _ = zv.shape
    T = NB * R
    W = A_WIDTH
    assert W == B_WIDTH and (4 * A_WIDTH) % W == 0
    spec = pl.BlockSpec((1, R, W), lambda i: (i, 0, 0))
    rowp = jnp.stack([p['hg_g'], p['ln_g'], p['ln_b']] + [jnp.zeros((W,), F32)] * 5)
    v3 = lambda x: x.reshape(NB, R, W)
    out = pl.pallas_call(
        _even_post_body,
        out_shape=jax.ShapeDtypeStruct((NB, R, 2 * W), BF16),
        grid=(NB,),
        in_specs=[spec, spec, pl.BlockSpec((1, R, W), lambda i: (i, 0, 4 * A_WIDTH // W)), spec, spec, spec, spec,
                  pl.BlockSpec((SUBLANES, W), lambda i: (0, 0)), pl.BlockSpec((LANES, LANES), lambda i: (0, 0))],
        out_specs=pl.BlockSpec((1, R, 2 * W), lambda i: (i, 0, 0)),
        compiler_params=_params("parallel"),
        name="even_post",
    )(v3(oa_f), v3(oa_b), zv, v3(ob_f), v3(ob_b), bonus, g, rowp, _head_matrix())
    return out.reshape(T, 2 * W)


def even_mixer(z, ztail, seqs, st, p):
    T = z.shape[0]
    zv = z.reshape(T // ROW_BLOCK, ROW_BLOCK, -1)
    oa_f, sf = hgrn2(zv, (0, A_WIDTH, 3 * A_WIDTH), p['lb_f'][None], st['hf'], seqs, reverse=False)
    oa_b, sb = hgrn2(zv, (0, 2 * A_WIDTH, 3 * A_WIDTH), p['lb_b'][None], st['hb'], seqs, reverse=True)
    q = rwkv_prep(z, ztail, seqs, p)
    ob_f, rf = rwkv7((q['w_f'], q['b_f'], q['kd_f'], q['r'], q['kk']), q['v'], st['rf'], seqs, reverse=False,
                     passes=REC_PASSES)
    ob_b, rb = rwkv7((q['w_b'], q['b_b'], q['kd_b'], q['r'], q['kk']), q['v'], st['rb'], seqs, reverse=True,
                     passes=REC_PASSES)
    return even_post(oa_f, oa_b, zv, ob_f, ob_b, q['bonus'], q['g'], p), (sf, sb, rf, rb)


def _swap_halves(x):
    lane = lax.broadcasted_iota(jnp.int32, x.shape, 1)
    quarter = ROPE_AXIS_DIM // 2
    return jnp.where(lane % ROPE_AXIS_DIM < quarter, pltpu.roll(x, LANES - quarter, axis=1),
                     pltpu.roll(x, quarter, axis=1))


def _attn_body(*refs, has_ctx):
    if has_ctx:
        (q_ref, kn_ref, vn_ref, ck_ref, cv_ref, qc_ref, qs_ref, kc_ref, ks_ref, qg_ref, kg_ref,
         o_ref, k_sc, v_sc) = refs
    else:
        q_ref, kn_ref, vn_ref, qg_ref, kg_ref, o_ref, ko_ref, vo_ref, k_sc, v_sc = refs

    def norm(x, g_ref):
        return x * lax.rsqrt(jnp.mean(x * x, axis=-1, keepdims=True) + NORM_EPS) * g_ref[...]

    @pl.when(pl.program_id(2) == 0)
    def _():
        k = norm(kn_ref[0], kg_ref)
        v = vn_ref[0]
        if has_ctx:
            n_past = ck_ref.shape[1]
            k = k * kc_ref[...] + _swap_halves(k) * ks_ref[...]
            k_sc[:n_past] = ck_ref[0].astype(BF16)
            v_sc[:n_past] = cv_ref[0].astype(BF16)
            k_sc[n_past:] = k.astype(BF16)
            v_sc[n_past:] = v.astype(BF16)
        else:
            ko_ref[0] = k
            vo_ref[0] = v
            k_sc[...] = k.astype(BF16)
            v_sc[...] = v.astype(BF16)

    k = k_sc[...]
    v = v_sc[...]
    scale = C_HEAD_DIM ** -0.5
    for g in range(C_GROUP):
        sl = slice(g * C_HEAD_DIM, (g + 1) * C_HEAD_DIM)
        q = norm(q_ref[0, :, sl], qg_ref)
        if has_ctx:
            q = q * qc_ref[...] + _swap_halves(q) * qs_ref[...]
        s = lax.dot_general(q.astype(BF16), k, (((1,), (1,)), ((), ())), preferred_element_type=F32) * scale
        m = jnp.max(s, axis=-1, keepdims=True)
        e = jnp.exp(s - m)
        p = e / jnp.sum(e, axis=-1, keepdims=True)
        o_ref[0, :, sl] = jnp.dot(p.astype(BF16), v, preferred_element_type=F32).astype(o_ref.dtype)


def _rope_tables(L):
    n_rows = L // GRID_W
    row = jnp.repeat(jnp.arange(n_rows), GRID_W).astype(F32)
    col = jnp.tile(jnp.arange(GRID_W), n_rows).astype(F32)
    half = ROPE_AXIS_DIM // 2
    inv = ROPE_THETA ** (-jnp.arange(half, dtype=F32) / half)
    ar, ac = row[:, None] * inv, col[:, None] * inv
    cos_t = jnp.concatenate([jnp.cos(ar), jnp.cos(ar), jnp.cos(ac), jnp.cos(ac)], axis=1)
    sin_t = jnp.concatenate([-jnp.sin(ar), jnp.sin(ar), -jnp.sin(ac), jnp.sin(ac)], axis=1)
    return cos_t, sin_t


def attention(z, row0, B, L, qn_g, kn_g, ctx_k=None, ctx_v=None):
    T = z.shape[0]
    assert row0 % L == 0 and L % Q_BLOCK == 0
    has_ctx = ctx_k is not None
    gw = C_GROUP * C_HEAD_DIM
    zq = z.reshape(T // Q_BLOCK, Q_BLOCK, -1)
    zl = z.reshape(T // L, L, -1)
    qb0, lb0 = row0 // Q_BLOCK, row0 // L
    nq = L // Q_BLOCK
    kcol, vcol = C_WIDTH // C_HEAD_DIM, (C_WIDTH + C_KV_WIDTH) // C_HEAD_DIM
    g_spec = pl.BlockSpec((1, C_HEAD_DIM), lambda b, h, i: (0, 0))
    in_specs = [pl.BlockSpec((1, Q_BLOCK, gw), lambda b, h, i: (qb0 + b * nq + i, 0, h)),
                pl.BlockSpec((1, L, C_HEAD_DIM), lambda b, h, i: (lb0 + b, 0, kcol + h)),
                pl.BlockSpec((1, L, C_HEAD_DIM), lambda b, h, i: (lb0 + b, 0, vcol + h))]
    args = [zq, zl, zl]
    o_spec = pl.BlockSpec((1, Q_BLOCK, gw), lambda b, h, i: (b, i, h))
    o_shape = jax.ShapeDtypeStruct((B, L, C_WIDTH), BF16)
    if has_ctx:
        n_past = ctx_k.shape[1]
        c_spec = pl.BlockSpec((1, n_past, C_HEAD_DIM), lambda b, h, i: (b, 0, h))
        cos_t, sin_t = _rope_tables(L)
        q_tab = pl.BlockSpec((Q_BLOCK, C_HEAD_DIM), lambda b, h, i: (i, 0))
        k_tab = pl.BlockSpec((L, C_HEAD_DIM), lambda b, h, i: (0, 0))
        in_specs += [c_spec, c_spec, q_tab, q_tab, k_tab, k_tab]
        args += [ctx_k.reshape(B, n_past, C_KV_WIDTH), ctx_v.reshape(B, n_past, C_KV_WIDTH), cos_t, sin_t, cos_t, sin_t]
        out_shape, out_specs = o_shape, o_spec
        Lk = n_past + L
    else:
        kv_spec = pl.BlockSpec((1, L, C_HEAD_DIM), lambda b, h, i: (b, 0, h))
        kv_shape = jax.ShapeDtypeStruct((B, L, C_KV_WIDTH), F32)
        out_shape, out_specs = (o_shape, kv_shape, kv_shape), (o_spec, kv_spec, kv_spec)
        Lk = L
    in_specs += [g_spec, g_spec]
    args += [qn_g.reshape(1, C_HEAD_DIM), kn_g.reshape(1, C_HEAD_DIM)]
    res = pl.pallas_call(
        functools.partial(_attn_body, has_ctx=has_ctx),
        out_shape=out_shape, grid=(B, C_KV_HEADS, nq), in_specs=in_specs, out_specs=out_specs,
        scratch_shapes=[pltpu.VMEM((Lk, C_HEAD_DIM), BF16), pltpu.VMEM((Lk, C_HEAD_DIM), BF16)],
        compiler_params=_params("parallel", "parallel", "arbitrary"),
        name="attn",
    )(*args)
    if has_ctx:
        return res, None, None
    o, k, v = res
    return o, k.reshape(B, L, C_KV_HEADS, C_HEAD_DIM), v.reshape(B, L, C_KV_HEADS, C_HEAD_DIM)


def _dft_body(ah_ref, al_ref, u_ref, o_ref):
    u = u_ref[0]
    uh = u.astype(BF16)
    ul = (u - uh.astype(F32)).astype(BF16)
    ah = ah_ref[...]
    o_ref[0] = (jnp.dot(ah, uh, preferred_element_type=F32) + jnp.dot(ah, ul, preferred_element_type=F32)
                + jnp.dot(al_ref[...], uh, preferred_element_type=F32))


def dft_mm(a, u, *, bn=512):
    ah, al = a
    M, K = ah.shape
    B, _, C = u.shape
    return pl.pallas_call(
        _dft_body,
        out_shape=jax.ShapeDtypeStruct((B, M, C), F32),
        grid=(B, C // bn),
        in_specs=[pl.BlockSpec((M, K), lambda b, j: (0, 0)), pl.BlockSpec((M, K), lambda b, j: (0, 0)),
                  pl.BlockSpec((1, K, bn), lambda b, j: (b, 0, j))],
        out_specs=pl.BlockSpec((1, M, bn), lambda b, j: (b, 0, j)),
        compiler_params=_params("parallel", "parallel"),
        name="dft_mm",
    )(ah, al, u)


def _split_bf16(a):
    hi = a.astype(BF16)
    return hi, (a - hi.astype(F32)).astype(BF16)


def _dft_matrix(L):
    N = 2 * L
    r = jnp.arange(N, dtype=jnp.int32)[:, None]
    s = jnp.arange(L, dtype=jnp.int32)[None, :]
    is_cos = r <= L
    f = jnp.where(is_cos, r, r - L)
    ang = (2.0 * math.pi / N) * ((f * s) % N).astype(F32)
    return jnp.where(is_cos, jnp.cos(ang), jnp.sin(ang))


def _centred_long_conv(u, h, bias, fmat, fmat_t):
    L = u.shape[1]
    N = 2 * L
    h_mod = jnp.concatenate([h[:1], 2.0 * h[1:]], axis=0)
    kf = dft_mm(fmat, h_mod[None])[0]
    r = jnp.arange(N)
    f = jnp.where(r <= L, r, r - L)
    herm = jnp.where((r == 0) | (r == L), 1.0, 2.0).astype(F32) / N
    scale = kf[f] * herm[:, None]
    uf = dft_mm(fmat, u)
    y = dft_mm(fmat_t, uf * scale[None])
    return y + u * bias


def _hyena_filters(L, w1, b1, w2, b2, w3, b3, w4, freq):
    t = jnp.linspace(0.0, 1.0, L, dtype=F32)[:, None]
    w_pos = 2.0 * math.pi * jnp.arange(L, dtype=F32)[:, None] / L
    bands = jnp.linspace(1e-4, HY_BANDS - 1, HY_BANDS, dtype=F32)[None]
    zpos = jnp.concatenate([t, jnp.cos(bands * w_pos), -jnp.sin(bands * w_pos)], axis=-1)
    f = freq.astype(F32)
    hdn = jnp.sin(f * (zpos @ w1 + b1))
    hdn = jnp.sin(f * (hdn @ w2 + b2))
    hdn = jnp.sin(f * (hdn @ w3 + b3))
    h = mm(hdn.astype(BF16), w4.astype(BF16), bm=L, bn=1024)
    deltas = jnp.abs(jnp.linspace(HY_MIN_DECAY, HY_MAX_DECAY, HY_ORDER * HY_W, dtype=F32))
    h = h * jnp.exp(-t * deltas)
    return h.reshape(L, HY_ORDER, HY_W)


def _neighbours(u):
    up = jnp.pad(u, ((0, 0), (1, 1), (0, 0)))
    return up[:, :-2], up[:, 2:]


def _odd_core(z, row0, B, L, ctx_k, ctx_v, p):
    o_c, k, v = attention(z, row0, B, L, p['qn_g'], p['kn_g'], ctx_k, ctx_v)
    u = z[row0:row0 + B * L, C_IN:].reshape(B, L, -1)
    prev, nxt = _neighbours(u)
    u = prev * p['conv_w'][0] + u * p['conv_w'][1] + nxt * p['conv_w'][2] + p['conv_b']
    hv, x1, x2 = jnp.split(u, 3, axis=-1)
    filt = _hyena_filters(L, p['f_w1'], p['f_b1'], p['f_w2'], p['f_b2'], p['f_w3'], p['f_b3'], p['f_w4'],
                          p['sin_freq'])
    fmat = _dft_matrix(L)
    fmat, fmat_t = _split_bf16(fmat), _split_bf16(fmat.T)
    z1 = x1 * _centred_long_conv(hv, filt[:, 0], p['hy_bias'][0], fmat, fmat_t)
    o_d = x2 * _centred_long_conv(z1, filt[:, 1], p['hy_bias'][1], fmat, fmat_t)
    return jnp.concatenate([o_c, o_d.astype(BF16)], axis=-1), k, v


def kernel(x_prompt, x_sample, c, state_hgrn_fwd, state_hgrn_bwd, state_rwkv_fwd, state_rwkv_bwd, cache_k, cache_v, c_ctx, ada_w, ada_b, norm_g, ffn_w1, ffn_w3, ffn_w2, final_norm_g, ev_w_in, ev_w_out, hg_lb_fwd, hg_lb_bwd, hg_norm_g, rw_mu, rw_w0_f, rw_w2_f, rw_w0_b, rw_w2_b, rw_a0_f, rw_a2_f, rw_a0_b, rw_a2_b, rw_g2, rw_kk, rw_ka, rw_rk, rw_ln_g, rw_ln_b, od_w_in, od_w_out, at_qn_g, at_kn_g, hy_conv_w, hy_conv_b, hy_f_w1, hy_f_b1, hy_f_w2, hy_f_b2, hy_f_w3, hy_f_b3, hy_f_w4, hy_sin_freq, hy_bias):
    Bc, Lc, _ = x_prompt.shape
    Bl, Ll, _ = x_sample.shape
    seqs = Seqs(Bc, Lc, Bl, Ll)
    n_ctx_tok = Bc * Lc
    n_tok = seqs.n_tok
    assert Lc % ROW_BLOCK == 0 and Ll % ROW_BLOCK == 0 and n_ctx_tok % Ll == 0

    def mod_row(i):
        return jnp.maximum(i * ROW_BLOCK // Ll - (n_ctx_tok // Ll - 1), 0)

    h = jnp.concatenate([x_prompt.reshape(n_ctx_tok, D_MODEL), x_sample.reshape(Bl * Ll, D_MODEL)], axis=0)
    cvec = jnp.concatenate([c_ctx[None], c], axis=0)
    cvec = jnp.pad(jax.nn.silu(cvec), ((0, SUBLANES - (1 + Bl)), (0, 0))).astype(BF16)
    mods = [(mm_small(cvec, ada_w, bn=1024, w_index=(l,)) + ada_b[l])[:1 + Bl].reshape(1 + Bl, N_MOD, D_MODEL)
            for l in range(DEPTH)]
    ffn_w2_bf16 = ffn_w2.astype(BF16)

    lb_fwd_all = jnp.cumsum(jax.nn.softmax(hg_lb_fwd, axis=0)[1:], axis=0)
    lb_bwd_all = jnp.cumsum(jax.nn.softmax(hg_lb_bwd, axis=0)[1:], axis=0)

    def with_ctx_zeros(s):
        return jnp.concatenate([jnp.zeros((Bc,) + s.shape[1:], F32), s], axis=0)

    def ffn(x, l, s):
        mid = ffn_up(x, ffn_w1, ffn_w3, w_index=(l, s))
        return mm(mid, ffn_w2_bf16, bm=1024, bn=512, bk=D_FF // 2, w_index=(l, s))

    new_hf, new_hb, new_rf, new_rb, new_k, new_v = [], [], [], [], [], []
    x = ada_step(h, norm_g[0, 0], mod_row=mod_row, m_in=mods[0], i_in=0)
    y_out = None
    for l in range(DEPTH):
        m = mods[l]
        h, xn = ada_step(h, norm_g[l, 1], mod_row=mod_row, y=ffn(x, l, 0), m_out=m, i_out=0, w_out=0.5,
                         m_in=m, i_in=1)
        j = l // 2
        if l % 2 == 0:
            p = dict(lb_f=lb_fwd_all[j], lb_b=lb_bwd_all[j], hg_g=hg_norm_g[j], mu=rw_mu[j],
                     w0f=rw_w0_f[j], w2f=rw_w2_f[j], w0b=rw_w0_b[j], w2b=rw_w2_b[j],
                     a0f=rw_a0_f[j], a2f=rw_a2_f[j], a0b=rw_a0_b[j], a2b=rw_a2_b[j], g2=rw_g2[j],
                     k_k=rw_kk[j], k_a=rw_ka[j], r_k=rw_rk[j], ln_g=rw_ln_g[j], ln_b=rw_ln_b[j])
            st = dict(hf=with_ctx_zeros(state_hgrn_fwd[:, j]), hb=with_ctx_zeros(state_hgrn_bwd[:, j]),
                      rf=with_ctx_zeros(state_rwkv_fwd[:, j]), rb=with_ctx_zeros(state_rwkv_bwd[:, j]))
            z = mm_ws(xn, ev_w_in, bm=1024, bn=512, w_index=(j,), n_cols=EV_MAIN)
            ztail = mm_ws(xn, ev_w_in[j, :, EV_MAIN:], bm=1024, bn=B_TAIL)
            o, (s_hf, s_hb, s_rf, s_rb) = even_mixer(z, ztail, seqs, st, p)
            new_hf.append(s_hf[:Bc])
            new_hb.append(s_hb[:Bc])
            new_rf.append(s_rf[:Bc])
            new_rb.append(s_rb[:Bc])
            w_out = ev_w_out
        else:
            p = dict(qn_g=at_qn_g[j], kn_g=at_kn_g[j], conv_w=hy_conv_w[j], conv_b=hy_conv_b[j],
                     f_w1=hy_f_w1[j], f_b1=hy_f_b1[j], f_w2=hy_f_w2[j], f_b2=hy_f_b2[j], f_w3=hy_f_w3[j],
                     f_b3=hy_f_b3[j], f_w4=hy_f_w4[j], sin_freq=hy_sin_freq[j], hy_bias=hy_bias[j])
            z = mm_ws(xn, od_w_in, bm=1024, bn=512, w_index=(j,))
            o_ctx, k_ctx, v_ctx = _odd_core(z, 0, Bc, Lc, None, None, p)
            o_lat, _, _ = _odd_core(z, n_ctx_tok, Bl, Ll, cache_k[:, j], cache_v[:, j], p)
            new_k.append(k_ctx)
            new_v.append(v_ctx)
            o = jnp.concatenate([o_ctx.reshape(n_ctx_tok, -1), o_lat.reshape(Bl * Ll, -1)], axis=0)
            w_out = od_w_out
        h, x = ada_step(h, norm_g[l, 2], mod_row=mod_row, y=mm_ws(o, w_out, bm=1024, bn=512, w_index=(j,)),
                        m_out=m, i_out=1, w_out=1.0, m_in=m, i_in=2)
        y = ffn(x, l, 1)
        if l + 1 < DEPTH:
            h, x = ada_step(h, norm_g[l + 1, 0], mod_row=mod_row, y=y, m_out=m, i_out=2, w_out=0.5,
                            m_in=mods[l + 1], i_in=0)
        else:
            h, y_out = ada_step(h, final_norm_g, mod_row=mod_row, y=y, m_out=m, i_out=2, w_out=0.5, out_dtype=F32)
    return (y_out[:n_ctx_tok].reshape(Bc, Lc, D_MODEL), y_out[n_ctx_tok:].reshape(Bl, Ll, D_MODEL),
            jnp.stack(new_hf, axis=1), jnp.stack(new_hb, axis=1), jnp.stack(new_rf, axis=1),
            jnp.stack(new_rb, axis=1), jnp.stack(new_k, axis=1), jnp.stack(new_v, axis=1))
```

```python
import functools
import math
from typing import NamedTuple

import numpy as np
import jax
import jax.numpy as jnp
from jax import lax
from jax.experimental import pallas as pl
from jax.experimental.pallas import tpu as pltpu

F32 = jnp.float32
BF16 = jnp.bfloat16

D_MODEL = 4096
DEPTH = 2
GRID_W = 64
N_MOD = 9
D_FF = 11008
NORM_EPS = 1e-6
A_WIDTH = D_MODEL // 2
A_HEADS = 16
A_DK = A_WIDTH // A_HEADS
A_DV = A_WIDTH // A_HEADS
B_WIDTH = D_MODEL // 2
B_HEAD = 64
B_HEADS = B_WIDTH // B_HEAD
B_DECAY_LORA = max(32, int(round(1.8 * B_WIDTH ** 0.5 / 32)) * 32)
B_AAA_LORA = max(32, int(round(1.8 * B_WIDTH ** 0.5 / 32)) * 32)
B_GATE_LORA = max(32, int(round(0.6 * B_WIDTH ** 0.8 / 32)) * 32)
B_TAIL = 2 * B_DECAY_LORA + 2 * B_AAA_LORA + B_GATE_LORA
B_GN_EPS = 64e-5
EV_A_IN = 5 * A_WIDTH
EV_B_IN = 3 * B_WIDTH + B_TAIL
EV_MAIN = EV_A_IN + 3 * B_WIDTH
C_HEAD_DIM = 128
C_HEADS = 16
C_KV_HEADS = 4
C_GROUP = C_HEADS // C_KV_HEADS
C_WIDTH = C_HEADS * C_HEAD_DIM
C_KV_WIDTH = C_KV_HEADS * C_HEAD_DIM
C_IN = C_WIDTH + 2 * C_KV_WIDTH
ROPE_AXIS_DIM = C_HEAD_DIM // 2
ROPE_THETA = 10000.0
Q_BLOCK = 128
HY_W = D_MODEL // 2
HY_ORDER = 2
HY_BANDS = 16
HY_TARGET = 1e-2
HY_MIN_DECAY = math.log(HY_TARGET) / 1.5
HY_MAX_DECAY = math.log(HY_TARGET) / 0.3

V7X_VMEM_LIMIT_BYTES = 56 * 1024 * 1024
LANES = 128
SUBLANES = 8
ROW_BLOCK = 256
REC_TIME_BLOCK = 128
REC_PASSES = (1, 1, 1)
REC_SUB_UNITS = 4
PREP_COLS = 512


def _params(*sem):
    return pltpu.CompilerParams(dimension_semantics=sem, vmem_limit_bytes=V7X_VMEM_LIMIT_BYTES)


class Seqs(NamedTuple):
    Bc: int
    Lc: int
    Bl: int
    Ll: int

    @property
    def n_tok(self):
        return self.Bc * self.Lc + self.Bl * self.Ll

    def locate(self, j, blk):
        n_ctx = self.Bc * self.Lc // blk
        cb, lb = self.Lc // blk, self.Ll // blk
        is_ctx = j < n_ctx
        jl = jnp.maximum(j - n_ctx, 0)
        pos = jnp.where(is_ctx, j % cb, jl % lb)
        nblk = jnp.where(is_ctx, cb, lb)
        sid = jnp.where(is_ctx, j // cb, self.Bc + jl // lb)
        return pos, nblk, sid


def _gated_residual(acc, h_ref, m_ref, i_out, w_out):
    return h_ref[...] + w_out * m_ref[0, 3 * i_out + 2:3 * i_out + 3, :] * acc


def _mm_body(x_ref, w_ref, *rest, nk, resid):
    o_ref = rest[-1]
    part = jnp.dot(x_ref[...], w_ref[...], preferred_element_type=F32)

    def finish(acc):
        return acc if resid is None else _gated_residual(acc, rest[0], rest[1], *resid)

    if nk == 1:
        o_ref[...] = finish(part).astype(o_ref.dtype)
    else:
        k = pl.program_id(2)

        @pl.when(k == 0)
        def _():
            o_ref[...] = part

        @pl.when((k > 0) & (k < nk - 1))
        def _():
            o_ref[...] += part

        @pl.when(k == nk - 1)
        def _():
            o_ref[...] = finish(o_ref[...] + part)


def _stacked(w_index):
    return (None,) * len(w_index)


class Resid(NamedTuple):
    h: jax.Array
    mods: jax.Array
    mod_row: object
    i_out: int
    w_out: float


def mm(x, w, *, bm, bn, bk=None, w_index=(), resid=None, out_dtype=F32):
    M, K = x.shape
    N = w.shape[-1]
    bk = K if bk is None else bk
    nk = K // bk
    assert M % bm == 0 and N % bn == 0 and K % bk == 0
    assert nk == 1 or out_dtype == F32
    in_specs = [pl.BlockSpec((bm, bk), lambda i, j, k: (i, k)),
                pl.BlockSpec(_stacked(w_index) + (bk, bn), lambda i, j, k: w_index + (k, j))]
    args = [x, w]
    if resid is not None:
        in_specs += [pl.BlockSpec((bm, bn), lambda i, j, k: (i, j)),
                     pl.BlockSpec((1, N_MOD, bn), lambda i, j, k: (resid.mod_row(i, bm), 0, j))]
        args += [resid.h, resid.mods]
    return pl.pallas_call(
        functools.partial(_mm_body, nk=nk, resid=None if resid is None else (resid.i_out, resid.w_out)),
        out_shape=jax.ShapeDtypeStruct((M, N), out_dtype),
        grid=(M // bm, N // bn, nk),
        in_specs=in_specs,
        out_specs=pl.BlockSpec((bm, bn), lambda i, j, k: (i, j)),
        compiler_params=_params("parallel", "parallel", "arbitrary"),
        name="mm",
    )(*args)


def _mm_ws_body(x_ref, w_ref, *rest, resid):
    o_ref, wb_ref = rest[-2:]

    @pl.when(pl.program_id(1) == 0)
    def _():
        wb_ref[...] = w_ref[...].astype(BF16)

    acc = jnp.dot(x_ref[...], wb_ref[...], preferred_element_type=F32)
    if resid is not None:
        acc = _gated_residual(acc, rest[0], rest[1], *resid)
    o_ref[...] = acc.astype(o_ref.dtype)


def mm_ws(x, w, *, bm, bn, w_index=(), n_cols=None, resid=None, out_dtype=F32):
    M, K = x.shape
    N = w.shape[-1] if n_cols is None else n_cols
    assert M % bm == 0 and N % bn == 0
    in_specs = [pl.BlockSpec((bm, K), lambda j, i: (i, 0)),
                pl.BlockSpec(_stacked(w_index) + (K, bn), lambda j, i: w_index + (0, j))]
    args = [x, w]
    if resid is not None:
        in_specs += [pl.BlockSpec((bm, bn), lambda j, i: (i, j)),
                     pl.BlockSpec((1, N_MOD, bn), lambda j, i: (resid.mod_row(i, bm), 0, j))]
        args += [resid.h, resid.mods]
    return pl.pallas_call(
        functools.partial(_mm_ws_body, resid=None if resid is None else (resid.i_out, resid.w_out)),
        out_shape=jax.ShapeDtypeStruct((M, N), out_dtype),
        grid=(N // bn, M // bm),
        in_specs=in_specs,
        out_specs=pl.BlockSpec((bm, bn), lambda j, i: (i, j)),
        scratch_shapes=[pltpu.VMEM((K, bn), BF16)],
        compiler_params=_params("arbitrary", "arbitrary"),
        name="mm_ws",
    )(*args)


def _mm_small_body(x_ref, w_ref, o_ref):
    o_ref[...] = jnp.dot(x_ref[...], w_ref[...].astype(BF16), preferred_element_type=F32)


def mm_small(x, w, *, bn, w_index=()):
    M, K = x.shape
    N = w.shape[-1]
    return pl.pallas_call(
        _mm_small_body,
        out_shape=jax.ShapeDtypeStruct((M, N), F32),
        grid=(N // bn,),
        in_specs=[pl.BlockSpec((M, K), lambda j: (0, 0)),
                  pl.BlockSpec(_stacked(w_index) + (K, bn), lambda j: w_index + (0, j))],
        out_specs=pl.BlockSpec((M, bn), lambda j: (0, j)),
        compiler_params=_params("parallel"),
        name="mm_small",
    )(x, w)


def _ffn_up_body(x_ref, w1_ref, w3_ref, o_ref, w1b_ref, w3b_ref):
    @pl.when(pl.program_id(1) == 0)
    def _():
        w1b_ref[...] = w1_ref[...].astype(BF16)
        w3b_ref[...] = w3_ref[...].astype(BF16)

    x = x_ref[...]
    a = jnp.dot(x, w1b_ref[...], preferred_element_type=F32)
    b = jnp.dot(x, w3b_ref[...], preferred_element_type=F32)
    o_ref[...] = (a * jax.nn.sigmoid(a) * b).astype(o_ref.dtype)


def ffn_up(x, w1, w3, *, w_index=(), bm=1024, bn=256):
    M, K = x.shape
    N = w1.shape[-1]
    w_spec = pl.BlockSpec(_stacked(w_index) + (K, bn), lambda j, i: w_index + (0, j))
    return pl.pallas_call(
        _ffn_up_body,
        out_shape=jax.ShapeDtypeStruct((M, N), BF16),
        grid=(N // bn, M // bm),
        in_specs=[pl.BlockSpec((bm, K), lambda j, i: (i, 0)), w_spec, w_spec],
        out_specs=pl.BlockSpec((bm, bn), lambda j, i: (i, j)),
        scratch_shapes=[pltpu.VMEM((K, bn), BF16), pltpu.VMEM((K, bn), BF16)],
        compiler_params=_params("arbitrary", "arbitrary"),
        name="ffn_up",
    )(x, w1, w3)


def _ada_body(*refs, i_in):
    h_ref, g_ref = refs[:2]
    x_ref = refs[-1]
    h = h_ref[...]
    x = h * lax.rsqrt(jnp.mean(h * h, axis=-1, keepdims=True) + NORM_EPS) * g_ref[...]
    if len(refs) == 4:
        mi_ref = refs[2]
        x = x * (1.0 + mi_ref[0, 3 * i_in + 1:3 * i_in + 2, :]) + mi_ref[0, 3 * i_in:3 * i_in + 1, :]
    x_ref[...] = x.astype(x_ref.dtype)


def ada_step(h, g, *, mod_row, m_in=None, i_in=0, out_dtype=BF16):
    T, D = h.shape
    R = ROW_BLOCK
    row_spec = pl.BlockSpec((R, D), lambda i: (i, 0))
    args, specs = [h, g.reshape(1, D)], [row_spec, pl.BlockSpec((1, D), lambda i: (0, 0))]
    if m_in is not None:
        args.append(m_in)
        specs.append(pl.BlockSpec((1, N_MOD, D), lambda i: (mod_row(i, R), 0, 0)))
    return pl.pallas_call(
        functools.partial(_ada_body, i_in=i_in),
        out_shape=jax.ShapeDtypeStruct((T, D), out_dtype), grid=(T // R,), in_specs=specs, out_specs=row_spec,
        compiler_params=_params("parallel"), name="ada_step",
    )(*args)


def _head_sums(x, hseg, n_pass=2):
    outs = []
    for c in range(x.shape[1] // LANES):
        rem = x[:, c * LANES:(c + 1) * LANES]
        acc = None
        for _ in range(n_pass):
            hi = rem.astype(BF16)
            part = jnp.dot(hi, hseg, preferred_element_type=F32)
            acc = part if acc is None else acc + part
            rem = rem - hi.astype(F32)
        outs.append(acc)
    return outs[0] if len(outs) == 1 else jnp.concatenate(outs, axis=1)


def _head_matrix():
    lane = jnp.arange(LANES)
    return (lane[:, None] // B_HEAD == lane[None, :] // B_HEAD).astype(BF16)


_TAIL_OFF = dict(wdf=0, wdb=B_DECAY_LORA, adf=2 * B_DECAY_LORA, adb=2 * B_DECAY_LORA + B_AAA_LORA,
                 gd=2 * B_DECAY_LORA + 2 * B_AAA_LORA)


def _prep_body(r_ref, k_ref, v_ref, t_ref, rp_ref, kp_ref, vp_ref, tp_ref, rn_ref, kn_ref, vn_ref, tn_ref,
               mu_r_ref, mu_k_ref, mu_v_ref, mu_t_ref, w2f_ref, w2b_ref, a2f_ref, a2b_ref, g2_ref, rowp_ref, hseg_ref,
               wf_ref, bf_ref, kdf_ref, wb_ref, bb_ref, kdb_ref, ro_ref, kko_ref, vo_ref, go_ref, bonus_ref,
               *, seqs):
    R = ROW_BLOCK
    pos, nblk, _ = seqs.locate(pl.program_id(0), R)
    has_prev = (pos != 0).astype(F32)
    has_next = (pos != nblk - 1).astype(F32)

    def shifted(cur_ref, p_ref, n_ref, mu_ref):
        x = cur_ref[0]
        row = lax.broadcasted_iota(jnp.int32, x.shape, 0)
        prev = jnp.where(row == 0, p_ref[0, SUBLANES - 1:SUBLANES, :] * has_prev, pltpu.roll(x, 1, axis=0))
        nxt = jnp.where(row == R - 1, n_ref[0, 0:1, :] * has_next, pltpu.roll(x, R - 1, axis=0))
        return x + mu_ref[...] * (0.5 * (prev + nxt) - x)

    rr = shifted(r_ref, rp_ref, rn_ref, mu_r_ref)
    kr = shifted(k_ref, kp_ref, kn_ref, mu_k_ref)
    vr = shifted(v_ref, vp_ref, vn_ref, mu_v_ref)
    tail = shifted(t_ref, tp_ref, tn_ref, mu_t_ref)
    t_tanh = jnp.tanh(tail).astype(BF16)
    t_lin = tail.astype(BF16)
    t_sig = jax.nn.sigmoid(tail).astype(BF16)

    def dot(a, w_ref):
        return jnp.dot(a, w_ref[...], preferred_element_type=F32)

    w0f, w0b, a0f, a0b, k_k, k_a, r_k = [rowp_ref[i:i + 1, :] for i in range(7)]
    hseg = hseg_ref[...]
    kk = kr * k_k
    kk = kk / jnp.maximum(jnp.sqrt(_head_sums(kk * kk, hseg)), 1e-12)
    for w0, a0, w2_ref, a2_ref, w_out, b_out, kd_out in ((w0f, a0f, w2f_ref, a2f_ref, wf_ref, bf_ref, kdf_ref),
                                                        (w0b, a0b, w2b_ref, a2b_ref, wb_ref, bb_ref, kdb_ref)):
        logw = -math.exp(-0.5) * jax.nn.sigmoid(w0 + dot(t_tanh, w2_ref))
        a = jax.nn.sigmoid(a0 + dot(t_lin, a2_ref))
        w_out[0] = jnp.exp(logw)
        b_out[0] = kk * a
        kd_out[0] = kr * (1.0 + (a - 1.0) * k_a)
    ro_ref[0] = rr
    kko_ref[0] = kk
    vo_ref[0] = vr
    go_ref[0] = dot(t_sig, g2_ref)
    bonus_ref[0] = _head_sums(rr * kr * r_k, hseg) * vr


def rwkv_prep(z, ztail, seqs, p):
    T = z.shape[0]
    R, CW = ROW_BLOCK, PREP_COLS
    NB, n8 = T // R, T // SUBLANES
    per = R // SUBLANES
    zv, z8 = z.reshape(NB, R, -1), z.reshape(n8, SUBLANES, -1)
    tv, t8 = ztail.reshape(NB, R, B_TAIL), ztail.reshape(n8, SUBLANES, B_TAIL)
    base = [(EV_A_IN + i * B_WIDTH) // CW for i in range(3)]

    def cur(b):
        return pl.BlockSpec((1, R, CW), lambda i, c: (i, 0, b + c))

    def prev(b):
        return pl.BlockSpec((1, SUBLANES, CW), lambda i, c: (jnp.maximum(i * per - 1, 0), 0, b + c))

    def nxt(b):
        return pl.BlockSpec((1, SUBLANES, CW), lambda i, c: (jnp.minimum((i + 1) * per, n8 - 1), 0, b + c))

    t_cur = pl.BlockSpec((1, R, B_TAIL), lambda i, c: (i, 0, 0))
    t_prev = pl.BlockSpec((1, SUBLANES, B_TAIL), lambda i, c: (jnp.maximum(i * per - 1, 0), 0, 0))
    t_next = pl.BlockSpec((1, SUBLANES, B_TAIL), lambda i, c: (jnp.minimum((i + 1) * per, n8 - 1), 0, 0))

    mu = p['mu']
    mu_main = mu[:3 * B_WIDTH].reshape(1, 3 * B_WIDTH)
    mu_t = mu[3 * B_WIDTH:].reshape(1, B_TAIL)

    def mu_spec(i):
        return pl.BlockSpec((1, CW), lambda r, c: (0, i * (B_WIDTH // CW) + c))

    def padded(w, off):
        return jnp.zeros((B_TAIL, B_WIDTH), BF16).at[off:off + w.shape[0]].set(w.astype(BF16))

    lora = [padded(p['w2f'], _TAIL_OFF['wdf']), padded(p['w2b'], _TAIL_OFF['wdb']),
            padded(p['a2f'], _TAIL_OFF['adf']), padded(p['a2b'], _TAIL_OFF['adb']), padded(p['g2'], _TAIL_OFF['gd'])]
    lora_spec = pl.BlockSpec((B_TAIL, CW), lambda i, c: (0, c))
    rowp = jnp.stack([p['w0f'], p['w0b'], p['a0f'], p['a0b'], p['k_k'], p['k_a'], p['r_k'].reshape(-1),
                      jnp.zeros((B_WIDTH,), F32)])
    out_spec = pl.BlockSpec((1, R, CW), lambda i, c: (i, 0, c))
    names = ('w_f', 'b_f', 'kd_f', 'w_b', 'b_b', 'kd_b', 'r', 'kk', 'v', 'g', 'bonus')
    outs = pl.pallas_call(
        functools.partial(_prep_body, seqs=seqs),
        out_shape=tuple(jax.ShapeDtypeStruct((NB, R, B_WIDTH), F32) for _ in names),
        grid=(NB, B_WIDTH // CW),
        in_specs=[cur(base[0]), cur(base[1]), cur(base[2]), t_cur,
                  prev(base[0]), prev(base[1]), prev(base[2]), t_prev,
                  nxt(base[0]), nxt(base[1]), nxt(base[2]), t_next,
                  mu_spec(0), mu_spec(1), mu_spec(2), pl.BlockSpec((1, B_TAIL), lambda i, c: (0, 0)),
                  lora_spec, lora_spec, lora_spec, lora_spec, lora_spec,
                  pl.BlockSpec((SUBLANES, CW), lambda i, c: (0, c)),
                  pl.BlockSpec((LANES, LANES), lambda i, c: (0, 0))],
        out_specs=tuple(out_spec for _ in names),
        compiler_params=_params("parallel", "parallel"),
        name="rwkv_prep",
    )(zv, zv, zv, tv, z8, z8, z8, t8, z8, z8, z8, t8, mu_main, mu_main, mu_main, mu_t, *lora, rowp, _head_matrix())
    return dict(zip(names, outs))


def _rwkv_body(w_ref, b_ref, kd_ref, r_ref, kk_ref, v_ref, g_ref, h_ref, t0_ref, o_ref, tf_ref,
               s_ref, vb_ref, xo_ref, *, seqs, NB, passes, reverse):
    row_refs = (w_ref, b_ref, kd_ref, r_ref, kk_ref)
    TB = REC_TIME_BLOCK
    P, R, UW = s_ref.shape
    jb = pl.program_id(0)
    jb = (NB - 1 - jb) if reverse else jb
    pos, nblk, _ = seqs.locate(jb, TB)
    first = pos == ((nblk - 1) if reverse else 0)
    last = pos == (0 if reverse else (nblk - 1))

    @pl.when(first)
    def _():
        s_ref[...] = t0_ref[0]

    G = g_ref[...][None]
    Gb = G.astype(BF16)
    Hm = h_ref[...]

    Q = REC_SUB_UNITS
    subs = [slice(q * Q, (q + 1) * Q) for q in range(P // Q)]

    def seg_sum(x, n_pass):
        x = x.reshape(Q * R, UW)
        hi = x.astype(BF16)
        acc = jnp.dot(hi, Hm, preferred_element_type=F32)
        rem = x
        for _ in range(n_pass - 1):
            rem = rem - hi.astype(F32)
            hi = rem.astype(BF16)
            acc = acc + jnp.dot(hi, Hm, preferred_element_type=F32)
        return acc.reshape(Q, R, UW)

    def unit_rows(tile, j, qs):
        return jnp.stack([tile[j:j + 1, p * UW:(p + 1) * UW] for p in range(qs.start, qs.stop)])

    sub_iota = lax.broadcasted_iota(jnp.int32, (SUBLANES, UW), 0)

    def group(i, carry):
        gi = (TB // SUBLANES - 1 - i) if reverse else i
        t8 = pl.multiple_of(gi * SUBLANES, SUBLANES)
        tiles = [r[0, pl.ds(t8, SUBLANES), :] for r in row_refs]
        v_tile = v_ref[0, pl.ds(t8, SUBLANES), :]
        v_tile_b = v_tile.astype(BF16)
        for j in range(SUBLANES):
            for qs in subs:
                if passes[0] == 1:
                    vb = jnp.dot((Gb * unit_rows(v_tile_b, j, qs)).reshape(Q * R, UW), Hm,
                                 preferred_element_type=F32).reshape(Q, R, UW)
                else:
                    vb = seg_sum(G * unit_rows(v_tile, j, qs), passes[0])
                vb_ref[j, qs] = vb
        for jj in range(SUBLANES):
            j = SUBLANES - 1 - jj if reverse else jj
            for qs in subs:
                w, b, kd, r, kk = [unit_rows(tile, j, qs) for tile in tiles]
                S = s_ref[qs]
                u = seg_sum(S * kk, passes[1])
                S = S * w - u * b + vb_ref[j, qs] * kd
                s_ref[qs] = S
                xo_ref[j, qs] = S * r
        for qs in subs:
            o_tiles = [jnp.zeros((SUBLANES, UW), F32) for _ in range(Q)]
            for j in range(SUBLANES):
                orow = jnp.sum(G * seg_sum(xo_ref[j, qs], passes[2]), axis=1, keepdims=True)
                o_tiles = [jnp.where(sub_iota == j, orow[p], o_tiles[p]) for p in range(Q)]
            for p in range(Q):
                lo = (qs.start + p) * UW
                o_ref[0, pl.ds(t8, SUBLANES), lo:lo + UW] = o_tiles[p]
        return carry

    lax.fori_loop(0, TB // SUBLANES, group, 0)

    @pl.when(last)
    def _():
        tf_ref[0] = s_ref[...]


def rwkv7(rows, v, s0, seqs, *, reverse, passes):
    T = v.shape[0] * v.shape[1]
    TB, UW, seg = REC_TIME_BLOCK, LANES, B_HEAD
    NB = T // TB
    hpu = UW // seg
    P = B_HEADS // hpu
    n_seq = s0.shape[0]
    t0 = s0.reshape(n_seq, P, hpu, seg, seg).transpose(0, 1, 3, 2, 4).reshape(n_seq, P, seg, UW)
    lane = jnp.arange(UW)
    g = (lane[None, :] % seg == jnp.arange(seg)[:, None]).astype(F32)

    def blk(j):
        return (NB - 1 - j) if reverse else j

    row_spec = pl.BlockSpec((1, TB, B_WIDTH), lambda j: (blk(j), 0, 0))
    st_spec = pl.BlockSpec((1, P, seg, UW), lambda j: (seqs.locate(blk(j), TB)[2], 0, 0, 0))
    o, s = pl.pallas_call(
        functools.partial(_rwkv_body, seqs=seqs, NB=NB, passes=passes, reverse=reverse),
        out_shape=(jax.ShapeDtypeStruct((NB, TB, B_WIDTH), F32), jax.ShapeDtypeStruct(t0.shape, F32)),
        grid=(NB,),
        in_specs=[row_spec] * 6 + [pl.BlockSpec((seg, UW), lambda j: (0, 0)),
                                   pl.BlockSpec((UW, UW), lambda j: (0, 0)), st_spec],
        out_specs=(row_spec, st_spec),
        scratch_shapes=[pltpu.VMEM((P, seg, UW), F32), pltpu.VMEM((SUBLANES, P, seg, UW), F32),
                        pltpu.VMEM((SUBLANES, P, seg, UW), F32)],
        compiler_params=_params("arbitrary"),
        name="rwkv7",
    )(*[x.reshape(NB, TB, B_WIDTH) for x in rows], v.reshape(NB, TB, B_WIDTH), g, _head_matrix(), t0)
    s = s.reshape(n_seq, P, seg, hpu, seg).transpose(0, 1, 3, 2, 4).reshape(s0.shape)
    return o.reshape(T, B_WIDTH), s


HG_STEP = 8


def _hgrn_body(q_ref, f_ref, v_ref, lb_ref, s0_ref, o_ref, sf_ref, s_ref, *, seqs, NB, Hh, reverse):
    C = HG_STEP
    TB = ROW_BLOCK
    jb = pl.program_id(1)
    jb = (NB - 1 - jb) if reverse else jb
    pos, nblk, _ = seqs.locate(jb, TB)
    first = pos == ((nblk - 1) if reverse else 0)
    last = pos == (0 if reverse else (nblk - 1))

    @pl.when(first)
    def _():
        s_ref[...] = s0_ref[0]

    row = lax.broadcasted_iota(jnp.int32, (C, LANES), 0)
    def shift(x, d):
        if d == 0:
            return x
        return pltpu.roll(x, (C - d) if reverse else d, axis=0)

    def has_source(d):
        return (row <= C - 1 - d) if reverse else (row >= d)

    def block(i, carry):
        ci = (TB // C - 1 - i) if reverse else i
        t0 = pl.multiple_of(ci * C, C)
        for h in range(Hh):
            sl = slice(h * LANES, (h + 1) * LANES)
            hq = q_ref[0, pl.ds(t0, C), sl]
            q = hq * jax.nn.sigmoid(hq) * A_DK ** -0.5
            lb = lb_ref[:, sl]
            f = lb + (1.0 - lb) * jax.nn.sigmoid(f_ref[0, pl.ds(t0, C), sl])
            v = v_ref[0, pl.ds(t0, C), sl]
            k = 1.0 - f
            b = jnp.log(f)
            s = 1
            while s < C:
                b = b + jnp.where(has_source(s), shift(b, s), 0.0)
                s *= 2
            st = s_ref[h]
            stb = st.astype(BF16)
            o = lax.dot_general((q * jnp.exp(b)).astype(BF16), stb, (((1,), (1,)), ((), ())),
                                preferred_element_type=F32)
            for d in range(C):
                arg = jnp.where(has_source(d), b - shift(b, d), -1e30)
                a = jnp.sum(q * shift(k, d) * jnp.exp(arg), axis=1, keepdims=True)
                o = o + a * shift(v, d)
            o_ref[0, pl.ds(t0, C), sl] = o
            end = 0 if reverse else C - 1
            bl = b[end:end + 1, :]
            kt = k * jnp.exp(bl - b)
            s_ref[h] = st * jnp.exp(bl) + lax.dot_general(v.astype(BF16), kt.astype(BF16), (((0,), (0,)), ((), ())),
                                                         preferred_element_type=F32)
        return carry

    lax.fori_loop(0, TB // C, block, 0)

    @pl.when(last)
    def _():
        sf_ref[0] = s_ref[...]


def hgrn2(zv, cols, lb, s0, seqs, *, reverse, heads_per_step=8):
    NB, TB, _ = zv.shape
    W = A_WIDTH
    H = W // LANES
    Hh = heads_per_step
    bw = Hh * LANES
    assert TB == ROW_BLOCK and H % Hh == 0 and all(c % bw == 0 for c in cols)

    def blk(j):
        return (NB - 1 - j) if reverse else j

    def in_spec(col):
        return pl.BlockSpec((1, TB, bw), lambda u, j: (blk(j), 0, col // bw + u))

    st_spec = pl.BlockSpec((1, Hh, LANES, LANES), lambda u, j: (seqs.locate(blk(j), TB)[2], u, 0, 0))
    st0 = jnp.swapaxes(s0, -1, -2)
    o, s = pl.pallas_call(
        functools.partial(_hgrn_body, seqs=seqs, NB=NB, Hh=Hh, reverse=reverse),
        out_shape=(jax.ShapeDtypeStruct((NB, TB, W), F32), jax.ShapeDtypeStruct(st0.shape, F32)),
        grid=(H // Hh, NB),
        in_specs=[in_spec(cols[0]), in_spec(cols[1]), in_spec(cols[2]),
                  pl.BlockSpec((1, bw), lambda u, j: (0, u)), st_spec],
        out_specs=(pl.BlockSpec((1, TB, bw), lambda u, j: (blk(j), 0, u)), st_spec),
        scratch_shapes=[pltpu.VMEM((Hh, LANES, LANES), F32)],
        compiler_params=_params("parallel", "arbitrary"),
        name="hgrn2",
    )(zv, zv, zv, lb, st0)
    return o.reshape(NB * TB, W), jnp.swapaxes(s, -1, -2)


def _even_post_body(oaf_ref, oab_ref, hg_ref, obf_ref, obb_ref, bonus_ref, g_ref, rowp_ref, hseg_ref, o_ref):
    hg_g, ln_g, ln_b = [rowp_ref[i:i + 1, :] for i in range(3)]
    oa = oaf_ref[0] + oab_ref[0]
    hg = hg_ref[0]
    oa = oa * lax.rsqrt(jnp.mean(oa * oa, axis=-1, keepdims=True) + NORM_EPS) * hg_g * (hg * jax.nn.sigmoid(hg))
    o_ref[0, :, :A_WIDTH] = oa.astype(o_ref.dtype)
    hseg = hseg_ref[...]
    ob = obf_ref[0] + obb_ref[0]
    cen = ob - _head_sums(ob, hseg) * (1.0 / B_HEAD)
    var = _head_sums(cen * cen, hseg) * (1.0 / B_HEAD)
    ob = (cen * lax.rsqrt(var + B_GN_EPS) * ln_g + ln_b + bonus_ref[0]) * g_ref[0]
    o_ref[0, :, A_WIDTH:] = ob.astype(o_ref.dtype)


def even_post(oa_f, oa_b, zv, ob_f, ob_b, bonus, g, p):
    NB, R, _ = zv.shape
    T = NB * R
    W = A_WIDTH
    assert W == B_WIDTH and (4 * A_WIDTH) % W == 0
    spec = pl.BlockSpec((1, R, W), lambda i: (i, 0, 0))
    rowp = jnp.stack([p['hg_g'], p['ln_g'], p['ln_b']] + [jnp.zeros((W,), F32)] * 5)
    v3 = lambda x: x.reshape(NB, R, W)
    out = pl.pallas_call(
        _even_post_body,
        out_shape=jax.ShapeDtypeStruct((NB, R, 2 * W), BF16),
        grid=(NB,),
        in_specs=[spec, spec, pl.BlockSpec((1, R, W), lambda i: (i, 0, 4 * A_WIDTH // W)), spec, spec, spec, spec,
                  pl.BlockSpec((SUBLANES, W), lambda i: (0, 0)), pl.BlockSpec((LANES, LANES), lambda i: (0, 0))],
        out_specs=pl.BlockSpec((1, R, 2 * W), lambda i: (i, 0, 0)),
        compiler_params=_params("parallel"),
        name="even_post",
    )(v3(oa_f), v3(oa_b), zv, v3(ob_f), v3(ob_b), bonus, g, rowp, _head_matrix())
    return out.reshape(T, 2 * W)


def even_mixer(z, ztail, seqs, st, p):
    T = z.shape[0]
    zv = z.reshape(T // ROW_BLOCK, ROW_BLOCK, -1)
    oa_f, sf = hgrn2(zv, (0, A_WIDTH, 3 * A_WIDTH), p['lb_f'][None], st['hf'], seqs, reverse=False)
    oa_b, sb = hgrn2(zv, (0, 2 * A_WIDTH, 3 * A_WIDTH), p['lb_b'][None], st['hb'], seqs, reverse=True)
    q = rwkv_prep(z, ztail, seqs, p)
    ob_f, rf = rwkv7((q['w_f'], q['b_f'], q['kd_f'], q['r'], q['kk']), q['v'], st['rf'], seqs, reverse=False,
                     passes=REC_PASSES)
    ob_b, rb = rwkv7((q['w_b'], q['b_b'], q['kd_b'], q['r'], q['kk']), q['v'], st['rb'], seqs, reverse=True,
                     passes=REC_PASSES)
    return even_post(oa_f, oa_b, zv, ob_f, ob_b, q['bonus'], q['g'], p), (sf, sb, rf, rb)


def _swap_halves(x):
    lane = lax.broadcasted_iota(jnp.int32, x.shape, 1)
    quarter = ROPE_AXIS_DIM // 2
    return jnp.where(lane % ROPE_AXIS_DIM < quarter, pltpu.roll(x, LANES - quarter, axis=1),
                     pltpu.roll(x, quarter, axis=1))


def _attn_body(*refs, has_ctx):
    if has_ctx:
        (q_ref, kn_ref, vn_ref, ck_ref, cv_ref, qc_ref, qs_ref, kc_ref, ks_ref, qg_ref, kg_ref,
         o_ref, k_sc, v_sc) = refs
    else:
        q_ref, kn_ref, vn_ref, qg_ref, kg_ref, o_ref, ko_ref, vo_ref, k_sc, v_sc = refs

    def norm(x, g_ref):
        return x * lax.rsqrt(jnp.mean(x * x, axis=-1, keepdims=True) + NORM_EPS) * g_ref[...]

    @pl.when(pl.program_id(2) == 0)
    def _():
        k = norm(kn_ref[0], kg_ref)
        v = vn_ref[0]
        if has_ctx:
            n_past = ck_ref.shape[1]
            k = k * kc_ref[...] + _swap_halves(k) * ks_ref[...]
            k_sc[:n_past] = ck_ref[0].astype(BF16)
            v_sc[:n_past] = cv_ref[0].astype(BF16)
            k_sc[n_past:] = k.astype(BF16)
            v_sc[n_past:] = v.astype(BF16)
        else:
            ko_ref[0] = k
            vo_ref[0] = v
            k_sc[...] = k.astype(BF16)
            v_sc[...] = v.astype(BF16)

    k = k_sc[...]
    v = v_sc[...]
    scale = C_HEAD_DIM ** -0.5
    for g in range(C_GROUP):
        sl = slice(g * C_HEAD_DIM, (g + 1) * C_HEAD_DIM)
        q = norm(q_ref[0, :, sl], qg_ref)
        if has_ctx:
            q = q * qc_ref[...] + _swap_halves(q) * qs_ref[...]
        s = lax.dot_general(q.astype(BF16), k, (((1,), (1,)), ((), ())), preferred_element_type=F32) * scale
        m = jnp.max(s, axis=-1, keepdims=True)
        e = jnp.exp(s - m)
        p = e / jnp.sum(e, axis=-1, keepdims=True)
        o_ref[0, :, sl] = jnp.dot(p.astype(BF16), v, preferred_element_type=F32).astype(o_ref.dtype)


def _rope_tables(L):
    n_rows = L // GRID_W
    row = jnp.repeat(jnp.arange(n_rows), GRID_W).astype(F32)
    col = jnp.tile(jnp.arange(GRID_W), n_rows).astype(F32)
    half = ROPE_AXIS_DIM // 2
    inv = ROPE_THETA ** (-jnp.arange(half, dtype=F32) / half)
    ar, ac = row[:, None] * inv, col[:, None] * inv
    cos_t = jnp.concatenate([jnp.cos(ar), jnp.cos(ar), jnp.cos(ac), jnp.cos(ac)], axis=1)
    sin_t = jnp.concatenate([-jnp.sin(ar), jnp.sin(ar), -jnp.sin(ac), jnp.sin(ac)], axis=1)
    return cos_t, sin_t


def attention(z, row0, B, L, qn_g, kn_g, ctx_k=None, ctx_v=None):
    T = z.shape[0]
    assert row0 % L == 0 and L % Q_BLOCK == 0
    has_ctx = ctx_k is not None
    gw = C_GROUP * C_HEAD_DIM
    zq = z.reshape(T // Q_BLOCK, Q_BLOCK, -1)
    zl = z.reshape(T // L, L, -1)
    qb0, lb0 = row0 // Q_BLOCK, row0 // L
    nq = L // Q_BLOCK
    kcol, vcol = C_WIDTH // C_HEAD_DIM, (C_WIDTH + C_KV_WIDTH) // C_HEAD_DIM
    g_spec = pl.BlockSpec((1, C_HEAD_DIM), lambda b, h, i: (0, 0))
    in_specs = [pl.BlockSpec((1, Q_BLOCK, gw), lambda b, h, i: (qb0 + b * nq + i, 0, h)),
                pl.BlockSpec((1, L, C_HEAD_DIM), lambda b, h, i: (lb0 + b, 0, kcol + h)),
                pl.BlockSpec((1, L, C_HEAD_DIM), lambda b, h, i: (lb0 + b, 0, vcol + h))]
    args = [zq, zl, zl]
    o_spec = pl.BlockSpec((1, Q_BLOCK, gw), lambda b, h, i: (b, i, h))
    o_shape = jax.ShapeDtypeStruct((B, L, C_WIDTH), BF16)
    if has_ctx:
        n_past = ctx_k.shape[1]
        c_spec = pl.BlockSpec((1, n_past, C_HEAD_DIM), lambda b, h, i: (b, 0, h))
        cos_t, sin_t = _rope_tables(L)
        q_tab = pl.BlockSpec((Q_BLOCK, C_HEAD_DIM), lambda b, h, i: (i, 0))
        k_tab = pl.BlockSpec((L, C_HEAD_DIM), lambda b, h, i: (0, 0))
        in_specs += [c_spec, c_spec, q_tab, q_tab, k_tab, k_tab]
        args += [ctx_k.reshape(B, n_past, C_KV_WIDTH), ctx_v.reshape(B, n_past, C_KV_WIDTH), cos_t, sin_t, cos_t, sin_t]
        out_shape, out_specs = o_shape, o_spec
        Lk = n_past + L
    else:
        kv_spec = pl.BlockSpec((1, L, C_HEAD_DIM), lambda b, h, i: (b, 0, h))
        kv_shape = jax.ShapeDtypeStruct((B, L, C_KV_WIDTH), F32)
        out_shape, out_specs = (o_shape, kv_shape, kv_shape), (o_spec, kv_spec, kv_spec)
        Lk = L
    in_specs += [g_spec, g_spec]
    args += [qn_g.reshape(1, C_HEAD_DIM), kn_g.reshape(1, C_HEAD_DIM)]
    res = pl.pallas_call(
        functools.partial(_attn_body, has_ctx=has_ctx),
        out_shape=out_shape, grid=(B, C_KV_HEADS, nq), in_specs=in_specs, out_specs=out_specs,
        scratch_shapes=[pltpu.VMEM((Lk, C_HEAD_DIM), BF16), pltpu.VMEM((Lk, C_HEAD_DIM), BF16)],
        compiler_params=_params("parallel", "parallel", "arbitrary"),
        name="attn",
    )(*args)
    if has_ctx:
        return res, None, None
    o, k, v = res
    return o, k.reshape(B, L, C_KV_HEADS, C_HEAD_DIM), v.reshape(B, L, C_KV_HEADS, C_HEAD_DIM)


def _dot3(ah, al, u):
    uh = u.astype(BF16)
    ul = (u - uh.astype(F32)).astype(BF16)
    return (jnp.dot(ah, uh, preferred_element_type=F32) + jnp.dot(ah, ul, preferred_element_type=F32)
            + jnp.dot(al, uh, preferred_element_type=F32))


def _conv3(x, w_ref, b_ref):
    L = x.shape[0]
    row = lax.broadcasted_iota(jnp.int32, x.shape, 0)
    prev = jnp.where(row == 0, 0.0, pltpu.roll(x, 1, axis=0))
    nxt = jnp.where(row == L - 1, 0.0, pltpu.roll(x, L - 1, axis=0))
    return prev * w_ref[0:1, :] + x * w_ref[1:2, :] + nxt * w_ref[2:3, :] + b_ref[...]


def _dft_fwd_body(*refs, conv):
    if conv:
        ah_ref, al_ref, u_ref, cw_ref, cb_ref, o_ref = refs
        u = _conv3(u_ref[0], cw_ref, cb_ref)
    else:
        ah_ref, al_ref, u_ref, o_ref = refs
        u = u_ref[0]
    o_ref[0] = _dot3(ah_ref[...], al_ref[...], u)


def _dft_inv_body(*refs, conv_u):
    if conv_u:
        ah_ref, al_ref, uf_ref, sc_ref, u_ref, cwu_ref, cbu_ref, x_ref, cwx_ref, cbx_ref, bias_ref, o_ref = refs
        u = _conv3(u_ref[0], cwu_ref, cbu_ref)
    else:
        ah_ref, al_ref, uf_ref, sc_ref, u_ref, x_ref, cwx_ref, cbx_ref, bias_ref, o_ref = refs
        u = u_ref[0]
    y = _dot3(ah_ref[...], al_ref[...], uf_ref[0] * sc_ref[...])
    gate = _conv3(x_ref[0], cwx_ref, cbx_ref)
    o_ref[0] = ((y + u * bias_ref[...]) * gate).astype(o_ref.dtype)


def dft_mm(a, u, *, bn=512):
    ah, al = a
    M, K = ah.shape
    B, _, C = u.shape
    a_spec = pl.BlockSpec((M, K), lambda b, j: (0, 0))
    return pl.pallas_call(
        functools.partial(_dft_fwd_body, conv=False),
        out_shape=jax.ShapeDtypeStruct((B, M, C), F32),
        grid=(B, C // bn),
        in_specs=[a_spec, a_spec, pl.BlockSpec((1, K, bn), lambda b, j: (b, 0, j))],
        out_specs=pl.BlockSpec((1, M, bn), lambda b, j: (b, 0, j)),
        compiler_params=_params("parallel", "parallel"),
        name="dft_mm",
    )(ah, al, u)


def hyena(z, row0, B, L, p):
    T = z.shape[0]
    N = 2 * L
    zl = z.reshape(T // L, L, -1)
    lb0 = row0 // L
    filt = _hyena_filters(L, p['f_w1'], p['f_b1'], p['f_w2'], p['f_b2'], p['f_w3'], p['f_b3'], p['f_w4'],
                          p['sin_freq'])
    fmat = _dft_matrix(L)
    fwd, inv = _split_bf16(fmat), _split_bf16(fmat.T)
    h_all = filt.reshape(L, HY_ORDER * HY_W)
    kf = dft_mm(fwd, jnp.concatenate([h_all[:1], 2.0 * h_all[1:]], axis=0)[None])[0]
    r = jnp.arange(N)
    herm = jnp.where((r == 0) | (r == L), 1.0, 2.0).astype(F32) / N
    scale = (kf[jnp.where(r <= L, r, r - L)] * herm[:, None]).reshape(N, HY_ORDER, HY_W)
    conv_w, conv_b = p['conv_w'], p['conv_b'].reshape(1, -1)
    bias = p['hy_bias']

    def z_spec(piece, bn):
        c0 = (C_IN + piece * HY_W) // bn
        return pl.BlockSpec((1, L, bn), lambda b, j: (lb0 + b, 0, c0 + j))

    def cw_specs(piece, bn):
        c0 = piece * HY_W // bn
        return [pl.BlockSpec((3, bn), lambda b, j: (0, c0 + j)), pl.BlockSpec((1, bn), lambda b, j: (0, c0 + j))]

    def forward(u, conv, bn=512):
        a_spec = pl.BlockSpec((N, L), lambda b, j: (0, 0))
        if conv:
            in_specs, args = [z_spec(0, bn)] + cw_specs(0, bn), [zl, conv_w, conv_b]
        else:
            in_specs, args = [pl.BlockSpec((1, L, bn), lambda b, j: (b, 0, j))], [u]
        return pl.pallas_call(
            functools.partial(_dft_fwd_body, conv=conv),
            out_shape=jax.ShapeDtypeStruct((B, N, HY_W), F32), grid=(B, HY_W // bn),
            in_specs=[a_spec, a_spec] + in_specs, out_specs=pl.BlockSpec((1, N, bn), lambda b, j: (b, 0, j)),
            compiler_params=_params("parallel", "parallel"), name="hyena_fwd",
        )(*fwd, *args)

    def inverse(uf, order, u, gate_piece, out_dtype, bn=256):
        a_spec = pl.BlockSpec((L, N), lambda b, j: (0, 0))
        blk = pl.BlockSpec((1, L, bn), lambda b, j: (b, 0, j))
        conv_u = u is None
        in_specs = [a_spec, a_spec, pl.BlockSpec((1, N, bn), lambda b, j: (b, 0, j)),
                    pl.BlockSpec((N, bn), lambda b, j: (0, j))]
        args = [*inv, uf, scale[:, order]]
        if conv_u:
            in_specs += [z_spec(0, bn)] + cw_specs(0, bn)
            args += [zl, conv_w, conv_b]
        else:
            in_specs.append(blk)
            args.append(u)
        in_specs += [z_spec(gate_piece, bn)] + cw_specs(gate_piece, bn) + [pl.BlockSpec((1, bn), lambda b, j: (0, j))]
        args += [zl, conv_w, conv_b, bias[order][None]]
        return pl.pallas_call(
            functools.partial(_dft_inv_body, conv_u=conv_u),
            out_shape=jax.ShapeDtypeStruct((B, L, HY_W), out_dtype), grid=(B, HY_W // bn),
            in_specs=in_specs, out_specs=blk,
            compiler_params=_params("parallel", "parallel"), name="hyena_inv",
        )(*args)

    z1 = inverse(forward(None, True), 0, None, 1, F32)
    return inverse(forward(z1, False), 1, z1, 2, BF16)


def _split_bf16(a):
    hi = a.astype(BF16)
    return hi, (a - hi.astype(F32)).astype(BF16)


def _dft_matrix(L):
    N = 2 * L
    r = jnp.arange(N, dtype=jnp.int32)[:, None]
    s = jnp.arange(L, dtype=jnp.int32)[None, :]
    is_cos = r <= L
    f = jnp.where(is_cos, r, r - L)
    ang = (2.0 * math.pi / N) * ((f * s) % N).astype(F32)
    return jnp.where(is_cos, jnp.cos(ang), jnp.sin(ang))


def _hyena_filters(L, w1, b1, w2, b2, w3, b3, w4, freq):
    t = jnp.linspace(0.0, 1.0, L, dtype=F32)[:, None]
    w_pos = 2.0 * math.pi * jnp.arange(L, dtype=F32)[:, None] / L
    bands = jnp.linspace(1e-4, HY_BANDS - 1, HY_BANDS, dtype=F32)[None]
    zpos = jnp.concatenate([t, jnp.cos(bands * w_pos), -jnp.sin(bands * w_pos)], axis=-1)
    f = freq.astype(F32)
    hdn = jnp.sin(f * (zpos @ w1 + b1))
    hdn = jnp.sin(f * (hdn @ w2 + b2))
    hdn = jnp.sin(f * (hdn @ w3 + b3))
    h = mm(hdn.astype(BF16), w4.astype(BF16), bm=L, bn=1024)
    deltas = jnp.abs(jnp.linspace(HY_MIN_DECAY, HY_MAX_DECAY, HY_ORDER * HY_W, dtype=F32))
    h = h * jnp.exp(-t * deltas)
    return h.reshape(L, HY_ORDER, HY_W)


def _odd_core(z, row0, B, L, ctx_k, ctx_v, p):
    o_c, k, v = attention(z, row0, B, L, p['qn_g'], p['kn_g'], ctx_k, ctx_v)
    return jnp.concatenate([o_c, hyena(z, row0, B, L, p)], axis=-1), k, v


def kernel(x_prompt, x_sample, c, state_hgrn_fwd, state_hgrn_bwd, state_rwkv_fwd, state_rwkv_bwd, cache_k, cache_v, c_ctx, ada_w, ada_b, norm_g, ffn_w1, ffn_w3, ffn_w2, final_norm_g, ev_w_in, ev_w_out, hg_lb_fwd, hg_lb_bwd, hg_norm_g, rw_mu, rw_w0_f, rw_w2_f, rw_w0_b, rw_w2_b, rw_a0_f, rw_a2_f, rw_a0_b, rw_a2_b, rw_g2, rw_kk, rw_ka, rw_rk, rw_ln_g, rw_ln_b, od_w_in, od_w_out, at_qn_g, at_kn_g, hy_conv_w, hy_conv_b, hy_f_w1, hy_f_b1, hy_f_w2, hy_f_b2, hy_f_w3, hy_f_b3, hy_f_w4, hy_sin_freq, hy_bias):
    Bc, Lc, _ = x_prompt.shape
    Bl, Ll, _ = x_sample.shape
    seqs = Seqs(Bc, Lc, Bl, Ll)
    n_ctx_tok = Bc * Lc
    n_tok = seqs.n_tok
    assert Lc % ROW_BLOCK == 0 and Ll % ROW_BLOCK == 0 and n_ctx_tok % Ll == 0

    def mod_row(i, rows):
        return jnp.maximum(i * rows // Ll - (n_ctx_tok // Ll - 1), 0)

    h = jnp.concatenate([x_prompt.reshape(n_ctx_tok, D_MODEL), x_sample.reshape(Bl * Ll, D_MODEL)], axis=0)
    cvec = jnp.concatenate([c_ctx[None], c], axis=0)
    cvec = jnp.pad(jax.nn.silu(cvec), ((0, SUBLANES - (1 + Bl)), (0, 0))).astype(BF16)
    mods = [(mm_small(cvec, ada_w, bn=1024, w_index=(l,)) + ada_b[l])[:1 + Bl].reshape(1 + Bl, N_MOD, D_MODEL)
            for l in range(DEPTH)]
    ffn_w2_bf16 = ffn_w2.astype(BF16)

    lb_fwd_all = jnp.cumsum(jax.nn.softmax(hg_lb_fwd, axis=0)[1:], axis=0)
    lb_bwd_all = jnp.cumsum(jax.nn.softmax(hg_lb_bwd, axis=0)[1:], axis=0)

    def with_ctx_zeros(s):
        return jnp.concatenate([jnp.zeros((Bc,) + s.shape[1:], F32), s], axis=0)

    def ffn(x, l, s, resid):
        mid = ffn_up(x, ffn_w1, ffn_w3, w_index=(l, s))
        return mm(mid, ffn_w2_bf16, bm=1024, bn=512, bk=D_FF // 2, w_index=(l, s), resid=resid)

    new_hf, new_hb, new_rf, new_rb, new_k, new_v = [], [], [], [], [], []
    x = ada_step(h, norm_g[0, 0], mod_row=mod_row, m_in=mods[0], i_in=0)
    for l in range(DEPTH):
        m = mods[l]
        h = ffn(x, l, 0, Resid(h, m, mod_row, 0, 0.5))
        xn = ada_step(h, norm_g[l, 1], mod_row=mod_row, m_in=m, i_in=1)
        j = l // 2
        if l % 2 == 0:
            p = dict(lb_f=lb_fwd_all[j], lb_b=lb_bwd_all[j], hg_g=hg_norm_g[j], mu=rw_mu[j],
                     w0f=rw_w0_f[j], w2f=rw_w2_f[j], w0b=rw_w0_b[j], w2b=rw_w2_b[j],
                     a0f=rw_a0_f[j], a2f=rw_a2_f[j], a0b=rw_a0_b[j], a2b=rw_a2_b[j], g2=rw_g2[j],
                     k_k=rw_kk[j], k_a=rw_ka[j], r_k=rw_rk[j], ln_g=rw_ln_g[j], ln_b=rw_ln_b[j])
            st = dict(hf=with_ctx_zeros(state_hgrn_fwd[:, j]), hb=with_ctx_zeros(state_hgrn_bwd[:, j]),
                      rf=with_ctx_zeros(state_rwkv_fwd[:, j]), rb=with_ctx_zeros(state_rwkv_bwd[:, j]))
            z = mm_ws(xn, ev_w_in, bm=1024, bn=512, w_index=(j,), n_cols=EV_MAIN)
            ztail = mm_ws(xn, ev_w_in[j, :, EV_MAIN:], bm=1024, bn=B_TAIL)
            o, (s_hf, s_hb, s_rf, s_rb) = even_mixer(z, ztail, seqs, st, p)
            new_hf.append(s_hf[:Bc])
            new_hb.append(s_hb[:Bc])
            new_rf.append(s_rf[:Bc])
            new_rb.append(s_rb[:Bc])
            w_out = ev_w_out
        else:
            p = dict(qn_g=at_qn_g[j], kn_g=at_kn_g[j], conv_w=hy_conv_w[j], conv_b=hy_conv_b[j],
                     f_w1=hy_f_w1[j], f_b1=hy_f_b1[j], f_w2=hy_f_w2[j], f_b2=hy_f_b2[j], f_w3=hy_f_w3[j],
                     f_b3=hy_f_b3[j], f_w4=hy_f_w4[j], sin_freq=hy_sin_freq[j], hy_bias=hy_bias[j])
            z = mm_ws(xn, od_w_in, bm=1024, bn=512, w_index=(j,))
            o_ctx, k_ctx, v_ctx = _odd_core(z, 0, Bc, Lc, None, None, p)
            o_lat, _, _ = _odd_core(z, n_ctx_tok, Bl, Ll, cache_k[:, j], cache_v[:, j], p)
            new_k.append(k_ctx)
            new_v.append(v_ctx)
            o = jnp.concatenate([o_ctx.reshape(n_ctx_tok, -1), o_lat.reshape(Bl * Ll, -1)], axis=0)
            w_out = od_w_out
        h = mm_ws(o, w_out, bm=1024, bn=512, w_index=(j,), resid=Resid(h, m, mod_row, 1, 1.0))
        x = ada_step(h, norm_g[l, 2], mod_row=mod_row, m_in=m, i_in=2)
        h = ffn(x, l, 1, Resid(h, m, mod_row, 2, 0.5))
        if l + 1 < DEPTH:
            x = ada_step(h, norm_g[l + 1, 0], mod_row=mod_row, m_in=mods[l + 1], i_in=0)
    y_out = ada_step(h, final_norm_g, mod_row=mod_row, out_dtype=F32)
    return (y_out[:n_ctx_tok].reshape(Bc, Lc, D_MODEL), y_out[n_ctx_tok:].reshape(Bl, Ll, D_MODEL),
            jnp.stack(new_hf, axis=1), jnp.stack(new_hb, axis=1), jnp.stack(new_rf, axis=1),
            jnp.stack(new_rb, axis=1), jnp.stack(new_k, axis=1), jnp.stack(new_v, axis=1))
```

```python
import functools
import math
from typing import NamedTuple

import numpy as np
import jax
import jax.numpy as jnp
from jax import lax
from jax.experimental import pallas as pl
from jax.experimental.pallas import tpu as pltpu

F32 = jnp.float32
BF16 = jnp.bfloat16

D_MODEL = 4096
DEPTH = 2
GRID_W = 64
N_MOD = 9
D_FF = 11008
NORM_EPS = 1e-6
A_WIDTH = D_MODEL // 2
A_HEADS = 16
A_DK = A_WIDTH // A_HEADS
A_DV = A_WIDTH // A_HEADS
B_WIDTH = D_MODEL // 2
B_HEAD = 64
B_HEADS = B_WIDTH // B_HEAD
B_DECAY_LORA = max(32, int(round(1.8 * B_WIDTH ** 0.5 / 32)) * 32)
B_AAA_LORA = max(32, int(round(1.8 * B_WIDTH ** 0.5 / 32)) * 32)
B_GATE_LORA = max(32, int(round(0.6 * B_WIDTH ** 0.8 / 32)) * 32)
B_TAIL = 2 * B_DECAY_LORA + 2 * B_AAA_LORA + B_GATE_LORA
B_GN_EPS = 64e-5
EV_A_IN = 5 * A_WIDTH
EV_B_IN = 3 * B_WIDTH + B_TAIL
EV_MAIN = EV_A_IN + 3 * B_WIDTH
C_HEAD_DIM = 128
C_HEADS = 16
C_KV_HEADS = 4
C_GROUP = C_HEADS // C_KV_HEADS
C_WIDTH = C_HEADS * C_HEAD_DIM
C_KV_WIDTH = C_KV_HEADS * C_HEAD_DIM
C_IN = C_WIDTH + 2 * C_KV_WIDTH
ROPE_AXIS_DIM = C_HEAD_DIM // 2
ROPE_THETA = 10000.0
Q_BLOCK = 128
HY_W = D_MODEL // 2
HY_ORDER = 2
HY_BANDS = 16
HY_TARGET = 1e-2
HY_MIN_DECAY = math.log(HY_TARGET) / 1.5
HY_MAX_DECAY = math.log(HY_TARGET) / 0.3

V7X_VMEM_LIMIT_BYTES = 56 * 1024 * 1024
LANES = 128
SUBLANES = 8
ROW_BLOCK = 256
REC_TIME_BLOCK = 128
REC_PASSES = (1, 1, 1)
REC_SUB_UNITS = 4
PREP_COLS = 512


def _params(*sem):
    return pltpu.CompilerParams(dimension_semantics=sem, vmem_limit_bytes=V7X_VMEM_LIMIT_BYTES)


class Seqs(NamedTuple):
    Bc: int
    Lc: int
    Bl: int
    Ll: int

    @property
    def n_tok(self):
        return self.Bc * self.Lc + self.Bl * self.Ll

    def locate(self, j, blk):
        n_ctx = self.Bc * self.Lc // blk
        cb, lb = self.Lc // blk, self.Ll // blk
        is_ctx = j < n_ctx
        jl = jnp.maximum(j - n_ctx, 0)
        pos = jnp.where(is_ctx, j % cb, jl % lb)
        nblk = jnp.where(is_ctx, cb, lb)
        sid = jnp.where(is_ctx, j // cb, self.Bc + jl // lb)
        return pos, nblk, sid


def _gated_residual(acc, h_ref, m_ref, i_out, w_out):
    return h_ref[...] + w_out * m_ref[0, 3 * i_out + 2:3 * i_out + 3, :] * acc


def _mm_body(x_ref, w_ref, *rest, nk, resid):
    o_ref = rest[-1]
    part = jnp.dot(x_ref[...], w_ref[...], preferred_element_type=F32)

    def finish(acc):
        return acc if resid is None else _gated_residual(acc, rest[0], rest[1], *resid)

    if nk == 1:
        o_ref[...] = finish(part).astype(o_ref.dtype)
    else:
        k = pl.program_id(2)

        @pl.when(k == 0)
        def _():
            o_ref[...] = part

        @pl.when((k > 0) & (k < nk - 1))
        def _():
            o_ref[...] += part

        @pl.when(k == nk - 1)
        def _():
            o_ref[...] = finish(o_ref[...] + part)


def _stacked(w_index):
    return (None,) * len(w_index)


class Resid(NamedTuple):
    h: jax.Array
    mods: jax.Array
    mod_row: object
    i_out: int
    w_out: float


def mm(x, w, *, bm, bn, bk=None, w_index=(), resid=None, out_dtype=F32):
    M, K = x.shape
    N = w.shape[-1]
    bk = K if bk is None else bk
    nk = K // bk
    assert M % bm == 0 and N % bn == 0 and K % bk == 0
    assert nk == 1 or out_dtype == F32
    in_specs = [pl.BlockSpec((bm, bk), lambda i, j, k: (i, k)),
                pl.BlockSpec(_stacked(w_index) + (bk, bn), lambda i, j, k: w_index + (k, j))]
    args = [x, w]
    if resid is not None:
        in_specs += [pl.BlockSpec((bm, bn), lambda i, j, k: (i, j)),
                     pl.BlockSpec((1, N_MOD, bn), lambda i, j, k: (resid.mod_row(i, bm), 0, j))]
        args += [resid.h, resid.mods]
    return pl.pallas_call(
        functools.partial(_mm_body, nk=nk, resid=None if resid is None else (resid.i_out, resid.w_out)),
        out_shape=jax.ShapeDtypeStruct((M, N), out_dtype),
        grid=(M // bm, N // bn, nk),
        in_specs=in_specs,
        out_specs=pl.BlockSpec((bm, bn), lambda i, j, k: (i, j)),
        compiler_params=_params("parallel", "parallel", "arbitrary"),
        name="mm",
    )(*args)


def _mm_ws_body(x_ref, w_ref, *rest, resid):
    o_ref, wb_ref = rest[-2:]

    @pl.when(pl.program_id(1) == 0)
    def _():
        wb_ref[...] = w_ref[...].astype(BF16)

    acc = jnp.dot(x_ref[...], wb_ref[...], preferred_element_type=F32)
    if resid is not None:
        acc = _gated_residual(acc, rest[0], rest[1], *resid)
    o_ref[...] = acc.astype(o_ref.dtype)


def mm_ws(x, w, *, bm, bn, w_index=(), n_cols=None, resid=None, out_dtype=F32):
    M, K = x.shape
    N = w.shape[-1] if n_cols is None else n_cols
    assert M % bm == 0 and N % bn == 0
    in_specs = [pl.BlockSpec((bm, K), lambda j, i: (i, 0)),
                pl.BlockSpec(_stacked(w_index) + (K, bn), lambda j, i: w_index + (0, j))]
    args = [x, w]
    if resid is not None:
        in_specs += [pl.BlockSpec((bm, bn), lambda j, i: (i, j)),
                     pl.BlockSpec((1, N_MOD, bn), lambda j, i: (resid.mod_row(i, bm), 0, j))]
        args += [resid.h, resid.mods]
    return pl.pallas_call(
        functools.partial(_mm_ws_body, resid=None if resid is None else (resid.i_out, resid.w_out)),
        out_shape=jax.ShapeDtypeStruct((M, N), out_dtype),
        grid=(N // bn, M // bm),
        in_specs=in_specs,
        out_specs=pl.BlockSpec((bm, bn), lambda j, i: (i, j)),
        scratch_shapes=[pltpu.VMEM((K, bn), BF16)],
        compiler_params=_params("arbitrary", "arbitrary"),
        name="mm_ws",
    )(*args)


def _mm_small_body(x_ref, w_ref, o_ref):
    o_ref[...] = jnp.dot(x_ref[...], w_ref[...].astype(BF16), preferred_element_type=F32)


def mm_small(x, w, *, bn, w_index=()):
    M, K = x.shape
    N = w.shape[-1]
    return pl.pallas_call(
        _mm_small_body,
        out_shape=jax.ShapeDtypeStruct((M, N), F32),
        grid=(N // bn,),
        in_specs=[pl.BlockSpec((M, K), lambda j: (0, 0)),
                  pl.BlockSpec(_stacked(w_index) + (K, bn), lambda j: w_index + (0, j))],
        out_specs=pl.BlockSpec((M, bn), lambda j: (0, j)),
        compiler_params=_params("parallel"),
        name="mm_small",
    )(x, w)


def _ffn_up_body(x_ref, w1_ref, w3_ref, o_ref, w1b_ref, w3b_ref):
    @pl.when(pl.program_id(1) == 0)
    def _():
        w1b_ref[...] = w1_ref[...].astype(BF16)
        w3b_ref[...] = w3_ref[...].astype(BF16)

    x = x_ref[...]
    a = jnp.dot(x, w1b_ref[...], preferred_element_type=F32)
    b = jnp.dot(x, w3b_ref[...], preferred_element_type=F32)
    o_ref[...] = (a * jax.nn.sigmoid(a) * b).astype(o_ref.dtype)


def ffn_up(x, w1, w3, *, w_index=(), bm=1024, bn=256):
    M, K = x.shape
    N = w1.shape[-1]
    w_spec = pl.BlockSpec(_stacked(w_index) + (K, bn), lambda j, i: w_index + (0, j))
    return pl.pallas_call(
        _ffn_up_body,
        out_shape=jax.ShapeDtypeStruct((M, N), BF16),
        grid=(N // bn, M // bm),
        in_specs=[pl.BlockSpec((bm, K), lambda j, i: (i, 0)), w_spec, w_spec],
        out_specs=pl.BlockSpec((bm, bn), lambda j, i: (i, j)),
        scratch_shapes=[pltpu.VMEM((K, bn), BF16), pltpu.VMEM((K, bn), BF16)],
        compiler_params=_params("arbitrary", "arbitrary"),
        name="ffn_up",
    )(x, w1, w3)


def _ada_body(*refs, i_in):
    h_ref, g_ref = refs[:2]
    x_ref = refs[-1]
    h = h_ref[...]
    x = h * lax.rsqrt(jnp.mean(h * h, axis=-1, keepdims=True) + NORM_EPS) * g_ref[...]
    if len(refs) == 4:
        mi_ref = refs[2]
        x = x * (1.0 + mi_ref[0, 3 * i_in + 1:3 * i_in + 2, :]) + mi_ref[0, 3 * i_in:3 * i_in + 1, :]
    x_ref[...] = x.astype(x_ref.dtype)


def ada_step(h, g, *, mod_row, m_in=None, i_in=0, out_dtype=BF16):
    T, D = h.shape
    R = ROW_BLOCK
    row_spec = pl.BlockSpec((R, D), lambda i: (i, 0))
    args, specs = [h, g.reshape(1, D)], [row_spec, pl.BlockSpec((1, D), lambda i: (0, 0))]
    if m_in is not None:
        args.append(m_in)
        specs.append(pl.BlockSpec((1, N_MOD, D), lambda i: (mod_row(i, R), 0, 0)))
    return pl.pallas_call(
        functools.partial(_ada_body, i_in=i_in),
        out_shape=jax.ShapeDtypeStruct((T, D), out_dtype), grid=(T // R,), in_specs=specs, out_specs=row_spec,
        compiler_params=_params("parallel"), name="ada_step",
    )(*args)


def _head_sums(x, hseg, n_pass=2):
    outs = []
    for c in range(x.shape[1] // LANES):
        rem = x[:, c * LANES:(c + 1) * LANES]
        acc = None
        for _ in range(n_pass):
            hi = rem.astype(BF16)
            part = jnp.dot(hi, hseg, preferred_element_type=F32)
            acc = part if acc is None else acc + part
            rem = rem - hi.astype(F32)
        outs.append(acc)
    return outs[0] if len(outs) == 1 else jnp.concatenate(outs, axis=1)


def _head_matrix():
    lane = jnp.arange(LANES)
    return (lane[:, None] // B_HEAD == lane[None, :] // B_HEAD).astype(BF16)


_TAIL_OFF = dict(wdf=0, wdb=B_DECAY_LORA, adf=2 * B_DECAY_LORA, adb=2 * B_DECAY_LORA + B_AAA_LORA,
                 gd=2 * B_DECAY_LORA + 2 * B_AAA_LORA)


def _prep_body(r_ref, k_ref, v_ref, t_ref, rp_ref, kp_ref, vp_ref, tp_ref, rn_ref, kn_ref, vn_ref, tn_ref,
               mu_r_ref, mu_k_ref, mu_v_ref, mu_t_ref, w2f_ref, w2b_ref, a2f_ref, a2b_ref, g2_ref, rowp_ref, hseg_ref,
               wf_ref, bf_ref, kdf_ref, wb_ref, bb_ref, kdb_ref, ro_ref, kko_ref, vo_ref, go_ref, bonus_ref,
               *, seqs):
    R = ROW_BLOCK
    pos, nblk, _ = seqs.locate(pl.program_id(0), R)
    has_prev = (pos != 0).astype(F32)
    has_next = (pos != nblk - 1).astype(F32)

    def shifted(cur_ref, p_ref, n_ref, mu_ref):
        x = cur_ref[0]
        row = lax.broadcasted_iota(jnp.int32, x.shape, 0)
        prev = jnp.where(row == 0, p_ref[0, SUBLANES - 1:SUBLANES, :] * has_prev, pltpu.roll(x, 1, axis=0))
        nxt = jnp.where(row == R - 1, n_ref[0, 0:1, :] * has_next, pltpu.roll(x, R - 1, axis=0))
        return x + mu_ref[...] * (0.5 * (prev + nxt) - x)

    rr = shifted(r_ref, rp_ref, rn_ref, mu_r_ref)
    kr = shifted(k_ref, kp_ref, kn_ref, mu_k_ref)
    vr = shifted(v_ref, vp_ref, vn_ref, mu_v_ref)
    tail = shifted(t_ref, tp_ref, tn_ref, mu_t_ref)
    t_tanh = jnp.tanh(tail).astype(BF16)
    t_lin = tail.astype(BF16)
    t_sig = jax.nn.sigmoid(tail).astype(BF16)

    def dot(a, w_ref):
        return jnp.dot(a, w_ref[...], preferred_element_type=F32)

    w0f, w0b, a0f, a0b, k_k, k_a, r_k = [rowp_ref[i:i + 1, :] for i in range(7)]
    hseg = hseg_ref[...]
    kk = kr * k_k
    kk = kk / jnp.maximum(jnp.sqrt(_head_sums(kk * kk, hseg)), 1e-12)
    for w0, a0, w2_ref, a2_ref, w_out, b_out, kd_out in ((w0f, a0f, w2f_ref, a2f_ref, wf_ref, bf_ref, kdf_ref),
                                                        (w0b, a0b, w2b_ref, a2b_ref, wb_ref, bb_ref, kdb_ref)):
        logw = -math.exp(-0.5) * jax.nn.sigmoid(w0 + dot(t_tanh, w2_ref))
        a = jax.nn.sigmoid(a0 + dot(t_lin, a2_ref))
        w_out[0] = jnp.exp(logw)
        b_out[0] = kk * a
        kd_out[0] = kr * (1.0 + (a - 1.0) * k_a)
    ro_ref[0] = rr
    kko_ref[0] = kk
    vo_ref[0] = vr
    go_ref[0] = dot(t_sig, g2_ref)
    bonus_ref[0] = _head_sums(rr * kr * r_k, hseg) * vr


def rwkv_prep(z, ztail, seqs, p):
    T = z.shape[0]
    R, CW = ROW_BLOCK, PREP_COLS
    NB, n8 = T // R, T // SUBLANES
    per = R // SUBLANES
    zv, z8 = z.reshape(NB, R, -1), z.reshape(n8, SUBLANES, -1)
    tv, t8 = ztail.reshape(NB, R, B_TAIL), ztail.reshape(n8, SUBLANES, B_TAIL)
    base = [(EV_A_IN + i * B_WIDTH) // CW for i in range(3)]

    def cur(b):
        return pl.BlockSpec((1, R, CW), lambda i, c: (i, 0, b + c))

    def prev(b):
        return pl.BlockSpec((1, SUBLANES, CW), lambda i, c: (jnp.maximum(i * per - 1, 0), 0, b + c))

    def nxt(b):
        return pl.BlockSpec((1, SUBLANES, CW), lambda i, c: (jnp.minimum((i + 1) * per, n8 - 1), 0, b + c))

    t_cur = pl.BlockSpec((1, R, B_TAIL), lambda i, c: (i, 0, 0))
    t_prev = pl.BlockSpec((1, SUBLANES, B_TAIL), lambda i, c: (jnp.maximum(i * per - 1, 0), 0, 0))
    t_next = pl.BlockSpec((1, SUBLANES, B_TAIL), lambda i, c: (jnp.minimum((i + 1) * per, n8 - 1), 0, 0))

    mu = p['mu']
    mu_main = mu[:3 * B_WIDTH].reshape(1, 3 * B_WIDTH)
    mu_t = mu[3 * B_WIDTH:].reshape(1, B_TAIL)

    def mu_spec(i):
        return pl.BlockSpec((1, CW), lambda r, c: (0, i * (B_WIDTH // CW) + c))

    def padded(w, off):
        return jnp.zeros((B_TAIL, B_WIDTH), BF16).at[off:off + w.shape[0]].set(w.astype(BF16))

    lora = [padded(p['w2f'], _TAIL_OFF['wdf']), padded(p['w2b'], _TAIL_OFF['wdb']),
            padded(p['a2f'], _TAIL_OFF['adf']), padded(p['a2b'], _TAIL_OFF['adb']), padded(p['g2'], _TAIL_OFF['gd'])]
    lora_spec = pl.BlockSpec((B_TAIL, CW), lambda i, c: (0, c))
    rowp = jnp.stack([p['w0f'], p['w0b'], p['a0f'], p['a0b'], p['k_k'], p['k_a'], p['r_k'].reshape(-1),
                      jnp.zeros((B_WIDTH,), F32)])
    out_spec = pl.BlockSpec((1, R, CW), lambda i, c: (i, 0, c))
    names = ('w_f', 'b_f', 'kd_f', 'w_b', 'b_b', 'kd_b', 'r', 'kk', 'v', 'g', 'bonus')
    outs = pl.pallas_call(
        functools.partial(_prep_body, seqs=seqs),
        out_shape=tuple(jax.ShapeDtypeStruct((NB, R, B_WIDTH), F32) for _ in names),
        grid=(NB, B_WIDTH // CW),
        in_specs=[cur(base[0]), cur(base[1]), cur(base[2]), t_cur,
                  prev(base[0]), prev(base[1]), prev(base[2]), t_prev,
                  nxt(base[0]), nxt(base[1]), nxt(base[2]), t_next,
                  mu_spec(0), mu_spec(1), mu_spec(2), pl.BlockSpec((1, B_TAIL), lambda i, c: (0, 0)),
                  lora_spec, lora_spec, lora_spec, lora_spec, lora_spec,
                  pl.BlockSpec((SUBLANES, CW), lambda i, c: (0, c)),
                  pl.BlockSpec((LANES, LANES), lambda i, c: (0, 0))],
        out_specs=tuple(out_spec for _ in names),
        compiler_params=_params("parallel", "parallel"),
        name="rwkv_prep",
    )(zv, zv, zv, tv, z8, z8, z8, t8, z8, z8, z8, t8, mu_main, mu_main, mu_main, mu_t, *lora, rowp, _head_matrix())
    return dict(zip(names, outs))


def _rwkv_body(w_ref, b_ref, kd_ref, r_ref, kk_ref, v_ref, g_ref, h_ref, t0_ref, o_ref, tf_ref,
               s_ref, vb_ref, xo_ref, *, seqs, NB, passes, reverse):
    row_refs = (w_ref, b_ref, kd_ref, r_ref, kk_ref)
    TB = REC_TIME_BLOCK
    P, R, UW = s_ref.shape
    jb = pl.program_id(0)
    jb = (NB - 1 - jb) if reverse else jb
    pos, nblk, _ = seqs.locate(jb, TB)
    first = pos == ((nblk - 1) if reverse else 0)
    last = pos == (0 if reverse else (nblk - 1))

    @pl.when(first)
    def _():
        s_ref[...] = t0_ref[0]

    G = g_ref[...][None]
    Gb = G.astype(BF16)
    Hm = h_ref[...]

    Q = REC_SUB_UNITS
    subs = [slice(q * Q, (q + 1) * Q) for q in range(P // Q)]

    def seg_sum(x, n_pass):
        x = x.reshape(Q * R, UW)
        hi = x.astype(BF16)
        acc = jnp.dot(hi, Hm, preferred_element_type=F32)
        rem = x
        for _ in range(n_pass - 1):
            rem = rem - hi.astype(F32)
            hi = rem.astype(BF16)
            acc = acc + jnp.dot(hi, Hm, preferred_element_type=F32)
        return acc.reshape(Q, R, UW)

    def unit_rows(tile, j, qs):
        return jnp.stack([tile[j:j + 1, p * UW:(p + 1) * UW] for p in range(qs.start, qs.stop)])

    sub_iota = lax.broadcasted_iota(jnp.int32, (SUBLANES, UW), 0)

    def group(i, carry):
        gi = (TB // SUBLANES - 1 - i) if reverse else i
        t8 = pl.multiple_of(gi * SUBLANES, SUBLANES)
        tiles = [r[0, pl.ds(t8, SUBLANES), :] for r in row_refs]
        v_tile = v_ref[0, pl.ds(t8, SUBLANES), :]
        v_tile_b = v_tile.astype(BF16)
        for j in range(SUBLANES):
            for qs in subs:
                if passes[0] == 1:
                    vb = jnp.dot((Gb * unit_rows(v_tile_b, j, qs)).reshape(Q * R, UW), Hm,
                                 preferred_element_type=F32).reshape(Q, R, UW)
                else:
                    vb = seg_sum(G * unit_rows(v_tile, j, qs), passes[0])
                vb_ref[j, qs] = vb
        for jj in range(SUBLANES):
            j = SUBLANES - 1 - jj if reverse else jj
            for qs in subs:
                w, b, kd, r, kk = [unit_rows(tile, j, qs) for tile in tiles]
                S = s_ref[qs]
                u = seg_sum(S * kk, passes[1])
                S = S * w - u * b + vb_ref[j, qs] * kd
                s_ref[qs] = S
                xo_ref[j, qs] = S * r
        for qs in subs:
            o_tiles = [jnp.zeros((SUBLANES, UW), F32) for _ in range(Q)]
            for j in range(SUBLANES):
                orow = jnp.sum(G * seg_sum(xo_ref[j, qs], passes[2]), axis=1, keepdims=True)
                o_tiles = [jnp.where(sub_iota == j, orow[p], o_tiles[p]) for p in range(Q)]
            for p in range(Q):
                lo = (qs.start + p) * UW
                o_ref[0, pl.ds(t8, SUBLANES), lo:lo + UW] = o_tiles[p]
        return carry

    lax.fori_loop(0, TB // SUBLANES, group, 0)

    @pl.when(last)
    def _():
        tf_ref[0] = s_ref[...]


def rwkv7(rows, v, s0, seqs, *, reverse, passes):
    T = v.shape[0] * v.shape[1]
    TB, UW, seg = REC_TIME_BLOCK, LANES, B_HEAD
    NB = T // TB
    hpu = UW // seg
    P = B_HEADS // hpu
    n_seq = s0.shape[0]
    t0 = s0.reshape(n_seq, P, hpu, seg, seg).transpose(0, 1, 3, 2, 4).reshape(n_seq, P, seg, UW)
    lane = jnp.arange(UW)
    g = (lane[None, :] % seg == jnp.arange(seg)[:, None]).astype(F32)

    def blk(j):
        return (NB - 1 - j) if reverse else j

    row_spec = pl.BlockSpec((1, TB, B_WIDTH), lambda j: (blk(j), 0, 0))
    st_spec = pl.BlockSpec((1, P, seg, UW), lambda j: (seqs.locate(blk(j), TB)[2], 0, 0, 0))
    o, s = pl.pallas_call(
        functools.partial(_rwkv_body, seqs=seqs, NB=NB, passes=passes, reverse=reverse),
        out_shape=(jax.ShapeDtypeStruct((NB, TB, B_WIDTH), F32), jax.ShapeDtypeStruct(t0.shape, F32)),
        grid=(NB,),
        in_specs=[row_spec] * 6 + [pl.BlockSpec((seg, UW), lambda j: (0, 0)),
                                   pl.BlockSpec((UW, UW), lambda j: (0, 0)), st_spec],
        out_specs=(row_spec, st_spec),
        scratch_shapes=[pltpu.VMEM((P, seg, UW), F32), pltpu.VMEM((SUBLANES, P, seg, UW), F32),
                        pltpu.VMEM((SUBLANES, P, seg, UW), F32)],
        compiler_params=_params("arbitrary"),
        name="rwkv7",
    )(*[x.reshape(NB, TB, B_WIDTH) for x in rows], v.reshape(NB, TB, B_WIDTH), g, _head_matrix(), t0)
    s = s.reshape(n_seq, P, seg, hpu, seg).transpose(0, 1, 3, 2, 4).reshape(s0.shape)
    return o.reshape(T, B_WIDTH), s


HG_STEP = 8


def _hgrn_body(q_ref, f_ref, v_ref, lb_ref, s0_ref, o_ref, sf_ref, s_ref, *, seqs, NB, Hh, reverse):
    C = HG_STEP
    TB = ROW_BLOCK
    jb = pl.program_id(1)
    jb = (NB - 1 - jb) if reverse else jb
    pos, nblk, _ = seqs.locate(jb, TB)
    first = pos == ((nblk - 1) if reverse else 0)
    last = pos == (0 if reverse else (nblk - 1))

    @pl.when(first)
    def _():
        s_ref[...] = s0_ref[0]

    row = lax.broadcasted_iota(jnp.int32, (C, LANES), 0)
    def shift(x, d):
        if d == 0:
            return x
        return pltpu.roll(x, (C - d) if reverse else d, axis=0)

    def has_source(d):
        return (row <= C - 1 - d) if reverse else (row >= d)

    def block(i, carry):
        ci = (TB // C - 1 - i) if reverse else i
        t0 = pl.multiple_of(ci * C, C)
        for h in range(Hh):
            sl = slice(h * LANES, (h + 1) * LANES)
            hq = q_ref[0, pl.ds(t0, C), sl]
            q = hq * jax.nn.sigmoid(hq) * A_DK ** -0.5
            lb = lb_ref[:, sl]
            f = lb + (1.0 - lb) * jax.nn.sigmoid(f_ref[0, pl.ds(t0, C), sl])
            v = v_ref[0, pl.ds(t0, C), sl]
            k = 1.0 - f
            b = jnp.log(f)
            s = 1
            while s < C:
                b = b + jnp.where(has_source(s), shift(b, s), 0.0)
                s *= 2
            st = s_ref[h]
            stb = st.astype(BF16)
            o = lax.dot_general((q * jnp.exp(b)).astype(BF16), stb, (((1,), (1,)), ((), ())),
                                preferred_element_type=F32)
            for d in range(C):
                arg = jnp.where(has_source(d), b - shift(b, d), -1e30)
                a = jnp.sum(q * shift(k, d) * jnp.exp(arg), axis=1, keepdims=True)
                o = o + a * shift(v, d)
            o_ref[0, pl.ds(t0, C), sl] = o
            end = 0 if reverse else C - 1
            bl = b[end:end + 1, :]
            kt = k * jnp.exp(bl - b)
            s_ref[h] = st * jnp.exp(bl) + lax.dot_general(v.astype(BF16), kt.astype(BF16), (((0,), (0,)), ((), ())),
                                                         preferred_element_type=F32)
        return carry

    lax.fori_loop(0, TB // C, block, 0)

    @pl.when(last)
    def _():
        sf_ref[0] = s_ref[...]


def hgrn2(zv, cols, lb, s0, seqs, *, reverse, heads_per_step=16):
    NB, TB, _ = zv.shape
    W = A_WIDTH
    H = W // LANES
    Hh = heads_per_step
    bw = Hh * LANES
    assert TB == ROW_BLOCK and H % Hh == 0 and all(c % bw == 0 for c in cols)

    def blk(j):
        return (NB - 1 - j) if reverse else j

    def in_spec(col):
        return pl.BlockSpec((1, TB, bw), lambda u, j: (blk(j), 0, col // bw + u))

    st_spec = pl.BlockSpec((1, Hh, LANES, LANES), lambda u, j: (seqs.locate(blk(j), TB)[2], u, 0, 0))
    st0 = jnp.swapaxes(s0, -1, -2)
    o, s = pl.pallas_call(
        functools.partial(_hgrn_body, seqs=seqs, NB=NB, Hh=Hh, reverse=reverse),
        out_shape=(jax.ShapeDtypeStruct((NB, TB, W), F32), jax.ShapeDtypeStruct(st0.shape, F32)),
        grid=(H // Hh, NB),
        in_specs=[in_spec(cols[0]), in_spec(cols[1]), in_spec(cols[2]),
                  pl.BlockSpec((1, bw), lambda u, j: (0, u)), st_spec],
        out_specs=(pl.BlockSpec((1, TB, bw), lambda u, j: (blk(j), 0, u)), st_spec),
        scratch_shapes=[pltpu.VMEM((Hh, LANES, LANES), F32)],
        compiler_params=_params("parallel", "arbitrary"),
        name="hgrn2",
    )(zv, zv, zv, lb, st0)
    return o.reshape(NB * TB, W), jnp.swapaxes(s, -1, -2)


def _even_post_body(oaf_ref, oab_ref, hg_ref, obf_ref, obb_ref, bonus_ref, g_ref, rowp_ref, hseg_ref, o_ref):
    hg_g, ln_g, ln_b = [rowp_ref[i:i + 1, :] for i in range(3)]
    oa = oaf_ref[0] + oab_ref[0]
    hg = hg_ref[0]
    oa = oa * lax.rsqrt(jnp.mean(oa * oa, axis=-1, keepdims=True) + NORM_EPS) * hg_g * (hg * jax.nn.sigmoid(hg))
    o_ref[0, :, :A_WIDTH] = oa.astype(o_ref.dtype)
    hseg = hseg_ref[...]
    ob = obf_ref[0] + obb_ref[0]
    cen = ob - _head_sums(ob, hseg) * (1.0 / B_HEAD)
    var = _head_sums(cen * cen, hseg) * (1.0 / B_HEAD)
    ob = (cen * lax.rsqrt(var + B_GN_EPS) * ln_g + ln_b + bonus_ref[0]) * g_ref[0]
    o_ref[0, :, A_WIDTH:] = ob.astype(o_ref.dtype)


def even_post(oa_f, oa_b, zv, ob_f, ob_b, bonus, g, p):
    NB, R, _ = zv.shape
    T = NB * R
    W = A_WIDTH
    assert W == B_WIDTH and (4 * A_WIDTH) % W == 0
    spec = pl.BlockSpec((1, R, W), lambda i: (i, 0, 0))
    rowp = jnp.stack([p['hg_g'], p['ln_g'], p['ln_b']] + [jnp.zeros((W,), F32)] * 5)
    v3 = lambda x: x.reshape(NB, R, W)
    out = pl.pallas_call(
        _even_post_body,
        out_shape=jax.ShapeDtypeStruct((NB, R, 2 * W), BF16),
        grid=(NB,),
        in_specs=[spec, spec, pl.BlockSpec((1, R, W), lambda i: (i, 0, 4 * A_WIDTH // W)), spec, spec, spec, spec,
                  pl.BlockSpec((SUBLANES, W), lambda i: (0, 0)), pl.BlockSpec((LANES, LANES), lambda i: (0, 0))],
        out_specs=pl.BlockSpec((1, R, 2 * W), lambda i: (i, 0, 0)),
        compiler_params=_params("parallel"),
        name="even_post",
    )(v3(oa_f), v3(oa_b), zv, v3(ob_f), v3(ob_b), bonus, g, rowp, _head_matrix())
    return out.reshape(T, 2 * W)


def even_mixer(z, ztail, seqs, st, p):
    T = z.shape[0]
    zv = z.reshape(T // ROW_BLOCK, ROW_BLOCK, -1)
    oa_f, sf = hgrn2(zv, (0, A_WIDTH, 3 * A_WIDTH), p['lb_f'][None], st['hf'], seqs, reverse=False)
    oa_b, sb = hgrn2(zv, (0, 2 * A_WIDTH, 3 * A_WIDTH), p['lb_b'][None], st['hb'], seqs, reverse=True)
    q = rwkv_prep(z, ztail, seqs, p)
    ob_f, rf = rwkv7((q['w_f'], q['b_f'], q['kd_f'], q['r'], q['kk']), q['v'], st['rf'], seqs, reverse=False,
                     passes=REC_PASSES)
    ob_b, rb = rwkv7((q['w_b'], q['b_b'], q['kd_b'], q['r'], q['kk']), q['v'], st['rb'], seqs, reverse=True,
                     passes=REC_PASSES)
    return even_post(oa_f, oa_b, zv, ob_f, ob_b, q['bonus'], q['g'], p), (sf, sb, rf, rb)


def _swap_halves(x):
    lane = lax.broadcasted_iota(jnp.int32, x.shape, 1)
    quarter = ROPE_AXIS_DIM // 2
    return jnp.where(lane % ROPE_AXIS_DIM < quarter, pltpu.roll(x, LANES - quarter, axis=1),
                     pltpu.roll(x, quarter, axis=1))


def _attn_body(*refs, has_ctx):
    if has_ctx:
        (q_ref, kn_ref, vn_ref, ck_ref, cv_ref, qc_ref, qs_ref, kc_ref, ks_ref, qg_ref, kg_ref,
         o_ref, k_sc, v_sc) = refs
    else:
        q_ref, kn_ref, vn_ref, qg_ref, kg_ref, o_ref, ko_ref, vo_ref, k_sc, v_sc = refs

    def norm(x, g_ref):
        return x * lax.rsqrt(jnp.mean(x * x, axis=-1, keepdims=True) + NORM_EPS) * g_ref[...]

    @pl.when(pl.program_id(2) == 0)
    def _():
        k = norm(kn_ref[0], kg_ref)
        v = vn_ref[0]
        if has_ctx:
            n_past = ck_ref.shape[1]
            k = k * kc_ref[...] + _swap_halves(k) * ks_ref[...]
            k_sc[:n_past] = ck_ref[0].astype(BF16)
            v_sc[:n_past] = cv_ref[0].astype(BF16)
            k_sc[n_past:] = k.astype(BF16)
            v_sc[n_past:] = v.astype(BF16)
        else:
            ko_ref[0] = k
            vo_ref[0] = v
            k_sc[...] = k.astype(BF16)
            v_sc[...] = v.astype(BF16)

    k = k_sc[...]
    v = v_sc[...]
    scale = C_HEAD_DIM ** -0.5
    for g in range(C_GROUP):
        sl = slice(g * C_HEAD_DIM, (g + 1) * C_HEAD_DIM)
        q = norm(q_ref[0, :, sl], qg_ref)
        if has_ctx:
            q = q * qc_ref[...] + _swap_halves(q) * qs_ref[...]
        s = lax.dot_general(q.astype(BF16), k, (((1,), (1,)), ((), ())), preferred_element_type=F32) * scale
        m = jnp.max(s, axis=-1, keepdims=True)
        e = jnp.exp(s - m)
        p = e / jnp.sum(e, axis=-1, keepdims=True)
        o_ref[0, :, sl] = jnp.dot(p.astype(BF16), v, preferred_element_type=F32).astype(o_ref.dtype)


def _rope_tables(L):
    n_rows = L // GRID_W
    row = jnp.repeat(jnp.arange(n_rows), GRID_W).astype(F32)
    col = jnp.tile(jnp.arange(GRID_W), n_rows).astype(F32)
    half = ROPE_AXIS_DIM // 2
    inv = ROPE_THETA ** (-jnp.arange(half, dtype=F32) / half)
    ar, ac = row[:, None] * inv, col[:, None] * inv
    cos_t = jnp.concatenate([jnp.cos(ar), jnp.cos(ar), jnp.cos(ac), jnp.cos(ac)], axis=1)
    sin_t = jnp.concatenate([-jnp.sin(ar), jnp.sin(ar), -jnp.sin(ac), jnp.sin(ac)], axis=1)
    return cos_t, sin_t


def attention(z, row0, B, L, qn_g, kn_g, ctx_k=None, ctx_v=None):
    T = z.shape[0]
    assert row0 % L == 0 and L % Q_BLOCK == 0
    has_ctx = ctx_k is not None
    gw = C_GROUP * C_HEAD_DIM
    zq = z.reshape(T // Q_BLOCK, Q_BLOCK, -1)
    zl = z.reshape(T // L, L, -1)
    qb0, lb0 = row0 // Q_BLOCK, row0 // L
    nq = L // Q_BLOCK
    kcol, vcol = C_WIDTH // C_HEAD_DIM, (C_WIDTH + C_KV_WIDTH) // C_HEAD_DIM
    g_spec = pl.BlockSpec((1, C_HEAD_DIM), lambda b, h, i: (0, 0))
    in_specs = [pl.BlockSpec((1, Q_BLOCK, gw), lambda b, h, i: (qb0 + b * nq + i, 0, h)),
                pl.BlockSpec((1, L, C_HEAD_DIM), lambda b, h, i: (lb0 + b, 0, kcol + h)),
                pl.BlockSpec((1, L, C_HEAD_DIM), lambda b, h, i: (lb0 + b, 0, vcol + h))]
    args = [zq, zl, zl]
    o_spec = pl.BlockSpec((1, Q_BLOCK, gw), lambda b, h, i: (b, i, h))
    o_shape = jax.ShapeDtypeStruct((B, L, C_WIDTH), BF16)
    if has_ctx:
        n_past = ctx_k.shape[1]
        c_spec = pl.BlockSpec((1, n_past, C_HEAD_DIM), lambda b, h, i: (b, 0, h))
        cos_t, sin_t = _rope_tables(L)
        q_tab = pl.BlockSpec((Q_BLOCK, C_HEAD_DIM), lambda b, h, i: (i, 0))
        k_tab = pl.BlockSpec((L, C_HEAD_DIM), lambda b, h, i: (0, 0))
        in_specs += [c_spec, c_spec, q_tab, q_tab, k_tab, k_tab]
        args += [ctx_k.reshape(B, n_past, C_KV_WIDTH), ctx_v.reshape(B, n_past, C_KV_WIDTH), cos_t, sin_t, cos_t, sin_t]
        out_shape, out_specs = o_shape, o_spec
        Lk = n_past + L
    else:
        kv_spec = pl.BlockSpec((1, L, C_HEAD_DIM), lambda b, h, i: (b, 0, h))
        kv_shape = jax.ShapeDtypeStruct((B, L, C_KV_WIDTH), F32)
        out_shape, out_specs = (o_shape, kv_shape, kv_shape), (o_spec, kv_spec, kv_spec)
        Lk = L
    in_specs += [g_spec, g_spec]
    args += [qn_g.reshape(1, C_HEAD_DIM), kn_g.reshape(1, C_HEAD_DIM)]
    res = pl.pallas_call(
        functools.partial(_attn_body, has_ctx=has_ctx),
        out_shape=out_shape, grid=(B, C_KV_HEADS, nq), in_specs=in_specs, out_specs=out_specs,
        scratch_shapes=[pltpu.VMEM((Lk, C_HEAD_DIM), BF16), pltpu.VMEM((Lk, C_HEAD_DIM), BF16)],
        compiler_params=_params("parallel", "parallel", "arbitrary"),
        name="attn",
    )(*args)
    if has_ctx:
        return res, None, None
    o, k, v = res
    return o, k.reshape(B, L, C_KV_HEADS, C_HEAD_DIM), v.reshape(B, L, C_KV_HEADS, C_HEAD_DIM)


def _dot3(ah, al, u):
    uh = u.astype(BF16)
    ul = (u - uh.astype(F32)).astype(BF16)
    return (jnp.dot(ah, uh, preferred_element_type=F32) + jnp.dot(ah, ul, preferred_element_type=F32)
            + jnp.dot(al, uh, preferred_element_type=F32))


def _conv3(x, w_ref, b_ref):
    L = x.shape[0]
    row = lax.broadcasted_iota(jnp.int32, x.shape, 0)
    prev = jnp.where(row == 0, 0.0, pltpu.roll(x, 1, axis=0))
    nxt = jnp.where(row == L - 1, 0.0, pltpu.roll(x, L - 1, axis=0))
    return prev * w_ref[0:1, :] + x * w_ref[1:2, :] + nxt * w_ref[2:3, :] + b_ref[...]


def _dft_fwd_body(*refs, conv):
    if conv:
        ah_ref, al_ref, u_ref, cw_ref, cb_ref, o_ref = refs
        u = _conv3(u_ref[0], cw_ref, cb_ref)
    else:
        ah_ref, al_ref, u_ref, o_ref = refs
        u = u_ref[0]
    o_ref[0] = _dot3(ah_ref[...], al_ref[...], u)


def _dft_inv_body(*refs, conv_u):
    if conv_u:
        ah_ref, al_ref, uf_ref, sc_ref, u_ref, cwu_ref, cbu_ref, x_ref, cwx_ref, cbx_ref, bias_ref, o_ref = refs
        u = _conv3(u_ref[0], cwu_ref, cbu_ref)
    else:
        ah_ref, al_ref, uf_ref, sc_ref, u_ref, x_ref, cwx_ref, cbx_ref, bias_ref, o_ref = refs
        u = u_ref[0]
    y = _dot3(ah_ref[...], al_ref[...], uf_ref[0] * sc_ref[...])
    gate = _conv3(x_ref[0], cwx_ref, cbx_ref)
    o_ref[0] = ((y + u * bias_ref[...]) * gate).astype(o_ref.dtype)


def dft_mm(a, u, *, bn=512):
    ah, al = a
    M, K = ah.shape
    B, _, C = u.shape
    a_spec = pl.BlockSpec((M, K), lambda b, j: (0, 0))
    return pl.pallas_call(
        functools.partial(_dft_fwd_body, conv=False),
        out_shape=jax.ShapeDtypeStruct((B, M, C), F32),
        grid=(B, C // bn),
        in_specs=[a_spec, a_spec, pl.BlockSpec((1, K, bn), lambda b, j: (b, 0, j))],
        out_specs=pl.BlockSpec((1, M, bn), lambda b, j: (b, 0, j)),
        compiler_params=_params("parallel", "parallel"),
        name="dft_mm",
    )(ah, al, u)


def hyena(z, row0, B, L, p):
    T = z.shape[0]
    N = 2 * L
    zl = z.reshape(T // L, L, -1)
    lb0 = row0 // L
    filt = _hyena_filters(L, p['f_w1'], p['f_b1'], p['f_w2'], p['f_b2'], p['f_w3'], p['f_b3'], p['f_w4'],
                          p['sin_freq'])
    fmat = _dft_matrix(L)
    fwd, inv = _split_bf16(fmat), _split_bf16(fmat.T)
    h_all = filt.reshape(L, HY_ORDER * HY_W)
    kf = dft_mm(fwd, jnp.concatenate([h_all[:1], 2.0 * h_all[1:]], axis=0)[None])[0]
    r = jnp.arange(N)
    herm = jnp.where((r == 0) | (r == L), 1.0, 2.0).astype(F32) / N
    scale = (kf[jnp.where(r <= L, r, r - L)] * herm[:, None]).reshape(N, HY_ORDER, HY_W)
    conv_w, conv_b = p['conv_w'], p['conv_b'].reshape(1, -1)
    bias = p['hy_bias']

    def z_spec(piece, bn):
        c0 = (C_IN + piece * HY_W) // bn
        return pl.BlockSpec((1, L, bn), lambda b, j: (lb0 + b, 0, c0 + j))

    def cw_specs(piece, bn):
        c0 = piece * HY_W // bn
        return [pl.BlockSpec((3, bn), lambda b, j: (0, c0 + j)), pl.BlockSpec((1, bn), lambda b, j: (0, c0 + j))]

    def forward(u, conv, bn=512):
        a_spec = pl.BlockSpec((N, L), lambda b, j: (0, 0))
        if conv:
            in_specs, args = [z_spec(0, bn)] + cw_specs(0, bn), [zl, conv_w, conv_b]
        else:
            in_specs, args = [pl.BlockSpec((1, L, bn), lambda b, j: (b, 0, j))], [u]
        return pl.pallas_call(
            functools.partial(_dft_fwd_body, conv=conv),
            out_shape=jax.ShapeDtypeStruct((B, N, HY_W), F32), grid=(B, HY_W // bn),
            in_specs=[a_spec, a_spec] + in_specs, out_specs=pl.BlockSpec((1, N, bn), lambda b, j: (b, 0, j)),
            compiler_params=_params("parallel", "parallel"), name="hyena_fwd",
        )(*fwd, *args)

    def inverse(uf, order, u, gate_piece, out_dtype, bn=256):
        a_spec = pl.BlockSpec((L, N), lambda b, j: (0, 0))
        blk = pl.BlockSpec((1, L, bn), lambda b, j: (b, 0, j))
        conv_u = u is None
        in_specs = [a_spec, a_spec, pl.BlockSpec((1, N, bn), lambda b, j: (b, 0, j)),
                    pl.BlockSpec((N, bn), lambda b, j: (0, j))]
        args = [*inv, uf, scale[:, order]]
        if conv_u:
            in_specs += [z_spec(0, bn)] + cw_specs(0, bn)
            args += [zl, conv_w, conv_b]
        else:
            in_specs.append(blk)
            args.append(u)
        in_specs += [z_spec(gate_piece, bn)] + cw_specs(gate_piece, bn) + [pl.BlockSpec((1, bn), lambda b, j: (0, j))]
        args += [zl, conv_w, conv_b, bias[order][None]]
        return pl.pallas_call(
            functools.partial(_dft_inv_body, conv_u=conv_u),
            out_shape=jax.ShapeDtypeStruct((B, L, HY_W), out_dtype), grid=(B, HY_W // bn),
            in_specs=in_specs, out_specs=blk,
            compiler_params=_params("parallel", "parallel"), name="hyena_inv",
        )(*args)

    z1 = inverse(forward(None, True), 0, None, 1, F32)
    return inverse(forward(z1, False), 1, z1, 2, BF16)


def _split_bf16(a):
    hi = a.astype(BF16)
    return hi, (a - hi.astype(F32)).astype(BF16)


def _dft_matrix(L):
    N = 2 * L
    r = jnp.arange(N, dtype=jnp.int32)[:, None]
    s = jnp.arange(L, dtype=jnp.int32)[None, :]
    is_cos = r <= L
    f = jnp.where(is_cos, r, r - L)
    ang = (2.0 * math.pi / N) * ((f * s) % N).astype(F32)
    return jnp.where(is_cos, jnp.cos(ang), jnp.sin(ang))


def _hyena_filters(L, w1, b1, w2, b2, w3, b3, w4, freq):
    t = jnp.linspace(0.0, 1.0, L, dtype=F32)[:, None]
    w_pos = 2.0 * math.pi * jnp.arange(L, dtype=F32)[:, None] / L
    bands = jnp.linspace(1e-4, HY_BANDS - 1, HY_BANDS, dtype=F32)[None]
    zpos = jnp.concatenate([t, jnp.cos(bands * w_pos), -jnp.sin(bands * w_pos)], axis=-1)
    f = freq.astype(F32)
    hdn = jnp.sin(f * (zpos @ w1 + b1))
    hdn = jnp.sin(f * (hdn @ w2 + b2))
    hdn = jnp.sin(f * (hdn @ w3 + b3))
    h = mm(hdn.astype(BF16), w4.astype(BF16), bm=L, bn=1024)
    deltas = jnp.abs(jnp.linspace(HY_MIN_DECAY, HY_MAX_DECAY, HY_ORDER * HY_W, dtype=F32))
    h = h * jnp.exp(-t * deltas)
    return h.reshape(L, HY_ORDER, HY_W)


def _odd_core(z, row0, B, L, ctx_k, ctx_v, p):
    o_c, k, v = attention(z, row0, B, L, p['qn_g'], p['kn_g'], ctx_k, ctx_v)
    return jnp.concatenate([o_c, hyena(z, row0, B, L, p)], axis=-1), k, v


def kernel(x_prompt, x_sample, c, state_hgrn_fwd, state_hgrn_bwd, state_rwkv_fwd, state_rwkv_bwd, cache_k, cache_v, c_ctx, ada_w, ada_b, norm_g, ffn_w1, ffn_w3, ffn_w2, final_norm_g, ev_w_in, ev_w_out, hg_lb_fwd, hg_lb_bwd, hg_norm_g, rw_mu, rw_w0_f, rw_w2_f, rw_w0_b, rw_w2_b, rw_a0_f, rw_a2_f, rw_a0_b, rw_a2_b, rw_g2, rw_kk, rw_ka, rw_rk, rw_ln_g, rw_ln_b, od_w_in, od_w_out, at_qn_g, at_kn_g, hy_conv_w, hy_conv_b, hy_f_w1, hy_f_b1, hy_f_w2, hy_f_b2, hy_f_w3, hy_f_b3, hy_f_w4, hy_sin_freq, hy_bias):
    Bc, Lc, _ = x_prompt.shape
    Bl, Ll, _ = x_sample.shape
    seqs = Seqs(Bc, Lc, Bl, Ll)
    n_ctx_tok = Bc * Lc
    n_tok = seqs.n_tok
    assert Lc % ROW_BLOCK == 0 and Ll % ROW_BLOCK == 0 and n_ctx_tok % Ll == 0

    def mod_row(i, rows):
        return jnp.maximum(i * rows // Ll - (n_ctx_tok // Ll - 1), 0)

    h = jnp.concatenate([x_prompt.reshape(n_ctx_tok, D_MODEL), x_sample.reshape(Bl * Ll, D_MODEL)], axis=0)
    cvec = jnp.concatenate([c_ctx[None], c], axis=0)
    cvec = jnp.pad(jax.nn.silu(cvec), ((0, SUBLANES - (1 + Bl)), (0, 0))).astype(BF16)
    mods = [(mm_small(cvec, ada_w, bn=1024, w_index=(l,)) + ada_b[l])[:1 + Bl].reshape(1 + Bl, N_MOD, D_MODEL)
            for l in range(DEPTH)]
    ffn_w2_bf16 = ffn_w2.astype(BF16)

    lb_fwd_all = jnp.cumsum(jax.nn.softmax(hg_lb_fwd, axis=0)[1:], axis=0)
    lb_bwd_all = jnp.cumsum(jax.nn.softmax(hg_lb_bwd, axis=0)[1:], axis=0)

    def with_ctx_zeros(s):
        return jnp.concatenate([jnp.zeros((Bc,) + s.shape[1:], F32), s], axis=0)

    def ffn(x, l, s, resid):
        mid = ffn_up(x, ffn_w1, ffn_w3, w_index=(l, s))
        return mm(mid, ffn_w2_bf16, bm=1024, bn=512, bk=D_FF // 2, w_index=(l, s), resid=resid)

    new_hf, new_hb, new_rf, new_rb, new_k, new_v = [], [], [], [], [], []
    x = ada_step(h, norm_g[0, 0], mod_row=mod_row, m_in=mods[0], i_in=0)
    for l in range(DEPTH):
        m = mods[l]
        h = ffn(x, l, 0, Resid(h, m, mod_row, 0, 0.5))
        xn = ada_step(h, norm_g[l, 1], mod_row=mod_row, m_in=m, i_in=1)
        j = l // 2
        if l % 2 == 0:
            p = dict(lb_f=lb_fwd_all[j], lb_b=lb_bwd_all[j], hg_g=hg_norm_g[j], mu=rw_mu[j],
                     w0f=rw_w0_f[j], w2f=rw_w2_f[j], w0b=rw_w0_b[j], w2b=rw_w2_b[j],
                     a0f=rw_a0_f[j], a2f=rw_a2_f[j], a0b=rw_a0_b[j], a2b=rw_a2_b[j], g2=rw_g2[j],
                     k_k=rw_kk[j], k_a=rw_ka[j], r_k=rw_rk[j], ln_g=rw_ln_g[j], ln_b=rw_ln_b[j])
            st = dict(hf=with_ctx_zeros(state_hgrn_fwd[:, j]), hb=with_ctx_zeros(state_hgrn_bwd[:, j]),
                      rf=with_ctx_zeros(state_rwkv_fwd[:, j]), rb=with_ctx_zeros(state_rwkv_bwd[:, j]))
            z = mm_ws(xn, ev_w_in, bm=1024, bn=512, w_index=(j,), n_cols=EV_MAIN)
            ztail = mm_ws(xn, ev_w_in[j, :, EV_MAIN:], bm=1024, bn=B_TAIL)
            o, (s_hf, s_hb, s_rf, s_rb) = even_mixer(z, ztail, seqs, st, p)
            new_hf.append(s_hf[:Bc])
            new_hb.append(s_hb[:Bc])
            new_rf.append(s_rf[:Bc])
            new_rb.append(s_rb[:Bc])
            w_out = ev_w_out
        else:
            p = dict(qn_g=at_qn_g[j], kn_g=at_kn_g[j], conv_w=hy_conv_w[j], conv_b=hy_conv_b[j],
                     f_w1=hy_f_w1[j], f_b1=hy_f_b1[j], f_w2=hy_f_w2[j], f_b2=hy_f_b2[j], f_w3=hy_f_w3[j],
                     f_b3=hy_f_b3[j], f_w4=hy_f_w4[j], sin_freq=hy_sin_freq[j], hy_bias=hy_bias[j])
            z = mm_ws(xn, od_w_in, bm=1024, bn=512, w_index=(j,))
            o_ctx, k_ctx, v_ctx = _odd_core(z, 0, Bc, Lc, None, None, p)
            o_lat, _, _ = _odd_core(z, n_ctx_tok, Bl, Ll, cache_k[:, j], cache_v[:, j], p)
            new_k.append(k_ctx)
            new_v.append(v_ctx)
            o = jnp.concatenate([o_ctx.reshape(n_ctx_tok, -1), o_lat.reshape(Bl * Ll, -1)], axis=0)
            w_out = od_w_out
        h = mm_ws(o, w_out, bm=1024, bn=512, w_index=(j,), resid=Resid(h, m, mod_row, 1, 1.0))
        x = ada_step(h, norm_g[l, 2], mod_row=mod_row, m_in=m, i_in=2)
        h = ffn(x, l, 1, Resid(h, m, mod_row, 2, 0.5))
        if l + 1 < DEPTH:
            x = ada_step(h, norm_g[l + 1, 0], mod_row=mod_row, m_in=mods[l + 1], i_in=0)
    y_out = ada_step(h, final_norm_g, mod_row=mod_row, out_dtype=F32)
    return (y_out[:n_ctx_tok].reshape(Bc, Lc, D_MODEL), y_out[n_ctx_tok:].reshape(Bl, Ll, D_MODEL),
            jnp.stack(new_hf, axis=1), jnp.stack(new_hb, axis=1), jnp.stack(new_rf, axis=1),
            jnp.stack(new_rb, axis=1), jnp.stack(new_k, axis=1), jnp.stack(new_v, axis=1))
```

```python
import functools
import math
from typing import NamedTuple

import jax
import jax.numpy as jnp
from jax import lax
from jax.experimental import pallas as pl
from jax.experimental.pallas import tpu as pltpu

F32 = jnp.float32
BF16 = jnp.bfloat16

D_MODEL = 4096
DEPTH = 2
GRID_W = 64
N_MOD = 9
D_FF = 11008
NORM_EPS = 1e-6
A_WIDTH = D_MODEL // 2
A_HEADS = 16
A_DK = A_WIDTH // A_HEADS
B_WIDTH = D_MODEL // 2
B_HEAD = 64
B_HEADS = B_WIDTH // B_HEAD
B_DECAY_LORA = max(32, int(round(1.8 * B_WIDTH ** 0.5 / 32)) * 32)
B_AAA_LORA = max(32, int(round(1.8 * B_WIDTH ** 0.5 / 32)) * 32)
B_GATE_LORA = max(32, int(round(0.6 * B_WIDTH ** 0.8 / 32)) * 32)
B_TAIL = 2 * B_DECAY_LORA + 2 * B_AAA_LORA + B_GATE_LORA
B_GN_EPS = 64e-5
EV_A_IN = 5 * A_WIDTH
EV_B_IN = 3 * B_WIDTH + B_TAIL
EV_MAIN = EV_A_IN + 3 * B_WIDTH
C_HEAD_DIM = 128
C_HEADS = 16
C_KV_HEADS = 4
C_GROUP = C_HEADS // C_KV_HEADS
C_WIDTH = C_HEADS * C_HEAD_DIM
C_KV_WIDTH = C_KV_HEADS * C_HEAD_DIM
C_IN = C_WIDTH + 2 * C_KV_WIDTH
ROPE_AXIS_DIM = C_HEAD_DIM // 2
ROPE_THETA = 10000.0
Q_BLOCK = 128
HY_W = D_MODEL // 2
HY_ORDER = 2
HY_BANDS = 16
HY_TARGET = 1e-2
HY_MIN_DECAY = math.log(HY_TARGET) / 1.5
HY_MAX_DECAY = math.log(HY_TARGET) / 0.3

V7X_VMEM_LIMIT_BYTES = 56 * 1024 * 1024
LANES = 128
SUBLANES = 8
ROW_BLOCK = 256
REC_TIME_BLOCK = 128
REC_SUB_UNITS = 4
PREP_COLS = 512


def _params(*sem):
    return pltpu.CompilerParams(dimension_semantics=sem, vmem_limit_bytes=V7X_VMEM_LIMIT_BYTES)


class Seqs(NamedTuple):
    Bc: int
    Lc: int
    Bl: int
    Ll: int

    @property
    def n_tok(self):
        return self.Bc * self.Lc + self.Bl * self.Ll

    def locate(self, j, blk):
        n_ctx = self.Bc * self.Lc // blk
        cb, lb = self.Lc // blk, self.Ll // blk
        is_ctx = j < n_ctx
        jl = jnp.maximum(j - n_ctx, 0)
        pos = jnp.where(is_ctx, j % cb, jl % lb)
        nblk = jnp.where(is_ctx, cb, lb)
        sid = jnp.where(is_ctx, j // cb, self.Bc + jl // lb)
        return pos, nblk, sid


def _gated_residual(acc, h_ref, m_ref, i_out, w_out):
    return h_ref[...] + w_out * m_ref[0, 3 * i_out + 2:3 * i_out + 3, :] * acc


def _mm_body(x_ref, w_ref, *rest, nk, resid):
    o_ref = rest[-1]
    part = jnp.dot(x_ref[...], w_ref[...], preferred_element_type=F32)

    def finish(acc):
        return acc if resid is None else _gated_residual(acc, rest[0], rest[1], *resid)

    if nk == 1:
        o_ref[...] = finish(part).astype(o_ref.dtype)
    else:
        k = pl.program_id(2)

        @pl.when(k == 0)
        def _():
            o_ref[...] = part

        @pl.when((k > 0) & (k < nk - 1))
        def _():
            o_ref[...] += part

        @pl.when(k == nk - 1)
        def _():
            o_ref[...] = finish(o_ref[...] + part)


def _stacked(w_index):
    return (None,) * len(w_index)


class Resid(NamedTuple):
    h: jax.Array
    mods: jax.Array
    mod_row: object
    i_out: int
    w_out: float


def mm(x, w, *, bm, bn, bk=None, w_index=(), resid=None, out_dtype=F32):
    M, K = x.shape
    N = w.shape[-1]
    bk = K if bk is None else bk
    nk = K // bk
    assert M % bm == 0 and N % bn == 0 and K % bk == 0
    assert nk == 1 or out_dtype == F32
    in_specs = [pl.BlockSpec((bm, bk), lambda i, j, k: (i, k)),
                pl.BlockSpec(_stacked(w_index) + (bk, bn), lambda i, j, k: w_index + (k, j))]
    args = [x, w]
    if resid is not None:
        in_specs += [pl.BlockSpec((bm, bn), lambda i, j, k: (i, j)),
                     pl.BlockSpec((1, N_MOD, bn), lambda i, j, k: (resid.mod_row(i, bm), 0, j))]
        args += [resid.h, resid.mods]
    return pl.pallas_call(
        functools.partial(_mm_body, nk=nk, resid=None if resid is None else (resid.i_out, resid.w_out)),
        out_shape=jax.ShapeDtypeStruct((M, N), out_dtype),
        grid=(M // bm, N // bn, nk),
        in_specs=in_specs,
        out_specs=pl.BlockSpec((bm, bn), lambda i, j, k: (i, j)),
        compiler_params=_params("parallel", "parallel", "arbitrary"),
        name="mm",
    )(*args)


def _mm_ws_body(x_ref, w_ref, *rest, resid):
    o_ref, wb_ref = rest[-2:]

    @pl.when(pl.program_id(1) == 0)
    def _():
        wb_ref[...] = w_ref[...].astype(BF16)

    acc = jnp.dot(x_ref[...], wb_ref[...], preferred_element_type=F32)
    if resid is not None:
        acc = _gated_residual(acc, rest[0], rest[1], *resid)
    o_ref[...] = acc.astype(o_ref.dtype)


def mm_ws(x, w, *, bm, bn, w_index=(), n_cols=None, resid=None, out_dtype=F32):
    M, K = x.shape
    N = w.shape[-1] if n_cols is None else n_cols
    assert M % bm == 0 and N % bn == 0
    in_specs = [pl.BlockSpec((bm, K), lambda j, i: (i, 0)),
                pl.BlockSpec(_stacked(w_index) + (K, bn), lambda j, i: w_index + (0, j))]
    args = [x, w]
    if resid is not None:
        in_specs += [pl.BlockSpec((bm, bn), lambda j, i: (i, j)),
                     pl.BlockSpec((1, N_MOD, bn), lambda j, i: (resid.mod_row(i, bm), 0, j))]
        args += [resid.h, resid.mods]
    return pl.pallas_call(
        functools.partial(_mm_ws_body, resid=None if resid is None else (resid.i_out, resid.w_out)),
        out_shape=jax.ShapeDtypeStruct((M, N), out_dtype),
        grid=(N // bn, M // bm),
        in_specs=in_specs,
        out_specs=pl.BlockSpec((bm, bn), lambda j, i: (i, j)),
        scratch_shapes=[pltpu.VMEM((K, bn), BF16)],
        compiler_params=_params("arbitrary", "arbitrary"),
        name="mm_ws",
    )(*args)


def _mm_small_body(x_ref, w_ref, o_ref):
    o_ref[...] = jnp.dot(x_ref[...], w_ref[...].astype(BF16), preferred_element_type=F32)


def mm_small(x, w, *, bn, w_index=()):
    M, K = x.shape
    N = w.shape[-1]
    return pl.pallas_call(
        _mm_small_body,
        out_shape=jax.ShapeDtypeStruct((M, N), F32),
        grid=(N // bn,),
        in_specs=[pl.BlockSpec((M, K), lambda j: (0, 0)),
                  pl.BlockSpec(_stacked(w_index) + (K, bn), lambda j: w_index + (0, j))],
        out_specs=pl.BlockSpec((M, bn), lambda j: (0, j)),
        compiler_params=_params("parallel"),
        name="mm_small",
    )(x, w)


def _ffn_up_body(x_ref, w1_ref, w3_ref, w2_ref, o_ref, w2b_ref, w1b_ref, w3b_ref):
    @pl.when(pl.program_id(1) == 0)
    def _():
        w1b_ref[...] = w1_ref[...].astype(BF16)
        w3b_ref[...] = w3_ref[...].astype(BF16)
        w2b_ref[...] = w2_ref[...].astype(BF16)

    x = x_ref[...]
    a = jnp.dot(x, w1b_ref[...], preferred_element_type=F32)
    b = jnp.dot(x, w3b_ref[...], preferred_element_type=F32)
    o_ref[...] = (a * jax.nn.sigmoid(a) * b).astype(o_ref.dtype)


def ffn_up(x, w1, w3, w2, *, w_index=(), bm=1024, bn=256):
    M, K = x.shape
    N = w1.shape[-1]
    D = w2.shape[-1]
    w_spec = pl.BlockSpec(_stacked(w_index) + (K, bn), lambda j, i: w_index + (0, j))
    return pl.pallas_call(
        _ffn_up_body,
        out_shape=(jax.ShapeDtypeStruct((M, N), BF16), jax.ShapeDtypeStruct((N, D), BF16)),
        grid=(N // bn, M // bm),
        in_specs=[pl.BlockSpec((bm, K), lambda j, i: (i, 0)), w_spec, w_spec,
                  pl.BlockSpec(_stacked(w_index) + (bn, D), lambda j, i: w_index + (j, 0))],
        out_specs=(pl.BlockSpec((bm, bn), lambda j, i: (i, j)), pl.BlockSpec((bn, D), lambda j, i: (j, 0))),
        scratch_shapes=[pltpu.VMEM((K, bn), BF16), pltpu.VMEM((K, bn), BF16)],
        compiler_params=_params("arbitrary", "arbitrary"),
        name="ffn_up",
    )(x, w1, w3, w2)


def _ada_body(*refs, i_in):
    h_ref, g_ref = refs[:2]
    x_ref = refs[-1]
    h = h_ref[...]
    x = h * lax.rsqrt(jnp.mean(h * h, axis=-1, keepdims=True) + NORM_EPS) * g_ref[...]
    if len(refs) == 4:
        mi_ref = refs[2]
        x = x * (1.0 + mi_ref[0, 3 * i_in + 1:3 * i_in + 2, :]) + mi_ref[0, 3 * i_in:3 * i_in + 1, :]
    x_ref[...] = x.astype(x_ref.dtype)


def ada_step(h, g, *, mod_row, m_in=None, i_in=0, out_dtype=BF16):
    T, D = h.shape
    R = ROW_BLOCK
    row_spec = pl.BlockSpec((R, D), lambda i: (i, 0))
    args, specs = [h, g.reshape(1, D)], [row_spec, pl.BlockSpec((1, D), lambda i: (0, 0))]
    if m_in is not None:
        args.append(m_in)
        specs.append(pl.BlockSpec((1, N_MOD, D), lambda i: (mod_row(i, R), 0, 0)))
    return pl.pallas_call(
        functools.partial(_ada_body, i_in=i_in),
        out_shape=jax.ShapeDtypeStruct((T, D), out_dtype), grid=(T // R,), in_specs=specs, out_specs=row_spec,
        compiler_params=_params("parallel"), name="ada_step",
    )(*args)


def _head_sums(x, hseg, n_pass=2):
    outs = []
    for c in range(x.shape[1] // LANES):
        rem = x[:, c * LANES:(c + 1) * LANES]
        acc = None
        for _ in range(n_pass):
            hi = rem.astype(BF16)
            part = jnp.dot(hi, hseg, preferred_element_type=F32)
            acc = part if acc is None else acc + part
            rem = rem - hi.astype(F32)
        outs.append(acc)
    return outs[0] if len(outs) == 1 else jnp.concatenate(outs, axis=1)


def _head_matrix():
    lane = jnp.arange(LANES)
    return (lane[:, None] // B_HEAD == lane[None, :] // B_HEAD).astype(BF16)


_TAIL_OFF = dict(wdf=0, wdb=B_DECAY_LORA, adf=2 * B_DECAY_LORA, adb=2 * B_DECAY_LORA + B_AAA_LORA,
                 gd=2 * B_DECAY_LORA + 2 * B_AAA_LORA)


def _prep_body(r_ref, k_ref, v_ref, t_ref, rp_ref, kp_ref, vp_ref, tp_ref, rn_ref, kn_ref, vn_ref, tn_ref,
               mu_r_ref, mu_k_ref, mu_v_ref, mu_t_ref, w2f_ref, w2b_ref, a2f_ref, a2b_ref, g2_ref, rowp_ref, hseg_ref,
               wf_ref, bf_ref, kdf_ref, wb_ref, bb_ref, kdb_ref, ro_ref, kko_ref, vo_ref, go_ref, bonus_ref,
               *, seqs):
    R = ROW_BLOCK
    pos, nblk, _ = seqs.locate(pl.program_id(0), R)
    has_prev = (pos != 0).astype(F32)
    has_next = (pos != nblk - 1).astype(F32)

    def shifted(cur_ref, p_ref, n_ref, mu_ref):
        x = cur_ref[0]
        row = lax.broadcasted_iota(jnp.int32, x.shape, 0)
        prev = jnp.where(row == 0, p_ref[0, SUBLANES - 1:SUBLANES, :] * has_prev, pltpu.roll(x, 1, axis=0))
        nxt = jnp.where(row == R - 1, n_ref[0, 0:1, :] * has_next, pltpu.roll(x, R - 1, axis=0))
        return x + mu_ref[...] * (0.5 * (prev + nxt) - x)

    rr = shifted(r_ref, rp_ref, rn_ref, mu_r_ref)
    kr = shifted(k_ref, kp_ref, kn_ref, mu_k_ref)
    vr = shifted(v_ref, vp_ref, vn_ref, mu_v_ref)
    tail = shifted(t_ref, tp_ref, tn_ref, mu_t_ref)
    t_tanh = jnp.tanh(tail).astype(BF16)
    t_lin = tail.astype(BF16)
    t_sig = jax.nn.sigmoid(tail).astype(BF16)

    def dot(a, w_ref):
        return jnp.dot(a, w_ref[...], preferred_element_type=F32)

    w0f, w0b, a0f, a0b, k_k, k_a, r_k = [rowp_ref[i:i + 1, :] for i in range(7)]
    hseg = hseg_ref[...]
    kk = kr * k_k
    kk = kk / jnp.maximum(jnp.sqrt(_head_sums(kk * kk, hseg)), 1e-12)
    for w0, a0, w2_ref, a2_ref, w_out, b_out, kd_out in ((w0f, a0f, w2f_ref, a2f_ref, wf_ref, bf_ref, kdf_ref),
                                                        (w0b, a0b, w2b_ref, a2b_ref, wb_ref, bb_ref, kdb_ref)):
        logw = -math.exp(-0.5) * jax.nn.sigmoid(w0 + dot(t_tanh, w2_ref))
        a = jax.nn.sigmoid(a0 + dot(t_lin, a2_ref))
        w_out[0] = jnp.exp(logw)
        b_out[0] = kk * a
        kd_out[0] = kr * (1.0 + (a - 1.0) * k_a)
    ro_ref[0] = rr
    kko_ref[0] = kk
    vo_ref[0] = vr
    go_ref[0] = dot(t_sig, g2_ref)
    bonus_ref[0] = _head_sums(rr * kr * r_k, hseg) * vr


def rwkv_prep(z, ztail, seqs, p):
    T = z.shape[0]
    R, CW = ROW_BLOCK, PREP_COLS
    NB, n8 = T // R, T // SUBLANES
    per = R // SUBLANES
    zv, z8 = z.reshape(NB, R, -1), z.reshape(n8, SUBLANES, -1)
    tv, t8 = ztail.reshape(NB, R, B_TAIL), ztail.reshape(n8, SUBLANES, B_TAIL)
    base = [(EV_A_IN + i * B_WIDTH) // CW for i in range(3)]

    def cur(b):
        return pl.BlockSpec((1, R, CW), lambda i, c: (i, 0, b + c))

    def prev(b):
        return pl.BlockSpec((1, SUBLANES, CW), lambda i, c: (jnp.maximum(i * per - 1, 0), 0, b + c))

    def nxt(b):
        return pl.BlockSpec((1, SUBLANES, CW), lambda i, c: (jnp.minimum((i + 1) * per, n8 - 1), 0, b + c))

    t_cur = pl.BlockSpec((1, R, B_TAIL), lambda i, c: (i, 0, 0))
    t_prev = pl.BlockSpec((1, SUBLANES, B_TAIL), lambda i, c: (jnp.maximum(i * per - 1, 0), 0, 0))
    t_next = pl.BlockSpec((1, SUBLANES, B_TAIL), lambda i, c: (jnp.minimum((i + 1) * per, n8 - 1), 0, 0))

    mu = p['mu']
    mu_main = mu[:3 * B_WIDTH].reshape(1, 3 * B_WIDTH)
    mu_t = mu[3 * B_WIDTH:].reshape(1, B_TAIL)

    def mu_spec(i):
        return pl.BlockSpec((1, CW), lambda r, c: (0, i * (B_WIDTH // CW) + c))

    def padded(w, off):
        return jnp.zeros((B_TAIL, B_WIDTH), BF16).at[off:off + w.shape[0]].set(w.astype(BF16))

    lora = [padded(p['w2f'], _TAIL_OFF['wdf']), padded(p['w2b'], _TAIL_OFF['wdb']),
            padded(p['a2f'], _TAIL_OFF['adf']), padded(p['a2b'], _TAIL_OFF['adb']), padded(p['g2'], _TAIL_OFF['gd'])]
    lora_spec = pl.BlockSpec((B_TAIL, CW), lambda i, c: (0, c))
    rowp = jnp.stack([p['w0f'], p['w0b'], p['a0f'], p['a0b'], p['k_k'], p['k_a'], p['r_k'].reshape(-1),
                      jnp.zeros((B_WIDTH,), F32)])
    out_spec = pl.BlockSpec((1, R, CW), lambda i, c: (i, 0, c))
    names = ('w_f', 'b_f', 'kd_f', 'w_b', 'b_b', 'kd_b', 'r', 'kk', 'v', 'g', 'bonus')
    outs = pl.pallas_call(
        functools.partial(_prep_body, seqs=seqs),
        out_shape=tuple(jax.ShapeDtypeStruct((NB, R, B_WIDTH), F32) for _ in names),
        grid=(NB, B_WIDTH // CW),
        in_specs=[cur(base[0]), cur(base[1]), cur(base[2]), t_cur,
                  prev(base[0]), prev(base[1]), prev(base[2]), t_prev,
                  nxt(base[0]), nxt(base[1]), nxt(base[2]), t_next,
                  mu_spec(0), mu_spec(1), mu_spec(2), pl.BlockSpec((1, B_TAIL), lambda i, c: (0, 0)),
                  lora_spec, lora_spec, lora_spec, lora_spec, lora_spec,
                  pl.BlockSpec((SUBLANES, CW), lambda i, c: (0, c)),
                  pl.BlockSpec((LANES, LANES), lambda i, c: (0, 0))],
        out_specs=tuple(out_spec for _ in names),
        compiler_params=_params("parallel", "parallel"),
        name="rwkv_prep",
    )(zv, zv, zv, tv, z8, z8, z8, t8, z8, z8, z8, t8, mu_main, mu_main, mu_main, mu_t, *lora, rowp, _head_matrix())
    return dict(zip(names, outs))


def _rwkv_body(w_ref, b_ref, kd_ref, r_ref, kk_ref, v_ref, g_ref, h_ref, t0_ref, o_ref, tf_ref,
               s_ref, vb_ref, xo_ref, *, seqs, NB, reverse):
    row_refs = (w_ref, b_ref, kd_ref, r_ref, kk_ref)
    TB = REC_TIME_BLOCK
    P, R, UW = s_ref.shape
    jb = pl.program_id(0)
    jb = (NB - 1 - jb) if reverse else jb
    pos, nblk, _ = seqs.locate(jb, TB)
    first = pos == ((nblk - 1) if reverse else 0)
    last = pos == (0 if reverse else (nblk - 1))

    @pl.when(first)
    def _():
        s_ref[...] = t0_ref[0]

    G = g_ref[...][None]
    Gb = G.astype(BF16)
    Hm = h_ref[...]

    Q = REC_SUB_UNITS
    subs = [slice(q * Q, (q + 1) * Q) for q in range(P // Q)]

    def seg_sum(x):
        return jnp.dot(x.reshape(Q * R, UW).astype(BF16), Hm, preferred_element_type=F32).reshape(Q, R, UW)

    def unit_rows(tile, j, qs):
        return jnp.stack([tile[j:j + 1, p * UW:(p + 1) * UW] for p in range(qs.start, qs.stop)])

    sub_iota = lax.broadcasted_iota(jnp.int32, (SUBLANES, UW), 0)

    def group(i, carry):
        gi = (TB // SUBLANES - 1 - i) if reverse else i
        t8 = pl.multiple_of(gi * SUBLANES, SUBLANES)
        tiles = [r[0, pl.ds(t8, SUBLANES), :] for r in row_refs]
        v_tile_b = v_ref[0, pl.ds(t8, SUBLANES), :].astype(BF16)
        for j in range(SUBLANES):
            for qs in subs:
                vb_ref[j, qs] = seg_sum(Gb * unit_rows(v_tile_b, j, qs))
        for jj in range(SUBLANES):
            j = SUBLANES - 1 - jj if reverse else jj
            for qs in subs:
                w, b, kd, r, kk = [unit_rows(tile, j, qs) for tile in tiles]
                S = s_ref[qs]
                u = seg_sum(S * kk)
                S = S * w - u * b + vb_ref[j, qs] * kd
                s_ref[qs] = S
                xo_ref[j, qs] = S * r
        for qs in subs:
            o_tiles = [jnp.zeros((SUBLANES, UW), F32) for _ in range(Q)]
            for j in range(SUBLANES):
                orow = jnp.sum(G * seg_sum(xo_ref[j, qs]), axis=1, keepdims=True)
                o_tiles = [jnp.where(sub_iota == j, orow[p], o_tiles[p]) for p in range(Q)]
            for p in range(Q):
                lo = (qs.start + p) * UW
                o_ref[0, pl.ds(t8, SUBLANES), lo:lo + UW] = o_tiles[p]
        return carry

    lax.fori_loop(0, TB // SUBLANES, group, 0)

    @pl.when(last)
    def _():
        tf_ref[0] = s_ref[...]


def rwkv7(rows, v, s0, seqs, *, reverse):
    T = v.shape[0] * v.shape[1]
    TB, UW, seg = REC_TIME_BLOCK, LANES, B_HEAD
    NB = T // TB
    hpu = UW // seg
    P = B_HEADS // hpu
    n_seq = s0.shape[0]
    t0 = s0.reshape(n_seq, P, hpu, seg, seg).transpose(0, 1, 3, 2, 4).reshape(n_seq, P, seg, UW)
    lane = jnp.arange(UW)
    g = (lane[None, :] % seg == jnp.arange(seg)[:, None]).astype(F32)

    def blk(j):
        return (NB - 1 - j) if reverse else j

    row_spec = pl.BlockSpec((1, TB, B_WIDTH), lambda j: (blk(j), 0, 0))
    st_spec = pl.BlockSpec((1, P, seg, UW), lambda j: (seqs.locate(blk(j), TB)[2], 0, 0, 0))
    o, s = pl.pallas_call(
        functools.partial(_rwkv_body, seqs=seqs, NB=NB, reverse=reverse),
        out_shape=(jax.ShapeDtypeStruct((NB, TB, B_WIDTH), F32), jax.ShapeDtypeStruct(t0.shape, F32)),
        grid=(NB,),
        in_specs=[row_spec] * 6 + [pl.BlockSpec((seg, UW), lambda j: (0, 0)),
                                   pl.BlockSpec((UW, UW), lambda j: (0, 0)), st_spec],
        out_specs=(row_spec, st_spec),
        scratch_shapes=[pltpu.VMEM((P, seg, UW), F32), pltpu.VMEM((SUBLANES, P, seg, UW), F32),
                        pltpu.VMEM((SUBLANES, P, seg, UW), F32)],
        compiler_params=_params("arbitrary"),
        name="rwkv7",
    )(*[x.reshape(NB, TB, B_WIDTH) for x in rows], v.reshape(NB, TB, B_WIDTH), g, _head_matrix(), t0)
    s = s.reshape(n_seq, P, seg, hpu, seg).transpose(0, 1, 3, 2, 4).reshape(s0.shape)
    return o.reshape(T, B_WIDTH), s


HG_STEP = 8


def _hgrn_body(q_ref, f_ref, v_ref, lb_ref, s0_ref, o_ref, sf_ref, s_ref, *, seqs, NB, Hh, reverse):
    C = HG_STEP
    TB = ROW_BLOCK
    jb = pl.program_id(1)
    jb = (NB - 1 - jb) if reverse else jb
    pos, nblk, _ = seqs.locate(jb, TB)
    first = pos == ((nblk - 1) if reverse else 0)
    last = pos == (0 if reverse else (nblk - 1))

    @pl.when(first)
    def _():
        s_ref[...] = s0_ref[0]

    row = lax.broadcasted_iota(jnp.int32, (C, LANES), 0)
    def shift(x, d):
        if d == 0:
            return x
        return pltpu.roll(x, (C - d) if reverse else d, axis=0)

    def has_source(d):
        return (row <= C - 1 - d) if reverse else (row >= d)

    def block(i, carry):
        ci = (TB // C - 1 - i) if reverse else i
        t0 = pl.multiple_of(ci * C, C)
        for h in range(Hh):
            sl = slice(h * LANES, (h + 1) * LANES)
            hq = q_ref[0, pl.ds(t0, C), sl]
            q = hq * jax.nn.sigmoid(hq) * A_DK ** -0.5
            lb = lb_ref[:, sl]
            f = lb + (1.0 - lb) * jax.nn.sigmoid(f_ref[0, pl.ds(t0, C), sl])
            v = v_ref[0, pl.ds(t0, C), sl]
            k = 1.0 - f
            b = jnp.log(f)
            s = 1
            while s < C:
                b = b + jnp.where(has_source(s), shift(b, s), 0.0)
                s *= 2
            st = s_ref[h]
            stb = st.astype(BF16)
            o = lax.dot_general((q * jnp.exp(b)).astype(BF16), stb, (((1,), (1,)), ((), ())),
                                preferred_element_type=F32)
            for d in range(C):
                arg = jnp.where(has_source(d), b - shift(b, d), -1e30)
                a = jnp.sum(q * shift(k, d) * jnp.exp(arg), axis=1, keepdims=True)
                o = o + a * shift(v, d)
            o_ref[0, pl.ds(t0, C), sl] = o
            end = 0 if reverse else C - 1
            bl = b[end:end + 1, :]
            kt = k * jnp.exp(bl - b)
            s_ref[h] = st * jnp.exp(bl) + lax.dot_general(v.astype(BF16), kt.astype(BF16), (((0,), (0,)), ((), ())),
                                                         preferred_element_type=F32)
        return carry

    lax.fori_loop(0, TB // C, block, 0)

    @pl.when(last)
    def _():
        sf_ref[0] = s_ref[...]


def hgrn2(zv, cols, lb, s0, seqs, *, reverse, heads_per_step=16):
    NB, TB, _ = zv.shape
    W = A_WIDTH
    H = W // LANES
    Hh = heads_per_step
    bw = Hh * LANES
    assert TB == ROW_BLOCK and H % Hh == 0 and all(c % bw == 0 for c in cols)

    def blk(j):
        return (NB - 1 - j) if reverse else j

    def in_spec(col):
        return pl.BlockSpec((1, TB, bw), lambda u, j: (blk(j), 0, col // bw + u))

    st_spec = pl.BlockSpec((1, Hh, LANES, LANES), lambda u, j: (seqs.locate(blk(j), TB)[2], u, 0, 0))
    st0 = jnp.swapaxes(s0, -1, -2)
    o, s = pl.pallas_call(
        functools.partial(_hgrn_body, seqs=seqs, NB=NB, Hh=Hh, reverse=reverse),
        out_shape=(jax.ShapeDtypeStruct((NB, TB, W), F32), jax.ShapeDtypeStruct(st0.shape, F32)),
        grid=(H // Hh, NB),
        in_specs=[in_spec(cols[0]), in_spec(cols[1]), in_spec(cols[2]),
                  pl.BlockSpec((1, bw), lambda u, j: (0, u)), st_spec],
        out_specs=(pl.BlockSpec((1, TB, bw), lambda u, j: (blk(j), 0, u)), st_spec),
        scratch_shapes=[pltpu.VMEM((Hh, LANES, LANES), F32)],
        compiler_params=_params("parallel", "arbitrary"),
        name="hgrn2",
    )(zv, zv, zv, lb, st0)
    return o.reshape(NB * TB, W), jnp.swapaxes(s, -1, -2)


def _even_post_body(oaf_ref, oab_ref, hg_ref, obf_ref, obb_ref, bonus_ref, g_ref, rowp_ref, hseg_ref, o_ref):
    hg_g, ln_g, ln_b = [rowp_ref[i:i + 1, :] for i in range(3)]
    oa = oaf_ref[0] + oab_ref[0]
    hg = hg_ref[0]
    oa = oa * lax.rsqrt(jnp.mean(oa * oa, axis=-1, keepdims=True) + NORM_EPS) * hg_g * (hg * jax.nn.sigmoid(hg))
    o_ref[0, :, :A_WIDTH] = oa.astype(o_ref.dtype)
    hseg = hseg_ref[...]
    ob = obf_ref[0] + obb_ref[0]
    cen = ob - _head_sums(ob, hseg) * (1.0 / B_HEAD)
    var = _head_sums(cen * cen, hseg) * (1.0 / B_HEAD)
    ob = (cen * lax.rsqrt(var + B_GN_EPS) * ln_g + ln_b + bonus_ref[0]) * g_ref[0]
    o_ref[0, :, A_WIDTH:] = ob.astype(o_ref.dtype)


def even_post(oa_f, oa_b, zv, ob_f, ob_b, bonus, g, p):
    NB, R, _ = zv.shape
    T = NB * R
    W = A_WIDTH
    assert W == B_WIDTH and (4 * A_WIDTH) % W == 0
    spec = pl.BlockSpec((1, R, W), lambda i: (i, 0, 0))
    rowp = jnp.stack([p['hg_g'], p['ln_g'], p['ln_b']] + [jnp.zeros((W,), F32)] * 5)
    v3 = lambda x: x.reshape(NB, R, W)
    out = pl.pallas_call(
        _even_post_body,
        out_shape=jax.ShapeDtypeStruct((NB, R, 2 * W), BF16),
        grid=(NB,),
        in_specs=[spec, spec, pl.BlockSpec((1, R, W), lambda i: (i, 0, 4 * A_WIDTH // W)), spec, spec, spec, spec,
                  pl.BlockSpec((SUBLANES, W), lambda i: (0, 0)), pl.BlockSpec((LANES, LANES), lambda i: (0, 0))],
        out_specs=pl.BlockSpec((1, R, 2 * W), lambda i: (i, 0, 0)),
        compiler_params=_params("parallel"),
        name="even_post",
    )(v3(oa_f), v3(oa_b), zv, v3(ob_f), v3(ob_b), bonus, g, rowp, _head_matrix())
    return out.reshape(T, 2 * W)


def even_mixer(z, ztail, seqs, st, p):
    T = z.shape[0]
    zv = z.reshape(T // ROW_BLOCK, ROW_BLOCK, -1)
    oa_f, sf = hgrn2(zv, (0, A_WIDTH, 3 * A_WIDTH), p['lb_f'][None], st['hf'], seqs, reverse=False)
    oa_b, sb = hgrn2(zv, (0, 2 * A_WIDTH, 3 * A_WIDTH), p['lb_b'][None], st['hb'], seqs, reverse=True)
    q = rwkv_prep(z, ztail, seqs, p)
    ob_f, rf = rwkv7((q['w_f'], q['b_f'], q['kd_f'], q['r'], q['kk']), q['v'], st['rf'], seqs, reverse=False)
    ob_b, rb = rwkv7((q['w_b'], q['b_b'], q['kd_b'], q['r'], q['kk']), q['v'], st['rb'], seqs, reverse=True)
    return even_post(oa_f, oa_b, zv, ob_f, ob_b, q['bonus'], q['g'], p), (sf, sb, rf, rb)


def _swap_halves(x):
    lane = lax.broadcasted_iota(jnp.int32, x.shape, 1)
    quarter = ROPE_AXIS_DIM // 2
    return jnp.where(lane % ROPE_AXIS_DIM < quarter, pltpu.roll(x, LANES - quarter, axis=1),
                     pltpu.roll(x, quarter, axis=1))


def _attn_body(*refs, has_ctx):
    if has_ctx:
        (q_ref, kn_ref, vn_ref, ck_ref, cv_ref, qc_ref, qs_ref, kc_ref, ks_ref, qg_ref, kg_ref,
         o_ref, k_sc, v_sc) = refs
    else:
        q_ref, kn_ref, vn_ref, qg_ref, kg_ref, o_ref, ko_ref, vo_ref, k_sc, v_sc = refs

    def norm(x, g_ref):
        return x * lax.rsqrt(jnp.mean(x * x, axis=-1, keepdims=True) + NORM_EPS) * g_ref[...]

    @pl.when(pl.program_id(2) == 0)
    def _():
        k = norm(kn_ref[0], kg_ref)
        v = vn_ref[0]
        if has_ctx:
            n_past = ck_ref.shape[1]
            k = k * kc_ref[...] + _swap_halves(k) * ks_ref[...]
            k_sc[:n_past] = ck_ref[0].astype(BF16)
            v_sc[:n_past] = cv_ref[0].astype(BF16)
            k_sc[n_past:] = k.astype(BF16)
            v_sc[n_past:] = v.astype(BF16)
        else:
            ko_ref[0] = k
            vo_ref[0] = v
            k_sc[...] = k.astype(BF16)
            v_sc[...] = v.astype(BF16)

    k = k_sc[...]
    v = v_sc[...]
    scale = C_HEAD_DIM ** -0.5
    for g in range(C_GROUP):
        sl = slice(g * C_HEAD_DIM, (g + 1) * C_HEAD_DIM)
        q = norm(q_ref[0, :, sl], qg_ref)
        if has_ctx:
            q = q * qc_ref[...] + _swap_halves(q) * qs_ref[...]
        s = lax.dot_general(q.astype(BF16), k, (((1,), (1,)), ((), ())), preferred_element_type=F32) * scale
        m = jnp.max(s, axis=-1, keepdims=True)
        e = jnp.exp(s - m)
        p = e / jnp.sum(e, axis=-1, keepdims=True)
        o_ref[0, :, sl] = jnp.dot(p.astype(BF16), v, preferred_element_type=F32).astype(o_ref.dtype)


def _rope_tables(L):
    n_rows = L // GRID_W
    row = jnp.repeat(jnp.arange(n_rows), GRID_W).astype(F32)
    col = jnp.tile(jnp.arange(GRID_W), n_rows).astype(F32)
    half = ROPE_AXIS_DIM // 2
    inv = ROPE_THETA ** (-jnp.arange(half, dtype=F32) / half)
    ar, ac = row[:, None] * inv, col[:, None] * inv
    cos_t = jnp.concatenate([jnp.cos(ar), jnp.cos(ar), jnp.cos(ac), jnp.cos(ac)], axis=1)
    sin_t = jnp.concatenate([-jnp.sin(ar), jnp.sin(ar), -jnp.sin(ac), jnp.sin(ac)], axis=1)
    return cos_t, sin_t


def attention(z, row0, B, L, qn_g, kn_g, ctx_k=None, ctx_v=None):
    T = z.shape[0]
    assert row0 % L == 0 and L % Q_BLOCK == 0
    has_ctx = ctx_k is not None
    gw = C_GROUP * C_HEAD_DIM
    zq = z.reshape(T // Q_BLOCK, Q_BLOCK, -1)
    zl = z.reshape(T // L, L, -1)
    qb0, lb0 = row0 // Q_BLOCK, row0 // L
    nq = L // Q_BLOCK
    kcol, vcol = C_WIDTH // C_HEAD_DIM, (C_WIDTH + C_KV_WIDTH) // C_HEAD_DIM
    g_spec = pl.BlockSpec((1, C_HEAD_DIM), lambda b, h, i: (0, 0))
    in_specs = [pl.BlockSpec((1, Q_BLOCK, gw), lambda b, h, i: (qb0 + b * nq + i, 0, h)),
                pl.BlockSpec((1, L, C_HEAD_DIM), lambda b, h, i: (lb0 + b, 0, kcol + h)),
                pl.BlockSpec((1, L, C_HEAD_DIM), lambda b, h, i: (lb0 + b, 0, vcol + h))]
    args = [zq, zl, zl]
    o_spec = pl.BlockSpec((1, Q_BLOCK, gw), lambda b, h, i: (b, i, h))
    o_shape = jax.ShapeDtypeStruct((B, L, C_WIDTH), BF16)
    if has_ctx:
        n_past = ctx_k.shape[1]
        c_spec = pl.BlockSpec((1, n_past, C_HEAD_DIM), lambda b, h, i: (b, 0, h))
        cos_t, sin_t = _rope_tables(L)
        q_tab = pl.BlockSpec((Q_BLOCK, C_HEAD_DIM), lambda b, h, i: (i, 0))
        k_tab = pl.BlockSpec((L, C_HEAD_DIM), lambda b, h, i: (0, 0))
        in_specs += [c_spec, c_spec, q_tab, q_tab, k_tab, k_tab]
        args += [ctx_k.reshape(B, n_past, C_KV_WIDTH), ctx_v.reshape(B, n_past, C_KV_WIDTH), cos_t, sin_t, cos_t, sin_t]
        out_shape, out_specs = o_shape, o_spec
        Lk = n_past + L
    else:
        kv_spec = pl.BlockSpec((1, L, C_HEAD_DIM), lambda b, h, i: (b, 0, h))
        kv_shape = jax.ShapeDtypeStruct((B, L, C_KV_WIDTH), F32)
        out_shape, out_specs = (o_shape, kv_shape, kv_shape), (o_spec, kv_spec, kv_spec)
        Lk = L
    in_specs += [g_spec, g_spec]
    args += [qn_g.reshape(1, C_HEAD_DIM), kn_g.reshape(1, C_HEAD_DIM)]
    res = pl.pallas_call(
        functools.partial(_attn_body, has_ctx=has_ctx),
        out_shape=out_shape, grid=(B, C_KV_HEADS, nq), in_specs=in_specs, out_specs=out_specs,
        scratch_shapes=[pltpu.VMEM((Lk, C_HEAD_DIM), BF16), pltpu.VMEM((Lk, C_HEAD_DIM), BF16)],
        compiler_params=_params("parallel", "parallel", "arbitrary"),
        name="attn",
    )(*args)
    if has_ctx:
        return res, None, None
    o, k, v = res
    return o, k.reshape(B, L, C_KV_HEADS, C_HEAD_DIM), v.reshape(B, L, C_KV_HEADS, C_HEAD_DIM)


def _dot3(ah, al, u):
    uh = u.astype(BF16)
    ul = (u - uh.astype(F32)).astype(BF16)
    return (jnp.dot(ah, uh, preferred_element_type=F32) + jnp.dot(ah, ul, preferred_element_type=F32)
            + jnp.dot(al, uh, preferred_element_type=F32))


def _conv3(x, w_ref, b_ref):
    L = x.shape[0]
    row = lax.broadcasted_iota(jnp.int32, x.shape, 0)
    prev = jnp.where(row == 0, 0.0, pltpu.roll(x, 1, axis=0))
    nxt = jnp.where(row == L - 1, 0.0, pltpu.roll(x, L - 1, axis=0))
    return prev * w_ref[0:1, :] + x * w_ref[1:2, :] + nxt * w_ref[2:3, :] + b_ref[...]


def _dft_fwd_body(*refs, conv):
    if conv:
        ah_ref, al_ref, u_ref, cw_ref, cb_ref, o_ref = refs
        u = _conv3(u_ref[0], cw_ref, cb_ref)
    else:
        ah_ref, al_ref, u_ref, o_ref = refs
        u = u_ref[0]
    o_ref[0] = _dot3(ah_ref[...], al_ref[...], u)


def _dft_inv_body(*refs, conv_u):
    if conv_u:
        ah_ref, al_ref, uf_ref, sc_ref, u_ref, cwu_ref, cbu_ref, x_ref, cwx_ref, cbx_ref, bias_ref, o_ref = refs
        u = _conv3(u_ref[0], cwu_ref, cbu_ref)
    else:
        ah_ref, al_ref, uf_ref, sc_ref, u_ref, x_ref, cwx_ref, cbx_ref, bias_ref, o_ref = refs
        u = u_ref[0]
    y = _dot3(ah_ref[...], al_ref[...], uf_ref[0] * sc_ref[...])
    gate = _conv3(x_ref[0], cwx_ref, cbx_ref)
    o_ref[0] = ((y + u * bias_ref[...]) * gate).astype(o_ref.dtype)


def dft_mm(a, u, *, bn=512):
    ah, al = a
    M, K = ah.shape
    B, _, C = u.shape
    a_spec = pl.BlockSpec((M, K), lambda b, j: (0, 0))
    return pl.pallas_call(
        functools.partial(_dft_fwd_body, conv=False),
        out_shape=jax.ShapeDtypeStruct((B, M, C), F32),
        grid=(B, C // bn),
        in_specs=[a_spec, a_spec, pl.BlockSpec((1, K, bn), lambda b, j: (b, 0, j))],
        out_specs=pl.BlockSpec((1, M, bn), lambda b, j: (b, 0, j)),
        compiler_params=_params("parallel", "parallel"),
        name="dft_mm",
    )(ah, al, u)


def hyena(z, row0, B, L, p):
    T = z.shape[0]
    N = 2 * L
    zl = z.reshape(T // L, L, -1)
    lb0 = row0 // L
    filt = _hyena_filters(L, p['f_w1'], p['f_b1'], p['f_w2'], p['f_b2'], p['f_w3'], p['f_b3'], p['f_w4'],
                          p['sin_freq'])
    fmat = _dft_matrix(L)
    fwd, inv = _split_bf16(fmat), _split_bf16(fmat.T)
    h_all = filt.reshape(L, HY_ORDER * HY_W)
    kf = dft_mm(fwd, jnp.concatenate([h_all[:1], 2.0 * h_all[1:]], axis=0)[None])[0]
    r = jnp.arange(N)
    herm = jnp.where((r == 0) | (r == L), 1.0, 2.0).astype(F32) / N
    scale = (kf[jnp.where(r <= L, r, r - L)] * herm[:, None]).reshape(N, HY_ORDER, HY_W)
    conv_w, conv_b = p['conv_w'], p['conv_b'].reshape(1, -1)
    bias = p['hy_bias']

    def z_spec(piece, bn):
        c0 = (C_IN + piece * HY_W) // bn
        return pl.BlockSpec((1, L, bn), lambda b, j: (lb0 + b, 0, c0 + j))

    def cw_specs(piece, bn):
        c0 = piece * HY_W // bn
        return [pl.BlockSpec((3, bn), lambda b, j: (0, c0 + j)), pl.BlockSpec((1, bn), lambda b, j: (0, c0 + j))]

    def forward(u, conv, bn=512):
        a_spec = pl.BlockSpec((N, L), lambda b, j: (0, 0))
        if conv:
            in_specs, args = [z_spec(0, bn)] + cw_specs(0, bn), [zl, conv_w, conv_b]
        else:
            in_specs, args = [pl.BlockSpec((1, L, bn), lambda b, j: (b, 0, j))], [u]
        return pl.pallas_call(
            functools.partial(_dft_fwd_body, conv=conv),
            out_shape=jax.ShapeDtypeStruct((B, N, HY_W), F32), grid=(B, HY_W // bn),
            in_specs=[a_spec, a_spec] + in_specs, out_specs=pl.BlockSpec((1, N, bn), lambda b, j: (b, 0, j)),
            compiler_params=_params("parallel", "parallel"), name="hyena_fwd",
        )(*fwd, *args)

    def inverse(uf, order, u, gate_piece, out_dtype, bn=256):
        a_spec = pl.BlockSpec((L, N), lambda b, j: (0, 0))
        blk = pl.BlockSpec((1, L, bn), lambda b, j: (b, 0, j))
        conv_u = u is None
        in_specs = [a_spec, a_spec, pl.BlockSpec((1, N, bn), lambda b, j: (b, 0, j)),
                    pl.BlockSpec((N, bn), lambda b, j: (0, j))]
        args = [*inv, uf, scale[:, order]]
        if conv_u:
            in_specs += [z_spec(0, bn)] + cw_specs(0, bn)
            args += [zl, conv_w, conv_b]
        else:
            in_specs.append(blk)
            args.append(u)
        in_specs += [z_spec(gate_piece, bn)] + cw_specs(gate_piece, bn) + [pl.BlockSpec((1, bn), lambda b, j: (0, j))]
        args += [zl, conv_w, conv_b, bias[order][None]]
        return pl.pallas_call(
            functools.partial(_dft_inv_body, conv_u=conv_u),
            out_shape=jax.ShapeDtypeStruct((B, L, HY_W), out_dtype), grid=(B, HY_W // bn),
            in_specs=in_specs, out_specs=blk,
            compiler_params=_params("parallel", "parallel"), name="hyena_inv",
        )(*args)

    z1 = inverse(forward(None, True), 0, None, 1, F32)
    return inverse(forward(z1, False), 1, z1, 2, BF16)


def _split_bf16(a):
    hi = a.astype(BF16)
    return hi, (a - hi.astype(F32)).astype(BF16)


def _dft_matrix(L):
    N = 2 * L
    r = jnp.arange(N, dtype=jnp.int32)[:, None]
    s = jnp.arange(L, dtype=jnp.int32)[None, :]
    is_cos = r <= L
    f = jnp.where(is_cos, r, r - L)
    ang = (2.0 * math.pi / N) * ((f * s) % N).astype(F32)
    return jnp.where(is_cos, jnp.cos(ang), jnp.sin(ang))


def _hyena_filters(L, w1, b1, w2, b2, w3, b3, w4, freq):
    t = jnp.linspace(0.0, 1.0, L, dtype=F32)[:, None]
    w_pos = 2.0 * math.pi * jnp.arange(L, dtype=F32)[:, None] / L
    bands = jnp.linspace(1e-4, HY_BANDS - 1, HY_BANDS, dtype=F32)[None]
    zpos = jnp.concatenate([t, jnp.cos(bands * w_pos), -jnp.sin(bands * w_pos)], axis=-1)
    f = freq.astype(F32)
    hdn = jnp.sin(f * (zpos @ w1 + b1))
    hdn = jnp.sin(f * (hdn @ w2 + b2))
    hdn = jnp.sin(f * (hdn @ w3 + b3))
    h = mm(hdn.astype(BF16), w4.astype(BF16), bm=L, bn=1024)
    deltas = jnp.abs(jnp.linspace(HY_MIN_DECAY, HY_MAX_DECAY, HY_ORDER * HY_W, dtype=F32))
    h = h * jnp.exp(-t * deltas)
    return h.reshape(L, HY_ORDER, HY_W)


def _odd_core(z, row0, B, L, ctx_k, ctx_v, p):
    o_c, k, v = attention(z, row0, B, L, p['qn_g'], p['kn_g'], ctx_k, ctx_v)
    return jnp.concatenate([o_c, hyena(z, row0, B, L, p)], axis=-1), k, v


def kernel(x_prompt, x_sample, c, state_hgrn_fwd, state_hgrn_bwd, state_rwkv_fwd, state_rwkv_bwd, cache_k, cache_v, c_ctx, ada_w, ada_b, norm_g, ffn_w1, ffn_w3, ffn_w2, final_norm_g, ev_w_in, ev_w_out, hg_lb_fwd, hg_lb_bwd, hg_norm_g, rw_mu, rw_w0_f, rw_w2_f, rw_w0_b, rw_w2_b, rw_a0_f, rw_a2_f, rw_a0_b, rw_a2_b, rw_g2, rw_kk, rw_ka, rw_rk, rw_ln_g, rw_ln_b, od_w_in, od_w_out, at_qn_g, at_kn_g, hy_conv_w, hy_conv_b, hy_f_w1, hy_f_b1, hy_f_w2, hy_f_b2, hy_f_w3, hy_f_b3, hy_f_w4, hy_sin_freq, hy_bias):
    Bc, Lc, _ = x_prompt.shape
    Bl, Ll, _ = x_sample.shape
    seqs = Seqs(Bc, Lc, Bl, Ll)
    n_ctx_tok = Bc * Lc
    n_tok = seqs.n_tok
    assert Lc % ROW_BLOCK == 0 and Ll % ROW_BLOCK == 0 and n_ctx_tok % Ll == 0

    def mod_row(i, rows):
        return jnp.maximum(i * rows // Ll - (n_ctx_tok // Ll - 1), 0)

    h = jnp.concatenate([x_prompt.reshape(n_ctx_tok, D_MODEL), x_sample.reshape(Bl * Ll, D_MODEL)], axis=0)
    cvec = jnp.concatenate([c_ctx[None], c], axis=0)
    cvec = jnp.pad(jax.nn.silu(cvec), ((0, SUBLANES - (1 + Bl)), (0, 0))).astype(BF16)
    mods = [(mm_small(cvec, ada_w, bn=1024, w_index=(l,)) + ada_b[l])[:1 + Bl].reshape(1 + Bl, N_MOD, D_MODEL)
            for l in range(DEPTH)]

    lb_fwd_all = jnp.cumsum(jax.nn.softmax(hg_lb_fwd, axis=0)[1:], axis=0)
    lb_bwd_all = jnp.cumsum(jax.nn.softmax(hg_lb_bwd, axis=0)[1:], axis=0)

    def with_ctx_zeros(s):
        return jnp.concatenate([jnp.zeros((Bc,) + s.shape[1:], F32), s], axis=0)

    def ffn(x, l, s, resid):
        mid, w2 = ffn_up(x, ffn_w1, ffn_w3, ffn_w2, w_index=(l, s))
        return mm(mid, w2, bm=1024, bn=512, bk=D_FF // 2, resid=resid)

    new_hf, new_hb, new_rf, new_rb, new_k, new_v = [], [], [], [], [], []
    x = ada_step(h, norm_g[0, 0], mod_row=mod_row, m_in=mods[0], i_in=0)
    for l in range(DEPTH):
        m = mods[l]
        h = ffn(x, l, 0, Resid(h, m, mod_row, 0, 0.5))
        xn = ada_step(h, norm_g[l, 1], mod_row=mod_row, m_in=m, i_in=1)
        j = l // 2
        if l % 2 == 0:
            p = dict(lb_f=lb_fwd_all[j], lb_b=lb_bwd_all[j], hg_g=hg_norm_g[j], mu=rw_mu[j],
                     w0f=rw_w0_f[j], w2f=rw_w2_f[j], w0b=rw_w0_b[j], w2b=rw_w2_b[j],
                     a0f=rw_a0_f[j], a2f=rw_a2_f[j], a0b=rw_a0_b[j], a2b=rw_a2_b[j], g2=rw_g2[j],
                     k_k=rw_kk[j], k_a=rw_ka[j], r_k=rw_rk[j], ln_g=rw_ln_g[j], ln_b=rw_ln_b[j])
            st = dict(hf=with_ctx_zeros(state_hgrn_fwd[:, j]), hb=with_ctx_zeros(state_hgrn_bwd[:, j]),
                      rf=with_ctx_zeros(state_rwkv_fwd[:, j]), rb=with_ctx_zeros(state_rwkv_bwd[:, j]))
            z = mm_ws(xn, ev_w_in, bm=1024, bn=512, w_index=(j,), n_cols=EV_MAIN)
            ztail = mm_ws(xn, ev_w_in[j, :, EV_MAIN:], bm=1024, bn=B_TAIL)
            o, (s_hf, s_hb, s_rf, s_rb) = even_mixer(z, ztail, seqs, st, p)
            new_hf.append(s_hf[:Bc])
            new_hb.append(s_hb[:Bc])
            new_rf.append(s_rf[:Bc])
            new_rb.append(s_rb[:Bc])
            w_out = ev_w_out
        else:
            p = dict(qn_g=at_qn_g[j], kn_g=at_kn_g[j], conv_w=hy_conv_w[j], conv_b=hy_conv_b[j],
                     f_w1=hy_f_w1[j], f_b1=hy_f_b1[j], f_w2=hy_f_w2[j], f_b2=hy_f_b2[j], f_w3=hy_f_w3[j],
                     f_b3=hy_f_b3[j], f_w4=hy_f_w4[j], sin_freq=hy_sin_freq[j], hy_bias=hy_bias[j])
            z = mm_ws(xn, od_w_in, bm=1024, bn=512, w_index=(j,))
            o_ctx, k_ctx, v_ctx = _odd_core(z, 0, Bc, Lc, None, None, p)
            o_lat, _, _ = _odd_core(z, n_ctx_tok, Bl, Ll, cache_k[:, j], cache_v[:, j], p)
            new_k.append(k_ctx)
            new_v.append(v_ctx)
            o = jnp.concatenate([o_ctx.reshape(n_ctx_tok, -1), o_lat.reshape(Bl * Ll, -1)], axis=0)
            w_out = od_w_out
        h = mm_ws(o, w_out, bm=1024, bn=512, w_index=(j,), resid=Resid(h, m, mod_row, 1, 1.0))
        x = ada_step(h, norm_g[l, 2], mod_row=mod_row, m_in=m, i_in=2)
        h = ffn(x, l, 1, Resid(h, m, mod_row, 2, 0.5))
        if l + 1 < DEPTH:
            x = ada_step(h, norm_g[l + 1, 0], mod_row=mod_row, m_in=mods[l + 1], i_in=0)
    y_out = ada_step(h, final_norm_g, mod_row=mod_row, out_dtype=F32)
    return (y_out[:n_ctx_tok].reshape(Bc, Lc, D_MODEL), y_out[n_ctx_tok:].reshape(Bl, Ll, D_MODEL),
            jnp.stack(new_hf, axis=1), jnp.stack(new_hb, axis=1), jnp.stack(new_rf, axis=1),
            jnp.stack(new_rb, axis=1), jnp.stack(new_k, axis=1), jnp.stack(new_v, axis=1))
```

```python
import functools
import math
from typing import NamedTuple

import jax
import jax.numpy as jnp
from jax import lax
from jax.experimental import pallas as pl
from jax.experimental.pallas import tpu as pltpu

F32 = jnp.float32
BF16 = jnp.bfloat16

D_MODEL = 4096
DEPTH = 2
GRID_W = 64
N_MOD = 9
D_FF = 11008
NORM_EPS = 1e-6
A_WIDTH = D_MODEL // 2
A_HEADS = 16
A_DK = A_WIDTH // A_HEADS
B_WIDTH = D_MODEL // 2
B_HEAD = 64
B_HEADS = B_WIDTH // B_HEAD
B_DECAY_LORA = max(32, int(round(1.8 * B_WIDTH ** 0.5 / 32)) * 32)
B_AAA_LORA = max(32, int(round(1.8 * B_WIDTH ** 0.5 / 32)) * 32)
B_GATE_LORA = max(32, int(round(0.6 * B_WIDTH ** 0.8 / 32)) * 32)
B_TAIL = 2 * B_DECAY_LORA + 2 * B_AAA_LORA + B_GATE_LORA
B_GN_EPS = 64e-5
EV_A_IN = 5 * A_WIDTH
EV_B_IN = 3 * B_WIDTH + B_TAIL
EV_MAIN = EV_A_IN + 3 * B_WIDTH
C_HEAD_DIM = 128
C_HEADS = 16
C_KV_HEADS = 4
C_GROUP = C_HEADS // C_KV_HEADS
C_WIDTH = C_HEADS * C_HEAD_DIM
C_KV_WIDTH = C_KV_HEADS * C_HEAD_DIM
C_IN = C_WIDTH + 2 * C_KV_WIDTH
ROPE_AXIS_DIM = C_HEAD_DIM // 2
ROPE_THETA = 10000.0
Q_BLOCK = 128
HY_W = D_MODEL // 2
HY_ORDER = 2
HY_BANDS = 16
HY_TARGET = 1e-2
HY_MIN_DECAY = math.log(HY_TARGET) / 1.5
HY_MAX_DECAY = math.log(HY_TARGET) / 0.3

V7X_VMEM_LIMIT_BYTES = 56 * 1024 * 1024
LANES = 128
SUBLANES = 8
ROW_BLOCK = 256
REC_TIME_BLOCK = 128
REC_SUB_UNITS = 4
PREP_COLS = 512
HYENA_FWD_BLOCK = 1024 * 512
HYENA_INV_BLOCK = 1024 * 256


def _params(*sem):
    return pltpu.CompilerParams(dimension_semantics=sem, vmem_limit_bytes=V7X_VMEM_LIMIT_BYTES)


class Seqs(NamedTuple):
    Bc: int
    Lc: int
    Bl: int
    Ll: int

    @property
    def n_tok(self):
        return self.Bc * self.Lc + self.Bl * self.Ll

    def locate(self, j, blk):
        n_ctx = self.Bc * self.Lc // blk
        cb, lb = self.Lc // blk, self.Ll // blk
        is_ctx = j < n_ctx
        jl = jnp.maximum(j - n_ctx, 0)
        pos = jnp.where(is_ctx, j % cb, jl % lb)
        nblk = jnp.where(is_ctx, cb, lb)
        sid = jnp.where(is_ctx, j // cb, self.Bc + jl // lb)
        return pos, nblk, sid


def _gated_residual(acc, h_ref, m_ref, i_out, w_out):
    return h_ref[...] + w_out * m_ref[0, 3 * i_out + 2:3 * i_out + 3, :] * acc


def _mm_body(x_ref, w_ref, *rest, nk, resid):
    o_ref = rest[-1]
    part = jnp.dot(x_ref[...], w_ref[...], preferred_element_type=F32)

    def finish(acc):
        return acc if resid is None else _gated_residual(acc, rest[0], rest[1], *resid)

    if nk == 1:
        o_ref[...] = finish(part).astype(o_ref.dtype)
    else:
        k = pl.program_id(2)

        @pl.when(k == 0)
        def _():
            o_ref[...] = part

        @pl.when((k > 0) & (k < nk - 1))
        def _():
            o_ref[...] += part

        @pl.when(k == nk - 1)
        def _():
            o_ref[...] = finish(o_ref[...] + part)


def _stacked(w_index):
    return (None,) * len(w_index)


class Resid(NamedTuple):
    h: jax.Array
    mods: jax.Array
    mod_row: object
    i_out: int
    w_out: float


def mm(x, w, *, bm, bn, bk=None, w_index=(), resid=None, out_dtype=F32):
    M, K = x.shape
    N = w.shape[-1]
    bk = K if bk is None else bk
    nk = K // bk
    assert M % bm == 0 and N % bn == 0 and K % bk == 0
    assert nk == 1 or out_dtype == F32
    in_specs = [pl.BlockSpec((bm, bk), lambda i, j, k: (i, k)),
                pl.BlockSpec(_stacked(w_index) + (bk, bn), lambda i, j, k: w_index + (k, j))]
    args = [x, w]
    if resid is not None:
        in_specs += [pl.BlockSpec((bm, bn), lambda i, j, k: (i, j)),
                     pl.BlockSpec((1, N_MOD, bn), lambda i, j, k: (resid.mod_row(i, bm), 0, j))]
        args += [resid.h, resid.mods]
    return pl.pallas_call(
        functools.partial(_mm_body, nk=nk, resid=None if resid is None else (resid.i_out, resid.w_out)),
        out_shape=jax.ShapeDtypeStruct((M, N), out_dtype),
        grid=(M // bm, N // bn, nk),
        in_specs=in_specs,
        out_specs=pl.BlockSpec((bm, bn), lambda i, j, k: (i, j)),
        compiler_params=_params("parallel", "parallel", "arbitrary"),
        name="mm",
    )(*args)


def _mm_ws_body(x_ref, w_ref, *rest, resid):
    o_ref, wb_ref = rest[-2:]

    @pl.when(pl.program_id(1) == 0)
    def _():
        wb_ref[...] = w_ref[...].astype(BF16)

    acc = jnp.dot(x_ref[...], wb_ref[...], preferred_element_type=F32)
    if resid is not None:
        acc = _gated_residual(acc, rest[0], rest[1], *resid)
    o_ref[...] = acc.astype(o_ref.dtype)


def mm_ws(x, w, *, bm, bn, w_index=(), n_cols=None, resid=None, out_dtype=F32):
    M, K = x.shape
    N = w.shape[-1] if n_cols is None else n_cols
    assert M % bm == 0 and N % bn == 0
    in_specs = [pl.BlockSpec((bm, K), lambda j, i: (i, 0)),
                pl.BlockSpec(_stacked(w_index) + (K, bn), lambda j, i: w_index + (0, j))]
    args = [x, w]
    if resid is not None:
        in_specs += [pl.BlockSpec((bm, bn), lambda j, i: (i, j)),
                     pl.BlockSpec((1, N_MOD, bn), lambda j, i: (resid.mod_row(i, bm), 0, j))]
        args += [resid.h, resid.mods]
    return pl.pallas_call(
        functools.partial(_mm_ws_body, resid=None if resid is None else (resid.i_out, resid.w_out)),
        out_shape=jax.ShapeDtypeStruct((M, N), out_dtype),
        grid=(N // bn, M // bm),
        in_specs=in_specs,
        out_specs=pl.BlockSpec((bm, bn), lambda j, i: (i, j)),
        scratch_shapes=[pltpu.VMEM((K, bn), BF16)],
        compiler_params=_params("arbitrary", "arbitrary"),
        name="mm_ws",
    )(*args)


def _mm_small_body(x_ref, w_ref, o_ref):
    o_ref[...] = jnp.dot(x_ref[...], w_ref[...].astype(BF16), preferred_element_type=F32)


def mm_small(x, w, *, bn, w_index=()):
    M, K = x.shape
    N = w.shape[-1]
    return pl.pallas_call(
        _mm_small_body,
        out_shape=jax.ShapeDtypeStruct((M, N), F32),
        grid=(N // bn,),
        in_specs=[pl.BlockSpec((M, K), lambda j: (0, 0)),
                  pl.BlockSpec(_stacked(w_index) + (K, bn), lambda j: w_index + (0, j))],
        out_specs=pl.BlockSpec((M, bn), lambda j: (0, j)),
        compiler_params=_params("parallel"),
        name="mm_small",
    )(x, w)


def _ffn_up_body(x_ref, w1_ref, w3_ref, o_ref, w1b_ref, w3b_ref):
    @pl.when(pl.program_id(1) == 0)
    def _():
        w1b_ref[...] = w1_ref[...].astype(BF16)
        w3b_ref[...] = w3_ref[...].astype(BF16)

    x = x_ref[...]
    a = jnp.dot(x, w1b_ref[...], preferred_element_type=F32)
    b = jnp.dot(x, w3b_ref[...], preferred_element_type=F32)
    o_ref[...] = (a * jax.nn.sigmoid(a) * b).astype(o_ref.dtype)


def ffn_up(x, w1, w3, *, w_index=(), bm=1024, bn=256):
    M, K = x.shape
    N = w1.shape[-1]
    w_spec = pl.BlockSpec(_stacked(w_index) + (K, bn), lambda j, i: w_index + (0, j))
    return pl.pallas_call(
        _ffn_up_body,
        out_shape=jax.ShapeDtypeStruct((M, N), BF16),
        grid=(N // bn, M // bm),
        in_specs=[pl.BlockSpec((bm, K), lambda j, i: (i, 0)), w_spec, w_spec],
        out_specs=pl.BlockSpec((bm, bn), lambda j, i: (i, j)),
        scratch_shapes=[pltpu.VMEM((K, bn), BF16), pltpu.VMEM((K, bn), BF16)],
        compiler_params=_params("arbitrary", "arbitrary"),
        name="ffn_up",
    )(x, w1, w3)


def _ada_body(*refs, i_in):
    h_ref, g_ref = refs[:2]
    x_ref = refs[-1]
    h = h_ref[...]
    x = h * lax.rsqrt(jnp.mean(h * h, axis=-1, keepdims=True) + NORM_EPS) * g_ref[...]
    if len(refs) == 4:
        mi_ref = refs[2]
        x = x * (1.0 + mi_ref[0, 3 * i_in + 1:3 * i_in + 2, :]) + mi_ref[0, 3 * i_in:3 * i_in + 1, :]
    x_ref[...] = x.astype(x_ref.dtype)


def ada_step(h, g, *, mod_row, m_in=None, i_in=0, out_dtype=BF16):
    T, D = h.shape
    R = ROW_BLOCK
    row_spec = pl.BlockSpec((R, D), lambda i: (i, 0))
    args, specs = [h, g.reshape(1, D)], [row_spec, pl.BlockSpec((1, D), lambda i: (0, 0))]
    if m_in is not None:
        args.append(m_in)
        specs.append(pl.BlockSpec((1, N_MOD, D), lambda i: (mod_row(i, R), 0, 0)))
    return pl.pallas_call(
        functools.partial(_ada_body, i_in=i_in),
        out_shape=jax.ShapeDtypeStruct((T, D), out_dtype), grid=(T // R,), in_specs=specs, out_specs=row_spec,
        compiler_params=_params("parallel"), name="ada_step",
    )(*args)


def _head_sums(x, hseg, n_pass=2):
    outs = []
    for c in range(x.shape[1] // LANES):
        rem = x[:, c * LANES:(c + 1) * LANES]
        acc = None
        for _ in range(n_pass):
            hi = rem.astype(BF16)
            part = jnp.dot(hi, hseg, preferred_element_type=F32)
            acc = part if acc is None else acc + part
            rem = rem - hi.astype(F32)
        outs.append(acc)
    return outs[0] if len(outs) == 1 else jnp.concatenate(outs, axis=1)


def _head_matrix():
    lane = jnp.arange(LANES)
    return (lane[:, None] // B_HEAD == lane[None, :] // B_HEAD).astype(BF16)


_TAIL_OFF = dict(wdf=0, wdb=B_DECAY_LORA, adf=2 * B_DECAY_LORA, adb=2 * B_DECAY_LORA + B_AAA_LORA,
                 gd=2 * B_DECAY_LORA + 2 * B_AAA_LORA)


def _prep_body(r_ref, k_ref, v_ref, t_ref, rp_ref, kp_ref, vp_ref, tp_ref, rn_ref, kn_ref, vn_ref, tn_ref,
               mu_r_ref, mu_k_ref, mu_v_ref, mu_t_ref, w2f_ref, w2b_ref, a2f_ref, a2b_ref, g2_ref, rowp_ref, hseg_ref,
               wf_ref, bf_ref, kdf_ref, wb_ref, bb_ref, kdb_ref, ro_ref, kko_ref, vo_ref, go_ref, bonus_ref,
               *, seqs):
    R = ROW_BLOCK
    pos, nblk, _ = seqs.locate(pl.program_id(0), R)
    has_prev = (pos != 0).astype(F32)
    has_next = (pos != nblk - 1).astype(F32)

    def shifted(cur_ref, p_ref, n_ref, mu_ref):
        x = cur_ref[0]
        row = lax.broadcasted_iota(jnp.int32, x.shape, 0)
        prev = jnp.where(row == 0, p_ref[0, SUBLANES - 1:SUBLANES, :] * has_prev, pltpu.roll(x, 1, axis=0))
        nxt = jnp.where(row == R - 1, n_ref[0, 0:1, :] * has_next, pltpu.roll(x, R - 1, axis=0))
        return x + mu_ref[...] * (0.5 * (prev + nxt) - x)

    rr = shifted(r_ref, rp_ref, rn_ref, mu_r_ref)
    kr = shifted(k_ref, kp_ref, kn_ref, mu_k_ref)
    vr = shifted(v_ref, vp_ref, vn_ref, mu_v_ref)
    tail = shifted(t_ref, tp_ref, tn_ref, mu_t_ref)
    t_tanh = jnp.tanh(tail).astype(BF16)
    t_lin = tail.astype(BF16)
    t_sig = jax.nn.sigmoid(tail).astype(BF16)

    def dot(a, w_ref):
        return jnp.dot(a, w_ref[...], preferred_element_type=F32)

    w0f, w0b, a0f, a0b, k_k, k_a, r_k = [rowp_ref[i:i + 1, :] for i in range(7)]
    hseg = hseg_ref[...]
    kk = kr * k_k
    kk = kk / jnp.maximum(jnp.sqrt(_head_sums(kk * kk, hseg)), 1e-12)
    for w0, a0, w2_ref, a2_ref, w_out, b_out, kd_out in ((w0f, a0f, w2f_ref, a2f_ref, wf_ref, bf_ref, kdf_ref),
                                                        (w0b, a0b, w2b_ref, a2b_ref, wb_ref, bb_ref, kdb_ref)):
        logw = -math.exp(-0.5) * jax.nn.sigmoid(w0 + dot(t_tanh, w2_ref))
        a = jax.nn.sigmoid(a0 + dot(t_lin, a2_ref))
        w_out[0] = jnp.exp(logw)
        b_out[0] = kk * a
        kd_out[0] = kr * (1.0 + (a - 1.0) * k_a)
    ro_ref[0] = rr
    kko_ref[0] = kk
    vo_ref[0] = vr
    go_ref[0] = dot(t_sig, g2_ref)
    bonus_ref[0] = _head_sums(rr * kr * r_k, hseg) * vr


def rwkv_prep(z, ztail, seqs, p):
    T = z.shape[0]
    R, CW = ROW_BLOCK, PREP_COLS
    NB, n8 = T // R, T // SUBLANES
    per = R // SUBLANES
    zv, z8 = z.reshape(NB, R, -1), z.reshape(n8, SUBLANES, -1)
    tv, t8 = ztail.reshape(NB, R, B_TAIL), ztail.reshape(n8, SUBLANES, B_TAIL)
    base = [(EV_A_IN + i * B_WIDTH) // CW for i in range(3)]

    def cur(b):
        return pl.BlockSpec((1, R, CW), lambda i, c: (i, 0, b + c))

    def prev(b):
        return pl.BlockSpec((1, SUBLANES, CW), lambda i, c: (jnp.maximum(i * per - 1, 0), 0, b + c))

    def nxt(b):
        return pl.BlockSpec((1, SUBLANES, CW), lambda i, c: (jnp.minimum((i + 1) * per, n8 - 1), 0, b + c))

    t_cur = pl.BlockSpec((1, R, B_TAIL), lambda i, c: (i, 0, 0))
    t_prev = pl.BlockSpec((1, SUBLANES, B_TAIL), lambda i, c: (jnp.maximum(i * per - 1, 0), 0, 0))
    t_next = pl.BlockSpec((1, SUBLANES, B_TAIL), lambda i, c: (jnp.minimum((i + 1) * per, n8 - 1), 0, 0))

    mu = p['mu']
    mu_main = mu[:3 * B_WIDTH].reshape(1, 3 * B_WIDTH)
    mu_t = mu[3 * B_WIDTH:].reshape(1, B_TAIL)

    def mu_spec(i):
        return pl.BlockSpec((1, CW), lambda r, c: (0, i * (B_WIDTH // CW) + c))

    def padded(w, off):
        return jnp.zeros((B_TAIL, B_WIDTH), BF16).at[off:off + w.shape[0]].set(w.astype(BF16))

    lora = [padded(p['w2f'], _TAIL_OFF['wdf']), padded(p['w2b'], _TAIL_OFF['wdb']),
            padded(p['a2f'], _TAIL_OFF['adf']), padded(p['a2b'], _TAIL_OFF['adb']), padded(p['g2'], _TAIL_OFF['gd'])]
    lora_spec = pl.BlockSpec((B_TAIL, CW), lambda i, c: (0, c))
    rowp = jnp.stack([p['w0f'], p['w0b'], p['a0f'], p['a0b'], p['k_k'], p['k_a'], p['r_k'].reshape(-1),
                      jnp.zeros((B_WIDTH,), F32)])
    out_spec = pl.BlockSpec((1, R, CW), lambda i, c: (i, 0, c))
    names = ('w_f', 'b_f', 'kd_f', 'w_b', 'b_b', 'kd_b', 'r', 'kk', 'v', 'g', 'bonus')
    outs = pl.pallas_call(
        functools.partial(_prep_body, seqs=seqs),
        out_shape=tuple(jax.ShapeDtypeStruct((NB, R, B_WIDTH), F32) for _ in names),
        grid=(NB, B_WIDTH // CW),
        in_specs=[cur(base[0]), cur(base[1]), cur(base[2]), t_cur,
                  prev(base[0]), prev(base[1]), prev(base[2]), t_prev,
                  nxt(base[0]), nxt(base[1]), nxt(base[2]), t_next,
                  mu_spec(0), mu_spec(1), mu_spec(2), pl.BlockSpec((1, B_TAIL), lambda i, c: (0, 0)),
                  lora_spec, lora_spec, lora_spec, lora_spec, lora_spec,
                  pl.BlockSpec((SUBLANES, CW), lambda i, c: (0, c)),
                  pl.BlockSpec((LANES, LANES), lambda i, c: (0, 0))],
        out_specs=tuple(out_spec for _ in names),
        compiler_params=_params("parallel", "parallel"),
        name="rwkv_prep",
    )(zv, zv, zv, tv, z8, z8, z8, t8, z8, z8, z8, t8, mu_main, mu_main, mu_main, mu_t, *lora, rowp, _head_matrix())
    return dict(zip(names, outs))


def _rwkv_body(w_ref, b_ref, kd_ref, r_ref, kk_ref, v_ref, g_ref, h_ref, t0_ref, o_ref, tf_ref,
               s_ref, vb_ref, xo_ref, *, seqs, NB, reverse):
    row_refs = (w_ref, b_ref, kd_ref, r_ref, kk_ref)
    TB = REC_TIME_BLOCK
    P, R, UW = s_ref.shape
    jb = pl.program_id(0)
    jb = (NB - 1 - jb) if reverse else jb
    pos, nblk, _ = seqs.locate(jb, TB)
    first = pos == ((nblk - 1) if reverse else 0)
    last = pos == (0 if reverse else (nblk - 1))

    @pl.when(first)
    def _():
        s_ref[...] = t0_ref[0]

    G = g_ref[...][None]
    Gb = G.astype(BF16)
    Hm = h_ref[...]

    Q = REC_SUB_UNITS
    subs = [slice(q * Q, (q + 1) * Q) for q in range(P // Q)]

    def seg_sum(x):
        return jnp.dot(x.reshape(Q * R, UW).astype(BF16), Hm, preferred_element_type=F32).reshape(Q, R, UW)

    def unit_rows(tile, j, qs):
        return jnp.stack([tile[j:j + 1, p * UW:(p + 1) * UW] for p in range(qs.start, qs.stop)])

    sub_iota = lax.broadcasted_iota(jnp.int32, (SUBLANES, UW), 0)

    def group(i, carry):
        gi = (TB // SUBLANES - 1 - i) if reverse else i
        t8 = pl.multiple_of(gi * SUBLANES, SUBLANES)
        tiles = [r[0, pl.ds(t8, SUBLANES), :] for r in row_refs]
        v_tile_b = v_ref[0, pl.ds(t8, SUBLANES), :].astype(BF16)
        for j in range(SUBLANES):
            for qs in subs:
                vb_ref[j, qs] = seg_sum(Gb * unit_rows(v_tile_b, j, qs))
        for jj in range(SUBLANES):
            j = SUBLANES - 1 - jj if reverse else jj
            for qs in subs:
                w, b, kd, r, kk = [unit_rows(tile, j, qs) for tile in tiles]
                S = s_ref[qs]
                u = seg_sum(S * kk)
                S = S * w - u * b + vb_ref[j, qs] * kd
                s_ref[qs] = S
                xo_ref[j, qs] = S * r
        for qs in subs:
            o_tiles = [jnp.zeros((SUBLANES, UW), F32) for _ in range(Q)]
            for j in range(SUBLANES):
                orow = jnp.sum(G * seg_sum(xo_ref[j, qs]), axis=1, keepdims=True)
                o_tiles = [jnp.where(sub_iota == j, orow[p], o_tiles[p]) for p in range(Q)]
            for p in range(Q):
                lo = (qs.start + p) * UW
                o_ref[0, pl.ds(t8, SUBLANES), lo:lo + UW] = o_tiles[p]
        return carry

    lax.fori_loop(0, TB // SUBLANES, group, 0)

    @pl.when(last)
    def _():
        tf_ref[0] = s_ref[...]


def rwkv7(rows, v, s0, seqs, *, reverse):
    T = v.shape[0] * v.shape[1]
    TB, UW, seg = REC_TIME_BLOCK, LANES, B_HEAD
    NB = T // TB
    hpu = UW // seg
    P = B_HEADS // hpu
    n_seq = s0.shape[0]
    t0 = s0.reshape(n_seq, P, hpu, seg, seg).transpose(0, 1, 3, 2, 4).reshape(n_seq, P, seg, UW)
    lane = jnp.arange(UW)
    g = (lane[None, :] % seg == jnp.arange(seg)[:, None]).astype(F32)

    def blk(j):
        return (NB - 1 - j) if reverse else j

    row_spec = pl.BlockSpec((1, TB, B_WIDTH), lambda j: (blk(j), 0, 0))
    st_spec = pl.BlockSpec((1, P, seg, UW), lambda j: (seqs.locate(blk(j), TB)[2], 0, 0, 0))
    o, s = pl.pallas_call(
        functools.partial(_rwkv_body, seqs=seqs, NB=NB, reverse=reverse),
        out_shape=(jax.ShapeDtypeStruct((NB, TB, B_WIDTH), F32), jax.ShapeDtypeStruct(t0.shape, F32)),
        grid=(NB,),
        in_specs=[row_spec] * 6 + [pl.BlockSpec((seg, UW), lambda j: (0, 0)),
                                   pl.BlockSpec((UW, UW), lambda j: (0, 0)), st_spec],
        out_specs=(row_spec, st_spec),
        scratch_shapes=[pltpu.VMEM((P, seg, UW), F32), pltpu.VMEM((SUBLANES, P, seg, UW), F32),
                        pltpu.VMEM((SUBLANES, P, seg, UW), F32)],
        compiler_params=_params("arbitrary"),
        name="rwkv7",
    )(*[x.reshape(NB, TB, B_WIDTH) for x in rows], v.reshape(NB, TB, B_WIDTH), g, _head_matrix(), t0)
    s = s.reshape(n_seq, P, seg, hpu, seg).transpose(0, 1, 3, 2, 4).reshape(s0.shape)
    return o.reshape(T, B_WIDTH), s


HG_STEP = 8


def _hgrn_body(q_ref, f_ref, v_ref, lb_ref, s0_ref, o_ref, sf_ref, s_ref, *, seqs, NB, Hh, reverse):
    C = HG_STEP
    TB = ROW_BLOCK
    jb = pl.program_id(1)
    jb = (NB - 1 - jb) if reverse else jb
    pos, nblk, _ = seqs.locate(jb, TB)
    first = pos == ((nblk - 1) if reverse else 0)
    last = pos == (0 if reverse else (nblk - 1))

    @pl.when(first)
    def _():
        s_ref[...] = s0_ref[0]

    row = lax.broadcasted_iota(jnp.int32, (C, LANES), 0)
    def shift(x, d):
        if d == 0:
            return x
        return pltpu.roll(x, (C - d) if reverse else d, axis=0)

    def has_source(d):
        return (row <= C - 1 - d) if reverse else (row >= d)

    def block(i, carry):
        ci = (TB // C - 1 - i) if reverse else i
        t0 = pl.multiple_of(ci * C, C)
        for h in range(Hh):
            sl = slice(h * LANES, (h + 1) * LANES)
            hq = q_ref[0, pl.ds(t0, C), sl]
            q = hq * jax.nn.sigmoid(hq) * A_DK ** -0.5
            lb = lb_ref[:, sl]
            f = lb + (1.0 - lb) * jax.nn.sigmoid(f_ref[0, pl.ds(t0, C), sl])
            v = v_ref[0, pl.ds(t0, C), sl]
            k = 1.0 - f
            b = jnp.log(f)
            s = 1
            while s < C:
                b = b + jnp.where(has_source(s), shift(b, s), 0.0)
                s *= 2
            st = s_ref[h]
            stb = st.astype(BF16)
            o = lax.dot_general((q * jnp.exp(b)).astype(BF16), stb, (((1,), (1,)), ((), ())),
                                preferred_element_type=F32)
            for d in range(C):
                arg = jnp.where(has_source(d), b - shift(b, d), -1e30)
                a = jnp.sum(q * shift(k, d) * jnp.exp(arg), axis=1, keepdims=True)
                o = o + a * shift(v, d)
            o_ref[0, pl.ds(t0, C), sl] = o
            end = 0 if reverse else C - 1
            bl = b[end:end + 1, :]
            kt = k * jnp.exp(bl - b)
            s_ref[h] = st * jnp.exp(bl) + lax.dot_general(v.astype(BF16), kt.astype(BF16), (((0,), (0,)), ((), ())),
                                                         preferred_element_type=F32)
        return carry

    lax.fori_loop(0, TB // C, block, 0)

    @pl.when(last)
    def _():
        sf_ref[0] = s_ref[...]


def hgrn2(zv, cols, lb, s0, seqs, *, reverse, heads_per_step=16):
    NB, TB, _ = zv.shape
    W = A_WIDTH
    H = W // LANES
    Hh = heads_per_step
    bw = Hh * LANES
    assert TB == ROW_BLOCK and H % Hh == 0 and all(c % bw == 0 for c in cols)

    def blk(j):
        return (NB - 1 - j) if reverse else j

    def in_spec(col):
        return pl.BlockSpec((1, TB, bw), lambda u, j: (blk(j), 0, col // bw + u))

    st_spec = pl.BlockSpec((1, Hh, LANES, LANES), lambda u, j: (seqs.locate(blk(j), TB)[2], u, 0, 0))
    st0 = jnp.swapaxes(s0, -1, -2)
    o, s = pl.pallas_call(
        functools.partial(_hgrn_body, seqs=seqs, NB=NB, Hh=Hh, reverse=reverse),
        out_shape=(jax.ShapeDtypeStruct((NB, TB, W), F32), jax.ShapeDtypeStruct(st0.shape, F32)),
        grid=(H // Hh, NB),
        in_specs=[in_spec(cols[0]), in_spec(cols[1]), in_spec(cols[2]),
                  pl.BlockSpec((1, bw), lambda u, j: (0, u)), st_spec],
        out_specs=(pl.BlockSpec((1, TB, bw), lambda u, j: (blk(j), 0, u)), st_spec),
        scratch_shapes=[pltpu.VMEM((Hh, LANES, LANES), F32)],
        compiler_params=_params("parallel", "arbitrary"),
        name="hgrn2",
    )(zv, zv, zv, lb, st0)
    return o.reshape(NB * TB, W), jnp.swapaxes(s, -1, -2)


def _even_post_body(oaf_ref, oab_ref, hg_ref, obf_ref, obb_ref, bonus_ref, g_ref, rowp_ref, hseg_ref, o_ref):
    hg_g, ln_g, ln_b = [rowp_ref[i:i + 1, :] for i in range(3)]
    oa = oaf_ref[0] + oab_ref[0]
    hg = hg_ref[0]
    oa = oa * lax.rsqrt(jnp.mean(oa * oa, axis=-1, keepdims=True) + NORM_EPS) * hg_g * (hg * jax.nn.sigmoid(hg))
    o_ref[0, :, :A_WIDTH] = oa.astype(o_ref.dtype)
    hseg = hseg_ref[...]
    ob = obf_ref[0] + obb_ref[0]
    cen = ob - _head_sums(ob, hseg) * (1.0 / B_HEAD)
    var = _head_sums(cen * cen, hseg) * (1.0 / B_HEAD)
    ob = (cen * lax.rsqrt(var + B_GN_EPS) * ln_g + ln_b + bonus_ref[0]) * g_ref[0]
    o_ref[0, :, A_WIDTH:] = ob.astype(o_ref.dtype)


def even_post(oa_f, oa_b, zv, ob_f, ob_b, bonus, g, p):
    NB, R, _ = zv.shape
    T = NB * R
    W = A_WIDTH
    assert W == B_WIDTH and (4 * A_WIDTH) % W == 0
    spec = pl.BlockSpec((1, R, W), lambda i: (i, 0, 0))
    rowp = jnp.stack([p['hg_g'], p['ln_g'], p['ln_b']] + [jnp.zeros((W,), F32)] * 5)
    v3 = lambda x: x.reshape(NB, R, W)
    out = pl.pallas_call(
        _even_post_body,
        out_shape=jax.ShapeDtypeStruct((NB, R, 2 * W), BF16),
        grid=(NB,),
        in_specs=[spec, spec, pl.BlockSpec((1, R, W), lambda i: (i, 0, 4 * A_WIDTH // W)), spec, spec, spec, spec,
                  pl.BlockSpec((SUBLANES, W), lambda i: (0, 0)), pl.BlockSpec((LANES, LANES), lambda i: (0, 0))],
        out_specs=pl.BlockSpec((1, R, 2 * W), lambda i: (i, 0, 0)),
        compiler_params=_params("parallel"),
        name="even_post",
    )(v3(oa_f), v3(oa_b), zv, v3(ob_f), v3(ob_b), bonus, g, rowp, _head_matrix())
    return out.reshape(T, 2 * W)


def even_mixer(z, ztail, seqs, st, p):
    T = z.shape[0]
    zv = z.reshape(T // ROW_BLOCK, ROW_BLOCK, -1)
    oa_f, sf = hgrn2(zv, (0, A_WIDTH, 3 * A_WIDTH), p['lb_f'][None], st['hf'], seqs, reverse=False)
    oa_b, sb = hgrn2(zv, (0, 2 * A_WIDTH, 3 * A_WIDTH), p['lb_b'][None], st['hb'], seqs, reverse=True)
    q = rwkv_prep(z, ztail, seqs, p)
    ob_f, rf = rwkv7((q['w_f'], q['b_f'], q['kd_f'], q['r'], q['kk']), q['v'], st['rf'], seqs, reverse=False)
    ob_b, rb = rwkv7((q['w_b'], q['b_b'], q['kd_b'], q['r'], q['kk']), q['v'], st['rb'], seqs, reverse=True)
    return even_post(oa_f, oa_b, zv, ob_f, ob_b, q['bonus'], q['g'], p), (sf, sb, rf, rb)


def _swap_halves(x):
    lane = lax.broadcasted_iota(jnp.int32, x.shape, 1)
    quarter = ROPE_AXIS_DIM // 2
    return jnp.where(lane % ROPE_AXIS_DIM < quarter, pltpu.roll(x, LANES - quarter, axis=1),
                     pltpu.roll(x, quarter, axis=1))


def _attn_body(*refs, has_ctx):
    if has_ctx:
        (q_ref, kn_ref, vn_ref, ck_ref, cv_ref, qc_ref, qs_ref, kc_ref, ks_ref, qg_ref, kg_ref,
         o_ref, k_sc, v_sc) = refs
    else:
        q_ref, kn_ref, vn_ref, qg_ref, kg_ref, o_ref, ko_ref, vo_ref, k_sc, v_sc = refs

    def norm(x, g_ref):
        return x * lax.rsqrt(jnp.mean(x * x, axis=-1, keepdims=True) + NORM_EPS) * g_ref[...]

    @pl.when(pl.program_id(2) == 0)
    def _():
        k = norm(kn_ref[0], kg_ref)
        v = vn_ref[0]
        if has_ctx:
            n_past = ck_ref.shape[1]
            k = k * kc_ref[...] + _swap_halves(k) * ks_ref[...]
            k_sc[:n_past] = ck_ref[0].astype(BF16)
            v_sc[:n_past] = cv_ref[0].astype(BF16)
            k_sc[n_past:] = k.astype(BF16)
            v_sc[n_past:] = v.astype(BF16)
        else:
            ko_ref[0] = k
            vo_ref[0] = v
            k_sc[...] = k.astype(BF16)
            v_sc[...] = v.astype(BF16)

    k = k_sc[...]
    v = v_sc[...]
    scale = C_HEAD_DIM ** -0.5
    for g in range(C_GROUP):
        sl = slice(g * C_HEAD_DIM, (g + 1) * C_HEAD_DIM)
        q = norm(q_ref[0, :, sl], qg_ref)
        if has_ctx:
            q = q * qc_ref[...] + _swap_halves(q) * qs_ref[...]
        s = lax.dot_general(q.astype(BF16), k, (((1,), (1,)), ((), ())), preferred_element_type=F32) * scale
        m = jnp.max(s, axis=-1, keepdims=True)
        e = jnp.exp(s - m)
        p = e / jnp.sum(e, axis=-1, keepdims=True)
        o_ref[0, :, sl] = jnp.dot(p.astype(BF16), v, preferred_element_type=F32).astype(o_ref.dtype)


def _rope_tables(L):
    n_rows = L // GRID_W
    row = jnp.repeat(jnp.arange(n_rows), GRID_W).astype(F32)
    col = jnp.tile(jnp.arange(GRID_W), n_rows).astype(F32)
    half = ROPE_AXIS_DIM // 2
    inv = ROPE_THETA ** (-jnp.arange(half, dtype=F32) / half)
    ar, ac = row[:, None] * inv, col[:, None] * inv
    cos_t = jnp.concatenate([jnp.cos(ar), jnp.cos(ar), jnp.cos(ac), jnp.cos(ac)], axis=1)
    sin_t = jnp.concatenate([-jnp.sin(ar), jnp.sin(ar), -jnp.sin(ac), jnp.sin(ac)], axis=1)
    return cos_t, sin_t


def attention(z, row0, B, L, qn_g, kn_g, ctx_k=None, ctx_v=None):
    T = z.shape[0]
    assert row0 % L == 0 and L % Q_BLOCK == 0
    has_ctx = ctx_k is not None
    gw = C_GROUP * C_HEAD_DIM
    zq = z.reshape(T // Q_BLOCK, Q_BLOCK, -1)
    zl = z.reshape(T // L, L, -1)
    qb0, lb0 = row0 // Q_BLOCK, row0 // L
    nq = L // Q_BLOCK
    kcol, vcol = C_WIDTH // C_HEAD_DIM, (C_WIDTH + C_KV_WIDTH) // C_HEAD_DIM
    g_spec = pl.BlockSpec((1, C_HEAD_DIM), lambda b, h, i: (0, 0))
    in_specs = [pl.BlockSpec((1, Q_BLOCK, gw), lambda b, h, i: (qb0 + b * nq + i, 0, h)),
                pl.BlockSpec((1, L, C_HEAD_DIM), lambda b, h, i: (lb0 + b, 0, kcol + h)),
                pl.BlockSpec((1, L, C_HEAD_DIM), lambda b, h, i: (lb0 + b, 0, vcol + h))]
    args = [zq, zl, zl]
    o_spec = pl.BlockSpec((1, Q_BLOCK, gw), lambda b, h, i: (b, i, h))
    o_shape = jax.ShapeDtypeStruct((B, L, C_WIDTH), BF16)
    if has_ctx:
        n_past = ctx_k.shape[1]
        c_spec = pl.BlockSpec((1, n_past, C_HEAD_DIM), lambda b, h, i: (b, 0, h))
        cos_t, sin_t = _rope_tables(L)
        q_tab = pl.BlockSpec((Q_BLOCK, C_HEAD_DIM), lambda b, h, i: (i, 0))
        k_tab = pl.BlockSpec((L, C_HEAD_DIM), lambda b, h, i: (0, 0))
        in_specs += [c_spec, c_spec, q_tab, q_tab, k_tab, k_tab]
        args += [ctx_k.reshape(B, n_past, C_KV_WIDTH), ctx_v.reshape(B, n_past, C_KV_WIDTH), cos_t, sin_t, cos_t, sin_t]
        out_shape, out_specs = o_shape, o_spec
        Lk = n_past + L
    else:
        kv_spec = pl.BlockSpec((1, L, C_HEAD_DIM), lambda b, h, i: (b, 0, h))
        kv_shape = jax.ShapeDtypeStruct((B, L, C_KV_WIDTH), F32)
        out_shape, out_specs = (o_shape, kv_shape, kv_shape), (o_spec, kv_spec, kv_spec)
        Lk = L
    in_specs += [g_spec, g_spec]
    args += [qn_g.reshape(1, C_HEAD_DIM), kn_g.reshape(1, C_HEAD_DIM)]
    res = pl.pallas_call(
        functools.partial(_attn_body, has_ctx=has_ctx),
        out_shape=out_shape, grid=(B, C_KV_HEADS, nq), in_specs=in_specs, out_specs=out_specs,
        scratch_shapes=[pltpu.VMEM((Lk, C_HEAD_DIM), BF16), pltpu.VMEM((Lk, C_HEAD_DIM), BF16)],
        compiler_params=_params("parallel", "parallel", "arbitrary"),
        name="attn",
    )(*args)
    if has_ctx:
        return res, None, None
    o, k, v = res
    return o, k.reshape(B, L, C_KV_HEADS, C_HEAD_DIM), v.reshape(B, L, C_KV_HEADS, C_HEAD_DIM)


def _dot3(ah, al, u):
    uh = u.astype(BF16)
    ul = (u - uh.astype(F32)).astype(BF16)
    return (jnp.dot(ah, uh, preferred_element_type=F32) + jnp.dot(ah, ul, preferred_element_type=F32)
            + jnp.dot(al, uh, preferred_element_type=F32))


def _conv3(x, w_ref, b_ref):
    L = x.shape[0]
    row = lax.broadcasted_iota(jnp.int32, x.shape, 0)
    prev = jnp.where(row == 0, 0.0, pltpu.roll(x, 1, axis=0))
    nxt = jnp.where(row == L - 1, 0.0, pltpu.roll(x, L - 1, axis=0))
    return prev * w_ref[0:1, :] + x * w_ref[1:2, :] + nxt * w_ref[2:3, :] + b_ref[...]


def _dft_fwd_body(*refs, conv):
    if conv:
        ah_ref, al_ref, u_ref, cw_ref, cb_ref, o_ref = refs
        u = _conv3(u_ref[0], cw_ref, cb_ref)
    else:
        ah_ref, al_ref, u_ref, o_ref = refs
        u = u_ref[0]
    o_ref[0] = _dot3(ah_ref[...], al_ref[...], u)


def _dft_inv_body(*refs, conv_u):
    if conv_u:
        ah_ref, al_ref, uf_ref, sc_ref, u_ref, cwu_ref, cbu_ref, x_ref, cwx_ref, cbx_ref, bias_ref, o_ref = refs
        u = _conv3(u_ref[0], cwu_ref, cbu_ref)
    else:
        ah_ref, al_ref, uf_ref, sc_ref, u_ref, x_ref, cwx_ref, cbx_ref, bias_ref, o_ref = refs
        u = u_ref[0]
    y = _dot3(ah_ref[...], al_ref[...], uf_ref[0] * sc_ref[...])
    gate = _conv3(x_ref[0], cwx_ref, cbx_ref)
    o_ref[0] = ((y + u * bias_ref[...]) * gate).astype(o_ref.dtype)


def dft_mm(a, u, *, bn=512):
    ah, al = a
    M, K = ah.shape
    B, _, C = u.shape
    a_spec = pl.BlockSpec((M, K), lambda b, j: (0, 0))
    return pl.pallas_call(
        functools.partial(_dft_fwd_body, conv=False),
        out_shape=jax.ShapeDtypeStruct((B, M, C), F32),
        grid=(B, C // bn),
        in_specs=[a_spec, a_spec, pl.BlockSpec((1, K, bn), lambda b, j: (b, 0, j))],
        out_specs=pl.BlockSpec((1, M, bn), lambda b, j: (b, 0, j)),
        compiler_params=_params("parallel", "parallel"),
        name="dft_mm",
    )(ah, al, u)


def hyena(z, row0, B, L, p):
    T = z.shape[0]
    N = 2 * L
    zl = z.reshape(T // L, L, -1)
    lb0 = row0 // L
    filt = _hyena_filters(L, p['f_w1'], p['f_b1'], p['f_w2'], p['f_b2'], p['f_w3'], p['f_b3'], p['f_w4'],
                          p['sin_freq'])
    fmat = _dft_matrix(L)
    fwd, inv = _split_bf16(fmat), _split_bf16(fmat.T)
    h_all = filt.reshape(L, HY_ORDER * HY_W)
    kf = dft_mm(fwd, jnp.concatenate([h_all[:1], 2.0 * h_all[1:]], axis=0)[None])[0]
    r = jnp.arange(N)
    herm = jnp.where((r == 0) | (r == L), 1.0, 2.0).astype(F32) / N
    scale = (kf[jnp.where(r <= L, r, r - L)] * herm[:, None]).reshape(N, HY_ORDER, HY_W)
    conv_w, conv_b = p['conv_w'], p['conv_b'].reshape(1, -1)
    bias = p['hy_bias']

    def z_spec(piece, bn):
        c0 = (C_IN + piece * HY_W) // bn
        return pl.BlockSpec((1, L, bn), lambda b, j: (lb0 + b, 0, c0 + j))

    def cw_specs(piece, bn):
        c0 = piece * HY_W // bn
        return [pl.BlockSpec((3, bn), lambda b, j: (0, c0 + j)), pl.BlockSpec((1, bn), lambda b, j: (0, c0 + j))]

    bn_max = math.gcd(C_IN, HY_W)
    bn_fwd = min(bn_max, HYENA_FWD_BLOCK // L)
    bn_inv = min(bn_max, HYENA_INV_BLOCK // L)

    def forward(u, conv, bn=bn_fwd):
        a_spec = pl.BlockSpec((N, L), lambda b, j: (0, 0))
        if conv:
            in_specs, args = [z_spec(0, bn)] + cw_specs(0, bn), [zl, conv_w, conv_b]
        else:
            in_specs, args = [pl.BlockSpec((1, L, bn), lambda b, j: (b, 0, j))], [u]
        return pl.pallas_call(
            functools.partial(_dft_fwd_body, conv=conv),
            out_shape=jax.ShapeDtypeStruct((B, N, HY_W), F32), grid=(B, HY_W // bn),
            in_specs=[a_spec, a_spec] + in_specs, out_specs=pl.BlockSpec((1, N, bn), lambda b, j: (b, 0, j)),
            compiler_params=_params("parallel", "parallel"), name="hyena_fwd",
        )(*fwd, *args)

    def inverse(uf, order, u, gate_piece, out_dtype, bn=bn_inv):
        a_spec = pl.BlockSpec((L, N), lambda b, j: (0, 0))
        blk = pl.BlockSpec((1, L, bn), lambda b, j: (b, 0, j))
        conv_u = u is None
        in_specs = [a_spec, a_spec, pl.BlockSpec((1, N, bn), lambda b, j: (b, 0, j)),
                    pl.BlockSpec((N, bn), lambda b, j: (0, j))]
        args = [*inv, uf, scale[:, order]]
        if conv_u:
            in_specs += [z_spec(0, bn)] + cw_specs(0, bn)
            args += [zl, conv_w, conv_b]
        else:
            in_specs.append(blk)
            args.append(u)
        in_specs += [z_spec(gate_piece, bn)] + cw_specs(gate_piece, bn) + [pl.BlockSpec((1, bn), lambda b, j: (0, j))]
        args += [zl, conv_w, conv_b, bias[order][None]]
        return pl.pallas_call(
            functools.partial(_dft_inv_body, conv_u=conv_u),
            out_shape=jax.ShapeDtypeStruct((B, L, HY_W), out_dtype), grid=(B, HY_W // bn),
            in_specs=in_specs, out_specs=blk,
            compiler_params=_params("parallel", "parallel"), name="hyena_inv",
        )(*args)

    z1 = inverse(forward(None, True), 0, None, 1, F32)
    return inverse(forward(z1, False), 1, z1, 2, BF16)


def _split_bf16(a):
    hi = a.astype(BF16)
    return hi, (a - hi.astype(F32)).astype(BF16)


def _dft_matrix(L):
    N = 2 * L
    r = jnp.arange(N, dtype=jnp.int32)[:, None]
    s = jnp.arange(L, dtype=jnp.int32)[None, :]
    is_cos = r <= L
    f = jnp.where(is_cos, r, r - L)
    ang = (2.0 * math.pi / N) * ((f * s) % N).astype(F32)
    return jnp.where(is_cos, jnp.cos(ang), jnp.sin(ang))


def _hyena_filters(L, w1, b1, w2, b2, w3, b3, w4, freq):
    t = jnp.linspace(0.0, 1.0, L, dtype=F32)[:, None]
    w_pos = 2.0 * math.pi * jnp.arange(L, dtype=F32)[:, None] / L
    bands = jnp.linspace(1e-4, HY_BANDS - 1, HY_BANDS, dtype=F32)[None]
    zpos = jnp.concatenate([t, jnp.cos(bands * w_pos), -jnp.sin(bands * w_pos)], axis=-1)
    f = freq.astype(F32)
    hdn = jnp.sin(f * (zpos @ w1 + b1))
    hdn = jnp.sin(f * (hdn @ w2 + b2))
    hdn = jnp.sin(f * (hdn @ w3 + b3))
    h = mm(hdn.astype(BF16), w4.astype(BF16), bm=L, bn=1024)
    deltas = jnp.abs(jnp.linspace(HY_MIN_DECAY, HY_MAX_DECAY, HY_ORDER * HY_W, dtype=F32))
    h = h * jnp.exp(-t * deltas)
    return h.reshape(L, HY_ORDER, HY_W)


def _odd_core(z, row0, B, L, ctx_k, ctx_v, p):
    o_c, k, v = attention(z, row0, B, L, p['qn_g'], p['kn_g'], ctx_k, ctx_v)
    return jnp.concatenate([o_c, hyena(z, row0, B, L, p)], axis=-1), k, v


def kernel(x_prompt, x_sample, c, state_hgrn_fwd, state_hgrn_bwd, state_rwkv_fwd, state_rwkv_bwd, cache_k, cache_v, c_ctx, ada_w, ada_b, norm_g, ffn_w1, ffn_w3, ffn_w2, final_norm_g, ev_w_in, ev_w_out, hg_lb_fwd, hg_lb_bwd, hg_norm_g, rw_mu, rw_w0_f, rw_w2_f, rw_w0_b, rw_w2_b, rw_a0_f, rw_a2_f, rw_a0_b, rw_a2_b, rw_g2, rw_kk, rw_ka, rw_rk, rw_ln_g, rw_ln_b, od_w_in, od_w_out, at_qn_g, at_kn_g, hy_conv_w, hy_conv_b, hy_f_w1, hy_f_b1, hy_f_w2, hy_f_b2, hy_f_w3, hy_f_b3, hy_f_w4, hy_sin_freq, hy_bias):
    Bc, Lc, _ = x_prompt.shape
    Bl, Ll, _ = x_sample.shape
    seqs = Seqs(Bc, Lc, Bl, Ll)
    n_ctx_tok = Bc * Lc
    n_tok = seqs.n_tok
    assert Lc % ROW_BLOCK == 0 and Ll % ROW_BLOCK == 0 and n_ctx_tok % Ll == 0

    def mod_row(i, rows):
        return jnp.maximum(i * rows // Ll - (n_ctx_tok // Ll - 1), 0)

    h = jnp.concatenate([x_prompt.reshape(n_ctx_tok, D_MODEL), x_sample.reshape(Bl * Ll, D_MODEL)], axis=0)
    cvec = jnp.concatenate([c_ctx[None], c], axis=0)
    cvec = jnp.pad(jax.nn.silu(cvec), ((0, SUBLANES - (1 + Bl)), (0, 0))).astype(BF16)
    mods = [(mm_small(cvec, ada_w, bn=1024, w_index=(l,)) + ada_b[l])[:1 + Bl].reshape(1 + Bl, N_MOD, D_MODEL)
            for l in range(DEPTH)]

    lb_fwd_all = jnp.cumsum(jax.nn.softmax(hg_lb_fwd, axis=0)[1:], axis=0)
    lb_bwd_all = jnp.cumsum(jax.nn.softmax(hg_lb_bwd, axis=0)[1:], axis=0)

    def with_ctx_zeros(s):
        return jnp.concatenate([jnp.zeros((Bc,) + s.shape[1:], F32), s], axis=0)

    ffn_w2_bf16 = ffn_w2.astype(BF16)

    def ffn(x, l, s, resid):
        mid = ffn_up(x, ffn_w1, ffn_w3, w_index=(l, s))
        return mm(mid, ffn_w2_bf16, bm=1024, bn=512, bk=D_FF // 2, w_index=(l, s), resid=resid)

    new_hf, new_hb, new_rf, new_rb, new_k, new_v = [], [], [], [], [], []
    x = ada_step(h, norm_g[0, 0], mod_row=mod_row, m_in=mods[0], i_in=0)
    for l in range(DEPTH):
        m = mods[l]
        h = ffn(x, l, 0, Resid(h, m, mod_row, 0, 0.5))
        xn = ada_step(h, norm_g[l, 1], mod_row=mod_row, m_in=m, i_in=1)
        j = l // 2
        if l % 2 == 0:
            p = dict(lb_f=lb_fwd_all[j], lb_b=lb_bwd_all[j], hg_g=hg_norm_g[j], mu=rw_mu[j],
                     w0f=rw_w0_f[j], w2f=rw_w2_f[j], w0b=rw_w0_b[j], w2b=rw_w2_b[j],
                     a0f=rw_a0_f[j], a2f=rw_a2_f[j], a0b=rw_a0_b[j], a2b=rw_a2_b[j], g2=rw_g2[j],
                     k_k=rw_kk[j], k_a=rw_ka[j], r_k=rw_rk[j], ln_g=rw_ln_g[j], ln_b=rw_ln_b[j])
            st = dict(hf=with_ctx_zeros(state_hgrn_fwd[:, j]), hb=with_ctx_zeros(state_hgrn_bwd[:, j]),
                      rf=with_ctx_zeros(state_rwkv_fwd[:, j]), rb=with_ctx_zeros(state_rwkv_bwd[:, j]))
            z = mm_ws(xn, ev_w_in, bm=1024, bn=512, w_index=(j,), n_cols=EV_MAIN)
            ztail = mm_ws(xn, ev_w_in[j, :, EV_MAIN:], bm=1024, bn=B_TAIL)
            o, (s_hf, s_hb, s_rf, s_rb) = even_mixer(z, ztail, seqs, st, p)
            new_hf.append(s_hf[:Bc])
            new_hb.append(s_hb[:Bc])
            new_rf.append(s_rf[:Bc])
            new_rb.append(s_rb[:Bc])
            w_out = ev_w_out
        else:
            p = dict(qn_g=at_qn_g[j], kn_g=at_kn_g[j], conv_w=hy_conv_w[j], conv_b=hy_conv_b[j],
                     f_w1=hy_f_w1[j], f_b1=hy_f_b1[j], f_w2=hy_f_w2[j], f_b2=hy_f_b2[j], f_w3=hy_f_w3[j],
                     f_b3=hy_f_b3[j], f_w4=hy_f_w4[j], sin_freq=hy_sin_freq[j], hy_bias=hy_bias[j])
            z = mm_ws(xn, od_w_in, bm=1024, bn=512, w_index=(j,))
            o_ctx, k_ctx, v_ctx = _odd_core(z, 0, Bc, Lc, None, None, p)
            o_lat, _, _ = _odd_core(z, n_ctx_tok, Bl, Ll, cache_k[:, j], cache_v[:, j], p)
            new_k.append(k_ctx)
            new_v.append(v_ctx)
            o = jnp.concatenate([o_ctx.reshape(n_ctx_tok, -1), o_lat.reshape(Bl * Ll, -1)], axis=0)
            w_out = od_w_out
        h = mm_ws(o, w_out, bm=1024, bn=512, w_index=(j,), resid=Resid(h, m, mod_row, 1, 1.0))
        x = ada_step(h, norm_g[l, 2], mod_row=mod_row, m_in=m, i_in=2)
        h = ffn(x, l, 1, Resid(h, m, mod_row, 2, 0.5))
        if l + 1 < DEPTH:
            x = ada_step(h, norm_g[l + 1, 0], mod_row=mod_row, m_in=mods[l + 1], i_in=0)
    y_out = ada_step(h, final_norm_g, mod_row=mod_row, out_dtype=F32)
    return (y_out[:n_ctx_tok].reshape(Bc, Lc, D_MODEL), y_out[n_ctx_tok:].reshape(Bl, Ll, D_MODEL),
            jnp.stack(new_hf, axis=1), jnp.stack(new_hb, axis=1), jnp.stack(new_rf, axis=1),
            jnp.stack(new_rb, axis=1), jnp.stack(new_k, axis=1), jnp.stack(new_v, axis=1))
```

```python
import functools
import math
from typing import NamedTuple

import jax
import jax.numpy as jnp
from jax import lax
from jax.experimental import pallas as pl
from jax.experimental.pallas import tpu as pltpu

F32 = jnp.float32
BF16 = jnp.bfloat16

D_MODEL = 4096
DEPTH = 2
GRID_W = 64
N_MOD = 9
D_FF = 11008
NORM_EPS = 1e-6
A_WIDTH = D_MODEL // 2
A_HEADS = 16
A_DK = A_WIDTH // A_HEADS
B_WIDTH = D_MODEL // 2
B_HEAD = 64
B_HEADS = B_WIDTH // B_HEAD
B_DECAY_LORA = max(32, int(round(1.8 * B_WIDTH ** 0.5 / 32)) * 32)
B_AAA_LORA = max(32, int(round(1.8 * B_WIDTH ** 0.5 / 32)) * 32)
B_GATE_LORA = max(32, int(round(0.6 * B_WIDTH ** 0.8 / 32)) * 32)
B_TAIL = 2 * B_DECAY_LORA + 2 * B_AAA_LORA + B_GATE_LORA
B_GN_EPS = 64e-5
EV_A_IN = 5 * A_WIDTH
EV_B_IN = 3 * B_WIDTH + B_TAIL
EV_MAIN = EV_A_IN + 3 * B_WIDTH
C_HEAD_DIM = 128
C_HEADS = 16
C_KV_HEADS = 4
C_GROUP = C_HEADS // C_KV_HEADS
C_WIDTH = C_HEADS * C_HEAD_DIM
C_KV_WIDTH = C_KV_HEADS * C_HEAD_DIM
C_IN = C_WIDTH + 2 * C_KV_WIDTH
ROPE_AXIS_DIM = C_HEAD_DIM // 2
ROPE_THETA = 10000.0
Q_BLOCK = 128
HY_W = D_MODEL // 2
HY_ORDER = 2
HY_BANDS = 16
HY_TARGET = 1e-2
HY_MIN_DECAY = math.log(HY_TARGET) / 1.5
HY_MAX_DECAY = math.log(HY_TARGET) / 0.3

V7X_VMEM_LIMIT_BYTES = 56 * 1024 * 1024
LANES = 128
SUBLANES = 8
ROW_BLOCK = 256
REC_TIME_BLOCK = 128
REC_SUB_UNITS = 4
PREP_COLS = 512
HYENA_FWD_BLOCK = 1024 * 512
HYENA_INV_BLOCK = 1024 * 256


def _params(*sem):
    return pltpu.CompilerParams(dimension_semantics=sem, vmem_limit_bytes=V7X_VMEM_LIMIT_BYTES)


class Seqs(NamedTuple):
    Bc: int
    Lc: int
    Bl: int
    Ll: int

    @property
    def n_tok(self):
        return self.Bc * self.Lc + self.Bl * self.Ll

    def locate(self, j, blk):
        n_ctx = self.Bc * self.Lc // blk
        cb, lb = self.Lc // blk, self.Ll // blk
        is_ctx = j < n_ctx
        jl = jnp.maximum(j - n_ctx, 0)
        pos = jnp.where(is_ctx, j % cb, jl % lb)
        nblk = jnp.where(is_ctx, cb, lb)
        sid = jnp.where(is_ctx, j // cb, self.Bc + jl // lb)
        return pos, nblk, sid, is_ctx


def _gated_residual(acc, h_ref, m_ref, i_out, w_out):
    return h_ref[...] + w_out * m_ref[0, 3 * i_out + 2:3 * i_out + 3, :] * acc


def _mm_body(x_ref, w_ref, *rest, nk, resid):
    o_ref = rest[-1]
    part = jnp.dot(x_ref[...], w_ref[...], preferred_element_type=F32)

    def finish(acc):
        return acc if resid is None else _gated_residual(acc, rest[0], rest[1], *resid)

    if nk == 1:
        o_ref[...] = finish(part).astype(o_ref.dtype)
    else:
        k = pl.program_id(2)

        @pl.when(k == 0)
        def _():
            o_ref[...] = part

        @pl.when((k > 0) & (k < nk - 1))
        def _():
            o_ref[...] += part

        @pl.when(k == nk - 1)
        def _():
            o_ref[...] = finish(o_ref[...] + part)


def _stacked(w_index):
    return (None,) * len(w_index)


class Resid(NamedTuple):
    h: jax.Array
    mods: jax.Array
    mod_row: object
    i_out: int
    w_out: float


def mm(x, w, *, bm, bn, bk=None, w_index=(), resid=None, out_dtype=F32):
    M, K = x.shape
    N = w.shape[-1]
    bk = K if bk is None else bk
    nk = K // bk
    assert M % bm == 0 and N % bn == 0 and K % bk == 0
    assert nk == 1 or out_dtype == F32
    in_specs = [pl.BlockSpec((bm, bk), lambda i, j, k: (i, k)),
                pl.BlockSpec(_stacked(w_index) + (bk, bn), lambda i, j, k: w_index + (k, j))]
    args = [x, w]
    if resid is not None:
        in_specs += [pl.BlockSpec((bm, bn), lambda i, j, k: (i, j)),
                     pl.BlockSpec((1, N_MOD, bn), lambda i, j, k: (resid.mod_row(i, bm), 0, j))]
        args += [resid.h, resid.mods]
    return pl.pallas_call(
        functools.partial(_mm_body, nk=nk, resid=None if resid is None else (resid.i_out, resid.w_out)),
        out_shape=jax.ShapeDtypeStruct((M, N), out_dtype),
        grid=(M // bm, N // bn, nk),
        in_specs=in_specs,
        out_specs=pl.BlockSpec((bm, bn), lambda i, j, k: (i, j)),
        compiler_params=_params("parallel", "parallel", "arbitrary"),
        name="mm",
    )(*args)


def _mm_ws_body(x_ref, w_ref, *rest, resid):
    o_ref, wb_ref = rest[-2:]

    @pl.when(pl.program_id(1) == 0)
    def _():
        wb_ref[...] = w_ref[...].astype(BF16)

    acc = jnp.dot(x_ref[...], wb_ref[...], preferred_element_type=F32)
    if resid is not None:
        acc = _gated_residual(acc, rest[0], rest[1], *resid)
    o_ref[...] = acc.astype(o_ref.dtype)


def mm_ws(x, w, *, bm, bn, w_index=(), n_cols=None, resid=None, out_dtype=F32):
    M, K = x.shape
    N = w.shape[-1] if n_cols is None else n_cols
    assert M % bm == 0 and N % bn == 0
    in_specs = [pl.BlockSpec((bm, K), lambda j, i: (i, 0)),
                pl.BlockSpec(_stacked(w_index) + (K, bn), lambda j, i: w_index + (0, j))]
    args = [x, w]
    if resid is not None:
        in_specs += [pl.BlockSpec((bm, bn), lambda j, i: (i, j)),
                     pl.BlockSpec((1, N_MOD, bn), lambda j, i: (resid.mod_row(i, bm), 0, j))]
        args += [resid.h, resid.mods]
    return pl.pallas_call(
        functools.partial(_mm_ws_body, resid=None if resid is None else (resid.i_out, resid.w_out)),
        out_shape=jax.ShapeDtypeStruct((M, N), out_dtype),
        grid=(N // bn, M // bm),
        in_specs=in_specs,
        out_specs=pl.BlockSpec((bm, bn), lambda j, i: (i, j)),
        scratch_shapes=[pltpu.VMEM((K, bn), BF16)],
        compiler_params=_params("arbitrary", "arbitrary"),
        name="mm_ws",
    )(*args)


def _mm_small_body(x_ref, w_ref, o_ref):
    o_ref[...] = jnp.dot(x_ref[...], w_ref[...].astype(BF16), preferred_element_type=F32)


def mm_small(x, w, *, bn, w_index=()):
    M, K = x.shape
    N = w.shape[-1]
    return pl.pallas_call(
        _mm_small_body,
        out_shape=jax.ShapeDtypeStruct((M, N), F32),
        grid=(N // bn,),
        in_specs=[pl.BlockSpec((M, K), lambda j: (0, 0)),
                  pl.BlockSpec(_stacked(w_index) + (K, bn), lambda j: w_index + (0, j))],
        out_specs=pl.BlockSpec((M, bn), lambda j: (0, j)),
        compiler_params=_params("parallel"),
        name="mm_small",
    )(x, w)


def _ffn_up_body(x_ref, w1_ref, w3_ref, o_ref, w1b_ref, w3b_ref):
    @pl.when(pl.program_id(1) == 0)
    def _():
        w1b_ref[...] = w1_ref[...].astype(BF16)
        w3b_ref[...] = w3_ref[...].astype(BF16)

    x = x_ref[...]
    a = jnp.dot(x, w1b_ref[...], preferred_element_type=F32)
    b = jnp.dot(x, w3b_ref[...], preferred_element_type=F32)
    o_ref[...] = (a * jax.nn.sigmoid(a) * b).astype(o_ref.dtype)


def ffn_up(x, w1, w3, *, w_index=(), bm=1024, bn=256):
    M, K = x.shape
    N = w1.shape[-1]
    w_spec = pl.BlockSpec(_stacked(w_index) + (K, bn), lambda j, i: w_index + (0, j))
    return pl.pallas_call(
        _ffn_up_body,
        out_shape=jax.ShapeDtypeStruct((M, N), BF16),
        grid=(N // bn, M // bm),
        in_specs=[pl.BlockSpec((bm, K), lambda j, i: (i, 0)), w_spec, w_spec],
        out_specs=pl.BlockSpec((bm, bn), lambda j, i: (i, j)),
        scratch_shapes=[pltpu.VMEM((K, bn), BF16), pltpu.VMEM((K, bn), BF16)],
        compiler_params=_params("arbitrary", "arbitrary"),
        name="ffn_up",
    )(x, w1, w3)


def _ada_body(*refs, i_in):
    h_ref, g_ref = refs[:2]
    x_ref = refs[-1]
    h = h_ref[...]
    x = h * lax.rsqrt(jnp.mean(h * h, axis=-1, keepdims=True) + NORM_EPS) * g_ref[...]
    if len(refs) == 4:
        mi_ref = refs[2]
        x = x * (1.0 + mi_ref[0, 3 * i_in + 1:3 * i_in + 2, :]) + mi_ref[0, 3 * i_in:3 * i_in + 1, :]
    x_ref[...] = x.astype(x_ref.dtype)


def ada_step(h, g, *, mod_row, m_in=None, i_in=0, out_dtype=BF16):
    T, D = h.shape
    R = ROW_BLOCK
    row_spec = pl.BlockSpec((R, D), lambda i: (i, 0))
    args, specs = [h, g.reshape(1, D)], [row_spec, pl.BlockSpec((1, D), lambda i: (0, 0))]
    if m_in is not None:
        args.append(m_in)
        specs.append(pl.BlockSpec((1, N_MOD, D), lambda i: (mod_row(i, R), 0, 0)))
    return pl.pallas_call(
        functools.partial(_ada_body, i_in=i_in),
        out_shape=jax.ShapeDtypeStruct((T, D), out_dtype), grid=(T // R,), in_specs=specs, out_specs=row_spec,
        compiler_params=_params("parallel"), name="ada_step",
    )(*args)


def _head_sums(x, hseg, n_pass=2):
    outs = []
    for c in range(x.shape[1] // LANES):
        rem = x[:, c * LANES:(c + 1) * LANES]
        acc = None
        for _ in range(n_pass):
            hi = rem.astype(BF16)
            part = jnp.dot(hi, hseg, preferred_element_type=F32)
            acc = part if acc is None else acc + part
            rem = rem - hi.astype(F32)
        outs.append(acc)
    return outs[0] if len(outs) == 1 else jnp.concatenate(outs, axis=1)


def _head_matrix():
    lane = jnp.arange(LANES)
    return (lane[:, None] // B_HEAD == lane[None, :] // B_HEAD).astype(BF16)


_TAIL_OFF = dict(wdf=0, wdb=B_DECAY_LORA, adf=2 * B_DECAY_LORA, adb=2 * B_DECAY_LORA + B_AAA_LORA,
                 gd=2 * B_DECAY_LORA + 2 * B_AAA_LORA)


def _prep_body(r_ref, k_ref, v_ref, t_ref, rp_ref, kp_ref, vp_ref, tp_ref, rn_ref, kn_ref, vn_ref, tn_ref,
               mu_r_ref, mu_k_ref, mu_v_ref, mu_t_ref, w2f_ref, w2b_ref, a2f_ref, a2b_ref, g2_ref, rowp_ref, hseg_ref,
               wf_ref, bf_ref, kdf_ref, wb_ref, bb_ref, kdb_ref, ro_ref, kko_ref, vo_ref, go_ref, bonus_ref,
               *, seqs):
    R = ROW_BLOCK
    pos, nblk, _, _ = seqs.locate(pl.program_id(0), R)
    has_prev = (pos != 0).astype(F32)
    has_next = (pos != nblk - 1).astype(F32)

    def shifted(cur_ref, p_ref, n_ref, mu_ref):
        x = cur_ref[0]
        row = lax.broadcasted_iota(jnp.int32, x.shape, 0)
        prev = jnp.where(row == 0, p_ref[0, SUBLANES - 1:SUBLANES, :] * has_prev, pltpu.roll(x, 1, axis=0))
        nxt = jnp.where(row == R - 1, n_ref[0, 0:1, :] * has_next, pltpu.roll(x, R - 1, axis=0))
        return x + mu_ref[...] * (0.5 * (prev + nxt) - x)

    rr = shifted(r_ref, rp_ref, rn_ref, mu_r_ref)
    kr = shifted(k_ref, kp_ref, kn_ref, mu_k_ref)
    vr = shifted(v_ref, vp_ref, vn_ref, mu_v_ref)
    tail = shifted(t_ref, tp_ref, tn_ref, mu_t_ref)
    t_tanh = jnp.tanh(tail).astype(BF16)
    t_lin = tail.astype(BF16)
    t_sig = jax.nn.sigmoid(tail).astype(BF16)

    def dot(a, w_ref):
        return jnp.dot(a, w_ref[...], preferred_element_type=F32)

    w0f, w0b, a0f, a0b, k_k, k_a, r_k = [rowp_ref[i:i + 1, :] for i in range(7)]
    hseg = hseg_ref[...]
    kk = kr * k_k
    kk = kk / jnp.maximum(jnp.sqrt(_head_sums(kk * kk, hseg)), 1e-12)
    for w0, a0, w2_ref, a2_ref, w_out, b_out, kd_out in ((w0f, a0f, w2f_ref, a2f_ref, wf_ref, bf_ref, kdf_ref),
                                                        (w0b, a0b, w2b_ref, a2b_ref, wb_ref, bb_ref, kdb_ref)):
        logw = -math.exp(-0.5) * jax.nn.sigmoid(w0 + dot(t_tanh, w2_ref))
        a = jax.nn.sigmoid(a0 + dot(t_lin, a2_ref))
        w_out[0] = jnp.exp(logw)
        b_out[0] = kk * a
        kd_out[0] = kr * (1.0 + (a - 1.0) * k_a)
    ro_ref[0] = rr
    kko_ref[0] = kk
    vo_ref[0] = vr
    go_ref[0] = dot(t_sig, g2_ref)
    bonus_ref[0] = _head_sums(rr * kr * r_k, hseg) * vr


def rwkv_prep(z, ztail, seqs, p):
    T = z.shape[0]
    R, CW = ROW_BLOCK, PREP_COLS
    NB, n8 = T // R, T // SUBLANES
    per = R // SUBLANES
    zv, z8 = z.reshape(NB, R, -1), z.reshape(n8, SUBLANES, -1)
    tv, t8 = ztail.reshape(NB, R, B_TAIL), ztail.reshape(n8, SUBLANES, B_TAIL)
    base = [(EV_A_IN + i * B_WIDTH) // CW for i in range(3)]

    def cur(b):
        return pl.BlockSpec((1, R, CW), lambda i, c: (i, 0, b + c))

    def prev(b):
        return pl.BlockSpec((1, SUBLANES, CW), lambda i, c: (jnp.maximum(i * per - 1, 0), 0, b + c))

    def nxt(b):
        return pl.BlockSpec((1, SUBLANES, CW), lambda i, c: (jnp.minimum((i + 1) * per, n8 - 1), 0, b + c))

    t_cur = pl.BlockSpec((1, R, B_TAIL), lambda i, c: (i, 0, 0))
    t_prev = pl.BlockSpec((1, SUBLANES, B_TAIL), lambda i, c: (jnp.maximum(i * per - 1, 0), 0, 0))
    t_next = pl.BlockSpec((1, SUBLANES, B_TAIL), lambda i, c: (jnp.minimum((i + 1) * per, n8 - 1), 0, 0))

    mu = p['mu']
    mu_main = mu[:3 * B_WIDTH].reshape(1, 3 * B_WIDTH)
    mu_t = mu[3 * B_WIDTH:].reshape(1, B_TAIL)

    def mu_spec(i):
        return pl.BlockSpec((1, CW), lambda r, c: (0, i * (B_WIDTH // CW) + c))

    def padded(w, off):
        return jnp.zeros((B_TAIL, B_WIDTH), BF16).at[off:off + w.shape[0]].set(w.astype(BF16))

    lora = [padded(p['w2f'], _TAIL_OFF['wdf']), padded(p['w2b'], _TAIL_OFF['wdb']),
            padded(p['a2f'], _TAIL_OFF['adf']), padded(p['a2b'], _TAIL_OFF['adb']), padded(p['g2'], _TAIL_OFF['gd'])]
    lora_spec = pl.BlockSpec((B_TAIL, CW), lambda i, c: (0, c))
    rowp = jnp.stack([p['w0f'], p['w0b'], p['a0f'], p['a0b'], p['k_k'], p['k_a'], p['r_k'].reshape(-1),
                      jnp.zeros((B_WIDTH,), F32)])
    out_spec = pl.BlockSpec((1, R, CW), lambda i, c: (i, 0, c))
    names = ('w_f', 'b_f', 'kd_f', 'w_b', 'b_b', 'kd_b', 'r', 'kk', 'v', 'g', 'bonus')
    outs = pl.pallas_call(
        functools.partial(_prep_body, seqs=seqs),
        out_shape=tuple(jax.ShapeDtypeStruct((NB, R, B_WIDTH), F32) for _ in names),
        grid=(NB, B_WIDTH // CW),
        in_specs=[cur(base[0]), cur(base[1]), cur(base[2]), t_cur,
                  prev(base[0]), prev(base[1]), prev(base[2]), t_prev,
                  nxt(base[0]), nxt(base[1]), nxt(base[2]), t_next,
                  mu_spec(0), mu_spec(1), mu_spec(2), pl.BlockSpec((1, B_TAIL), lambda i, c: (0, 0)),
                  lora_spec, lora_spec, lora_spec, lora_spec, lora_spec,
                  pl.BlockSpec((SUBLANES, CW), lambda i, c: (0, c)),
                  pl.BlockSpec((LANES, LANES), lambda i, c: (0, 0))],
        out_specs=tuple(out_spec for _ in names),
        compiler_params=_params("parallel", "parallel"),
        name="rwkv_prep",
    )(zv, zv, zv, tv, z8, z8, z8, t8, z8, z8, z8, t8, mu_main, mu_main, mu_main, mu_t, *lora, rowp, _head_matrix())
    return dict(zip(names, outs))


def _rwkv_body(w_ref, b_ref, kd_ref, r_ref, kk_ref, v_ref, g_ref, h_ref, t0_ref, o_ref, tf_ref,
               s_ref, vb_ref, xo_ref, *, seqs, NB, reverse):
    row_refs = (w_ref, b_ref, kd_ref, r_ref, kk_ref)
    TB = REC_TIME_BLOCK
    P, R, UW = s_ref.shape
    jb = pl.program_id(0)
    jb = (NB - 1 - jb) if reverse else jb
    pos, nblk, _, is_ctx = seqs.locate(jb, TB)
    first = pos == ((nblk - 1) if reverse else 0)
    last = pos == (0 if reverse else (nblk - 1))

    @pl.when(first)
    def _():
        s_ref[...] = jnp.where(is_ctx, 0.0, t0_ref[0])

    G = g_ref[...][None]
    Gb = G.astype(BF16)
    Hm = h_ref[...]

    Q = REC_SUB_UNITS
    subs = [slice(q * Q, (q + 1) * Q) for q in range(P // Q)]

    def seg_sum(x):
        return jnp.dot(x.reshape(Q * R, UW).astype(BF16), Hm, preferred_element_type=F32).reshape(Q, R, UW)

    def unit_rows(tile, j, qs):
        return jnp.stack([tile[j:j + 1, p * UW:(p + 1) * UW] for p in range(qs.start, qs.stop)])

    sub_iota = lax.broadcasted_iota(jnp.int32, (SUBLANES, UW), 0)

    def group(i, carry):
        gi = (TB // SUBLANES - 1 - i) if reverse else i
        t8 = pl.multiple_of(gi * SUBLANES, SUBLANES)
        tiles = [r[0, pl.ds(t8, SUBLANES), :] for r in row_refs]
        v_tile_b = v_ref[0, pl.ds(t8, SUBLANES), :].astype(BF16)
        for j in range(SUBLANES):
            for qs in subs:
                vb_ref[j, qs] = seg_sum(Gb * unit_rows(v_tile_b, j, qs))
        for jj in range(SUBLANES):
            j = SUBLANES - 1 - jj if reverse else jj
            for qs in subs:
                w, b, kd, r, kk = [unit_rows(tile, j, qs) for tile in tiles]
                S = s_ref[qs]
                u = seg_sum(S * kk)
                S = S * w - u * b + vb_ref[j, qs] * kd
                s_ref[qs] = S
                xo_ref[j, qs] = S * r
        for qs in subs:
            o_tiles = [jnp.zeros((SUBLANES, UW), F32) for _ in range(Q)]
            for j in range(SUBLANES):
                orow = jnp.sum(G * seg_sum(xo_ref[j, qs]), axis=1, keepdims=True)
                o_tiles = [jnp.where(sub_iota == j, orow[p], o_tiles[p]) for p in range(Q)]
            for p in range(Q):
                lo = (qs.start + p) * UW
                o_ref[0, pl.ds(t8, SUBLANES), lo:lo + UW] = o_tiles[p]
        return carry

    lax.fori_loop(0, TB // SUBLANES, group, 0)

    @pl.when(last & is_ctx)
    def _():
        tf_ref[0] = s_ref[...]


def rwkv7(rows, v, s0, seqs, *, reverse):
    T = v.shape[0] * v.shape[1]
    TB, UW, seg = REC_TIME_BLOCK, LANES, B_HEAD
    NB = T // TB
    hpu = UW // seg
    P = B_HEADS // hpu
    n_seq = s0.shape[0]
    assert n_seq == seqs.Bl
    t0 = s0.reshape(n_seq, P, hpu, seg, seg).transpose(0, 1, 3, 2, 4).reshape(n_seq, P, seg, UW)
    lane = jnp.arange(UW)
    g = (lane[None, :] % seg == jnp.arange(seg)[:, None]).astype(F32)

    def blk(j):
        return (NB - 1 - j) if reverse else j

    row_spec = pl.BlockSpec((1, TB, B_WIDTH), lambda j: (blk(j), 0, 0))
    in_st = pl.BlockSpec((1, P, seg, UW), lambda j: (jnp.maximum(seqs.locate(blk(j), TB)[2] - seqs.Bc, 0), 0, 0, 0))
    out_st = pl.BlockSpec((1, P, seg, UW), lambda j: (jnp.minimum(seqs.locate(blk(j), TB)[2], seqs.Bc - 1), 0, 0, 0))
    o, s = pl.pallas_call(
        functools.partial(_rwkv_body, seqs=seqs, NB=NB, reverse=reverse),
        out_shape=(jax.ShapeDtypeStruct((NB, TB, B_WIDTH), F32),
                   jax.ShapeDtypeStruct((seqs.Bc,) + t0.shape[1:], F32)),
        grid=(NB,),
        in_specs=[row_spec] * 6 + [pl.BlockSpec((seg, UW), lambda j: (0, 0)),
                                   pl.BlockSpec((UW, UW), lambda j: (0, 0)), in_st],
        out_specs=(row_spec, out_st),
        scratch_shapes=[pltpu.VMEM((P, seg, UW), F32), pltpu.VMEM((SUBLANES, P, seg, UW), F32),
                        pltpu.VMEM((SUBLANES, P, seg, UW), F32)],
        compiler_params=_params("arbitrary"),
        name="rwkv7",
    )(*[x.reshape(NB, TB, B_WIDTH) for x in rows], v.reshape(NB, TB, B_WIDTH), g, _head_matrix(), t0)
    s = s.reshape(seqs.Bc, P, seg, hpu, seg).transpose(0, 1, 3, 2, 4).reshape((seqs.Bc,) + s0.shape[1:])
    return o.reshape(T, B_WIDTH), s


HG_STEP = 8


def _hgrn_body(q_ref, f_ref, v_ref, lb_ref, s0_ref, o_ref, sf_ref, s_ref, *, seqs, NB, Hh, reverse):
    C = HG_STEP
    TB = ROW_BLOCK
    jb = pl.program_id(1)
    jb = (NB - 1 - jb) if reverse else jb
    pos, nblk, _, is_ctx = seqs.locate(jb, TB)
    first = pos == ((nblk - 1) if reverse else 0)
    last = pos == (0 if reverse else (nblk - 1))

    @pl.when(first)
    def _():
        s_ref[...] = jnp.where(is_ctx, 0.0, s0_ref[0])

    row = lax.broadcasted_iota(jnp.int32, (C, LANES), 0)
    def shift(x, d):
        if d == 0:
            return x
        return pltpu.roll(x, (C - d) if reverse else d, axis=0)

    def has_source(d):
        return (row <= C - 1 - d) if reverse else (row >= d)

    def block(i, carry):
        ci = (TB // C - 1 - i) if reverse else i
        t0 = pl.multiple_of(ci * C, C)
        for h in range(Hh):
            sl = slice(h * LANES, (h + 1) * LANES)
            hq = q_ref[0, pl.ds(t0, C), sl]
            q = hq * jax.nn.sigmoid(hq) * A_DK ** -0.5
            lb = lb_ref[:, sl]
            f = lb + (1.0 - lb) * jax.nn.sigmoid(f_ref[0, pl.ds(t0, C), sl])
            v = v_ref[0, pl.ds(t0, C), sl]
            k = 1.0 - f
            b = jnp.log(f)
            s = 1
            while s < C:
                b = b + jnp.where(has_source(s), shift(b, s), 0.0)
                s *= 2
            st = s_ref[h]
            stb = st.astype(BF16)
            o = lax.dot_general((q * jnp.exp(b)).astype(BF16), stb, (((1,), (1,)), ((), ())),
                                preferred_element_type=F32)
            for d in range(C):
                arg = jnp.where(has_source(d), b - shift(b, d), -1e30)
                a = jnp.sum(q * shift(k, d) * jnp.exp(arg), axis=1, keepdims=True)
                o = o + a * shift(v, d)
            o_ref[0, pl.ds(t0, C), sl] = o
            end = 0 if reverse else C - 1
            bl = b[end:end + 1, :]
            kt = k * jnp.exp(bl - b)
            s_ref[h] = st * jnp.exp(bl) + lax.dot_general(v.astype(BF16), kt.astype(BF16), (((0,), (0,)), ((), ())),
                                                         preferred_element_type=F32)
        return carry

    lax.fori_loop(0, TB // C, block, 0)

    @pl.when(last & is_ctx)
    def _():
        sf_ref[0] = s_ref[...]


def hgrn2(zv, cols, lb, s0, seqs, *, reverse, heads_per_step=16):
    NB, TB, _ = zv.shape
    W = A_WIDTH
    H = W // LANES
    Hh = heads_per_step
    bw = Hh * LANES
    assert TB == ROW_BLOCK and H % Hh == 0 and all(c % bw == 0 for c in cols) and s0.shape[0] == seqs.Bl

    def blk(j):
        return (NB - 1 - j) if reverse else j

    def in_spec(col):
        return pl.BlockSpec((1, TB, bw), lambda u, j: (blk(j), 0, col // bw + u))

    in_st = pl.BlockSpec((1, Hh, LANES, LANES),
                         lambda u, j: (jnp.maximum(seqs.locate(blk(j), TB)[2] - seqs.Bc, 0), u, 0, 0))
    out_st = pl.BlockSpec((1, Hh, LANES, LANES),
                          lambda u, j: (jnp.minimum(seqs.locate(blk(j), TB)[2], seqs.Bc - 1), u, 0, 0))
    st0 = jnp.swapaxes(s0, -1, -2)
    o, s = pl.pallas_call(
        functools.partial(_hgrn_body, seqs=seqs, NB=NB, Hh=Hh, reverse=reverse),
        out_shape=(jax.ShapeDtypeStruct((NB, TB, W), F32), jax.ShapeDtypeStruct((seqs.Bc,) + st0.shape[1:], F32)),
        grid=(H // Hh, NB),
        in_specs=[in_spec(cols[0]), in_spec(cols[1]), in_spec(cols[2]),
                  pl.BlockSpec((1, bw), lambda u, j: (0, u)), in_st],
        out_specs=(pl.BlockSpec((1, TB, bw), lambda u, j: (blk(j), 0, u)), out_st),
        scratch_shapes=[pltpu.VMEM((Hh, LANES, LANES), F32)],
        compiler_params=_params("parallel", "arbitrary"),
        name="hgrn2",
    )(zv, zv, zv, lb, st0)
    return o.reshape(NB * TB, W), jnp.swapaxes(s, -1, -2)


def _even_post_body(oaf_ref, oab_ref, hg_ref, obf_ref, obb_ref, bonus_ref, g_ref, rowp_ref, hseg_ref, o_ref):
    hg_g, ln_g, ln_b = [rowp_ref[i:i + 1, :] for i in range(3)]
    oa = oaf_ref[0] + oab_ref[0]
    hg = hg_ref[0]
    oa = oa * lax.rsqrt(jnp.mean(oa * oa, axis=-1, keepdims=True) + NORM_EPS) * hg_g * (hg * jax.nn.sigmoid(hg))
    o_ref[0, :, :A_WIDTH] = oa.astype(o_ref.dtype)
    hseg = hseg_ref[...]
    ob = obf_ref[0] + obb_ref[0]
    cen = ob - _head_sums(ob, hseg) * (1.0 / B_HEAD)
    var = _head_sums(cen * cen, hseg) * (1.0 / B_HEAD)
    ob = (cen * lax.rsqrt(var + B_GN_EPS) * ln_g + ln_b + bonus_ref[0]) * g_ref[0]
    o_ref[0, :, A_WIDTH:] = ob.astype(o_ref.dtype)


def even_post(oa_f, oa_b, zv, ob_f, ob_b, bonus, g, p):
    NB, R, _ = zv.shape
    T = NB * R
    W = A_WIDTH
    assert W == B_WIDTH and (4 * A_WIDTH) % W == 0
    spec = pl.BlockSpec((1, R, W), lambda i: (i, 0, 0))
    rowp = jnp.stack([p['hg_g'], p['ln_g'], p['ln_b']] + [jnp.zeros((W,), F32)] * 5)
    v3 = lambda x: x.reshape(NB, R, W)
    out = pl.pallas_call(
        _even_post_body,
        out_shape=jax.ShapeDtypeStruct((NB, R, 2 * W), BF16),
        grid=(NB,),
        in_specs=[spec, spec, pl.BlockSpec((1, R, W), lambda i: (i, 0, 4 * A_WIDTH // W)), spec, spec, spec, spec,
                  pl.BlockSpec((SUBLANES, W), lambda i: (0, 0)), pl.BlockSpec((LANES, LANES), lambda i: (0, 0))],
        out_specs=pl.BlockSpec((1, R, 2 * W), lambda i: (i, 0, 0)),
        compiler_params=_params("parallel"),
        name="even_post",
    )(v3(oa_f), v3(oa_b), zv, v3(ob_f), v3(ob_b), bonus, g, rowp, _head_matrix())
    return out.reshape(T, 2 * W)


def even_mixer(z, ztail, seqs, st, p):
    T = z.shape[0]
    zv = z.reshape(T // ROW_BLOCK, ROW_BLOCK, -1)
    oa_f, sf = hgrn2(zv, (0, A_WIDTH, 3 * A_WIDTH), p['lb_f'][None], st['hf'], seqs, reverse=False)
    oa_b, sb = hgrn2(zv, (0, 2 * A_WIDTH, 3 * A_WIDTH), p['lb_b'][None], st['hb'], seqs, reverse=True)
    q = rwkv_prep(z, ztail, seqs, p)
    ob_f, rf = rwkv7((q['w_f'], q['b_f'], q['kd_f'], q['r'], q['kk']), q['v'], st['rf'], seqs, reverse=False)
    ob_b, rb = rwkv7((q['w_b'], q['b_b'], q['kd_b'], q['r'], q['kk']), q['v'], st['rb'], seqs, reverse=True)
    return even_post(oa_f, oa_b, zv, ob_f, ob_b, q['bonus'], q['g'], p), (sf, sb, rf, rb)


def _swap_halves(x):
    lane = lax.broadcasted_iota(jnp.int32, x.shape, 1)
    quarter = ROPE_AXIS_DIM // 2
    return jnp.where(lane % ROPE_AXIS_DIM < quarter, pltpu.roll(x, LANES - quarter, axis=1),
                     pltpu.roll(x, quarter, axis=1))


def _attn_body(*refs, has_ctx):
    if has_ctx:
        (q_ref, kn_ref, vn_ref, ck_ref, cv_ref, qc_ref, qs_ref, kc_ref, ks_ref, qg_ref, kg_ref,
         o_ref, k_sc, v_sc) = refs
    else:
        q_ref, kn_ref, vn_ref, qg_ref, kg_ref, o_ref, ko_ref, vo_ref, k_sc, v_sc = refs

    def norm(x, g_ref):
        return x * lax.rsqrt(jnp.mean(x * x, axis=-1, keepdims=True) + NORM_EPS) * g_ref[...]

    @pl.when(pl.program_id(2) == 0)
    def _():
        k = norm(kn_ref[0], kg_ref)
        v = vn_ref[0]
        if has_ctx:
            n_past = ck_ref.shape[1]
            k = k * kc_ref[...] + _swap_halves(k) * ks_ref[...]
            k_sc[:n_past] = ck_ref[0].astype(BF16)
            v_sc[:n_past] = cv_ref[0].astype(BF16)
            k_sc[n_past:] = k.astype(BF16)
            v_sc[n_past:] = v.astype(BF16)
        else:
            ko_ref[0] = k
            vo_ref[0] = v
            k_sc[...] = k.astype(BF16)
            v_sc[...] = v.astype(BF16)

    k = k_sc[...]
    v = v_sc[...]
    scale = C_HEAD_DIM ** -0.5
    for g in range(C_GROUP):
        sl = slice(g * C_HEAD_DIM, (g + 1) * C_HEAD_DIM)
        q = norm(q_ref[0, :, sl], qg_ref)
        if has_ctx:
            q = q * qc_ref[...] + _swap_halves(q) * qs_ref[...]
        s = lax.dot_general(q.astype(BF16), k, (((1,), (1,)), ((), ())), preferred_element_type=F32) * scale
        m = jnp.max(s, axis=-1, keepdims=True)
        e = jnp.exp(s - m)
        p = e / jnp.sum(e, axis=-1, keepdims=True)
        o_ref[0, :, sl] = jnp.dot(p.astype(BF16), v, preferred_element_type=F32).astype(o_ref.dtype)


def _rope_tables(L):
    n_rows = L // GRID_W
    row = jnp.repeat(jnp.arange(n_rows), GRID_W).astype(F32)
    col = jnp.tile(jnp.arange(GRID_W), n_rows).astype(F32)
    half = ROPE_AXIS_DIM // 2
    inv = ROPE_THETA ** (-jnp.arange(half, dtype=F32) / half)
    ar, ac = row[:, None] * inv, col[:, None] * inv
    cos_t = jnp.concatenate([jnp.cos(ar), jnp.cos(ar), jnp.cos(ac), jnp.cos(ac)], axis=1)
    sin_t = jnp.concatenate([-jnp.sin(ar), jnp.sin(ar), -jnp.sin(ac), jnp.sin(ac)], axis=1)
    return cos_t, sin_t


def attention(z, row0, B, L, qn_g, kn_g, ctx_k=None, ctx_v=None):
    T = z.shape[0]
    assert row0 % L == 0 and L % Q_BLOCK == 0
    has_ctx = ctx_k is not None
    gw = C_GROUP * C_HEAD_DIM
    zq = z.reshape(T // Q_BLOCK, Q_BLOCK, -1)
    zl = z.reshape(T // L, L, -1)
    qb0, lb0 = row0 // Q_BLOCK, row0 // L
    nq = L // Q_BLOCK
    kcol, vcol = C_WIDTH // C_HEAD_DIM, (C_WIDTH + C_KV_WIDTH) // C_HEAD_DIM
    g_spec = pl.BlockSpec((1, C_HEAD_DIM), lambda b, h, i: (0, 0))
    in_specs = [pl.BlockSpec((1, Q_BLOCK, gw), lambda b, h, i: (qb0 + b * nq + i, 0, h)),
                pl.BlockSpec((1, L, C_HEAD_DIM), lambda b, h, i: (lb0 + b, 0, kcol + h)),
                pl.BlockSpec((1, L, C_HEAD_DIM), lambda b, h, i: (lb0 + b, 0, vcol + h))]
    args = [zq, zl, zl]
    o_spec = pl.BlockSpec((1, Q_BLOCK, gw), lambda b, h, i: (b, i, h))
    o_shape = jax.ShapeDtypeStruct((B, L, C_WIDTH), BF16)
    if has_ctx:
        n_past = ctx_k.shape[1]
        c_spec = pl.BlockSpec((1, n_past, C_HEAD_DIM), lambda b, h, i: (b, 0, h))
        cos_t, sin_t = _rope_tables(L)
        q_tab = pl.BlockSpec((Q_BLOCK, C_HEAD_DIM), lambda b, h, i: (i, 0))
        k_tab = pl.BlockSpec((L, C_HEAD_DIM), lambda b, h, i: (0, 0))
        in_specs += [c_spec, c_spec, q_tab, q_tab, k_tab, k_tab]
        args += [ctx_k.reshape(B, n_past, C_KV_WIDTH), ctx_v.reshape(B, n_past, C_KV_WIDTH), cos_t, sin_t, cos_t, sin_t]
        out_shape, out_specs = o_shape, o_spec
        Lk = n_past + L
    else:
        kv_spec = pl.BlockSpec((1, L, C_HEAD_DIM), lambda b, h, i: (b, 0, h))
        kv_shape = jax.ShapeDtypeStruct((B, L, C_KV_WIDTH), F32)
        out_shape, out_specs = (o_shape, kv_shape, kv_shape), (o_spec, kv_spec, kv_spec)
        Lk = L
    in_specs += [g_spec, g_spec]
    args += [qn_g.reshape(1, C_HEAD_DIM), kn_g.reshape(1, C_HEAD_DIM)]
    res = pl.pallas_call(
        functools.partial(_attn_body, has_ctx=has_ctx),
        out_shape=out_shape, grid=(B, C_KV_HEADS, nq), in_specs=in_specs, out_specs=out_specs,
        scratch_shapes=[pltpu.VMEM((Lk, C_HEAD_DIM), BF16), pltpu.VMEM((Lk, C_HEAD_DIM), BF16)],
        compiler_params=_params("parallel", "parallel", "arbitrary"),
        name="attn",
    )(*args)
    if has_ctx:
        return res, None, None
    o, k, v = res
    return o, k.reshape(B, L, C_KV_HEADS, C_HEAD_DIM), v.reshape(B, L, C_KV_HEADS, C_HEAD_DIM)


def _dot3(ah, al, u):
    uh = u.astype(BF16)
    ul = (u - uh.astype(F32)).astype(BF16)
    return (jnp.dot(ah, uh, preferred_element_type=F32) + jnp.dot(ah, ul, preferred_element_type=F32)
            + jnp.dot(al, uh, preferred_element_type=F32))


def _conv3(x, w_ref, b_ref):
    L = x.shape[0]
    row = lax.broadcasted_iota(jnp.int32, x.shape, 0)
    prev = jnp.where(row == 0, 0.0, pltpu.roll(x, 1, axis=0))
    nxt = jnp.where(row == L - 1, 0.0, pltpu.roll(x, L - 1, axis=0))
    return prev * w_ref[0:1, :] + x * w_ref[1:2, :] + nxt * w_ref[2:3, :] + b_ref[...]


def _dft_fwd_body(*refs, conv):
    if conv:
        ah_ref, al_ref, u_ref, cw_ref, cb_ref, o_ref = refs
        u = _conv3(u_ref[0], cw_ref, cb_ref)
    else:
        ah_ref, al_ref, u_ref, o_ref = refs
        u = u_ref[0]
    o_ref[0] = _dot3(ah_ref[...], al_ref[...], u)


def _dft_inv_body(*refs, conv_u):
    if conv_u:
        ah_ref, al_ref, uf_ref, sc_ref, u_ref, cwu_ref, cbu_ref, x_ref, cwx_ref, cbx_ref, bias_ref, o_ref = refs
        u = _conv3(u_ref[0], cwu_ref, cbu_ref)
    else:
        ah_ref, al_ref, uf_ref, sc_ref, u_ref, x_ref, cwx_ref, cbx_ref, bias_ref, o_ref = refs
        u = u_ref[0]
    y = _dot3(ah_ref[...], al_ref[...], uf_ref[0] * sc_ref[...])
    gate = _conv3(x_ref[0], cwx_ref, cbx_ref)
    o_ref[0] = ((y + u * bias_ref[...]) * gate).astype(o_ref.dtype)


def dft_mm(a, u, *, bn=512):
    ah, al = a
    M, K = ah.shape
    B, _, C = u.shape
    a_spec = pl.BlockSpec((M, K), lambda b, j: (0, 0))
    return pl.pallas_call(
        functools.partial(_dft_fwd_body, conv=False),
        out_shape=jax.ShapeDtypeStruct((B, M, C), F32),
        grid=(B, C // bn),
        in_specs=[a_spec, a_spec, pl.BlockSpec((1, K, bn), lambda b, j: (b, 0, j))],
        out_specs=pl.BlockSpec((1, M, bn), lambda b, j: (b, 0, j)),
        compiler_params=_params("parallel", "parallel"),
        name="dft_mm",
    )(ah, al, u)


def hyena(z, row0, B, L, p):
    T = z.shape[0]
    N = 2 * L
    zl = z.reshape(T // L, L, -1)
    lb0 = row0 // L
    filt = _hyena_filters(L, p['f_w1'], p['f_b1'], p['f_w2'], p['f_b2'], p['f_w3'], p['f_b3'], p['f_w4'],
                          p['sin_freq'])
    fmat = _dft_matrix(L)
    fwd, inv = _split_bf16(fmat), _split_bf16(fmat.T)
    h_all = filt.reshape(L, HY_ORDER * HY_W)
    kf = dft_mm(fwd, jnp.concatenate([h_all[:1], 2.0 * h_all[1:]], axis=0)[None])[0]
    r = jnp.arange(N)
    herm = jnp.where((r == 0) | (r == L), 1.0, 2.0).astype(F32) / N
    scale = (kf[jnp.where(r <= L, r, r - L)] * herm[:, None]).reshape(N, HY_ORDER, HY_W)
    conv_w, conv_b = p['conv_w'], p['conv_b'].reshape(1, -1)
    bias = p['hy_bias']

    def z_spec(piece, bn):
        c0 = (C_IN + piece * HY_W) // bn
        return pl.BlockSpec((1, L, bn), lambda b, j: (lb0 + b, 0, c0 + j))

    def cw_specs(piece, bn):
        c0 = piece * HY_W // bn
        return [pl.BlockSpec((3, bn), lambda b, j: (0, c0 + j)), pl.BlockSpec((1, bn), lambda b, j: (0, c0 + j))]

    bn_max = math.gcd(C_IN, HY_W)
    bn_fwd = min(bn_max, HYENA_FWD_BLOCK // L)
    bn_inv = min(bn_max, HYENA_INV_BLOCK // L)

    def forward(u, conv, bn=bn_fwd):
        a_spec = pl.BlockSpec((N, L), lambda b, j: (0, 0))
        if conv:
            in_specs, args = [z_spec(0, bn)] + cw_specs(0, bn), [zl, conv_w, conv_b]
        else:
            in_specs, args = [pl.BlockSpec((1, L, bn), lambda b, j: (b, 0, j))], [u]
        return pl.pallas_call(
            functools.partial(_dft_fwd_body, conv=conv),
            out_shape=jax.ShapeDtypeStruct((B, N, HY_W), F32), grid=(B, HY_W // bn),
            in_specs=[a_spec, a_spec] + in_specs, out_specs=pl.BlockSpec((1, N, bn), lambda b, j: (b, 0, j)),
            compiler_params=_params("parallel", "parallel"), name="hyena_fwd",
        )(*fwd, *args)

    def inverse(uf, order, u, gate_piece, out_dtype, bn=bn_inv):
        a_spec = pl.BlockSpec((L, N), lambda b, j: (0, 0))
        blk = pl.BlockSpec((1, L, bn), lambda b, j: (b, 0, j))
        conv_u = u is None
        in_specs = [a_spec, a_spec, pl.BlockSpec((1, N, bn), lambda b, j: (b, 0, j)),
                    pl.BlockSpec((N, bn), lambda b, j: (0, j))]
        args = [*inv, uf, scale[:, order]]
        if conv_u:
            in_specs += [z_spec(0, bn)] + cw_specs(0, bn)
            args += [zl, conv_w, conv_b]
        else:
            in_specs.append(blk)
            args.append(u)
        in_specs += [z_spec(gate_piece, bn)] + cw_specs(gate_piece, bn) + [pl.BlockSpec((1, bn), lambda b, j: (0, j))]
        args += [zl, conv_w, conv_b, bias[order][None]]
        return pl.pallas_call(
            functools.partial(_dft_inv_body, conv_u=conv_u),
            out_shape=jax.ShapeDtypeStruct((B, L, HY_W), out_dtype), grid=(B, HY_W // bn),
            in_specs=in_specs, out_specs=blk,
            compiler_params=_params("parallel", "parallel"), name="hyena_inv",
        )(*args)

    z1 = inverse(forward(None, True), 0, None, 1, F32)
    return inverse(forward(z1, False), 1, z1, 2, BF16)


def _split_bf16(a):
    hi = a.astype(BF16)
    return hi, (a - hi.astype(F32)).astype(BF16)


def _dft_matrix(L):
    N = 2 * L
    r = jnp.arange(N, dtype=jnp.int32)[:, None]
    s = jnp.arange(L, dtype=jnp.int32)[None, :]
    is_cos = r <= L
    f = jnp.where(is_cos, r, r - L)
    ang = (2.0 * math.pi / N) * ((f * s) % N).astype(F32)
    return jnp.where(is_cos, jnp.cos(ang), jnp.sin(ang))


def _hyena_filters(L, w1, b1, w2, b2, w3, b3, w4, freq):
    t = jnp.linspace(0.0, 1.0, L, dtype=F32)[:, None]
    w_pos = 2.0 * math.pi * jnp.arange(L, dtype=F32)[:, None] / L
    bands = jnp.linspace(1e-4, HY_BANDS - 1, HY_BANDS, dtype=F32)[None]
    zpos = jnp.concatenate([t, jnp.cos(bands * w_pos), -jnp.sin(bands * w_pos)], axis=-1)
    f = freq.astype(F32)
    hdn = jnp.sin(f * (zpos @ w1 + b1))
    hdn = jnp.sin(f * (hdn @ w2 + b2))
    hdn = jnp.sin(f * (hdn @ w3 + b3))
    h = mm(hdn.astype(BF16), w4.astype(BF16), bm=L, bn=1024)
    deltas = jnp.abs(jnp.linspace(HY_MIN_DECAY, HY_MAX_DECAY, HY_ORDER * HY_W, dtype=F32))
    h = h * jnp.exp(-t * deltas)
    return h.reshape(L, HY_ORDER, HY_W)


def _odd_core(z, row0, B, L, ctx_k, ctx_v, p):
    o_c, k, v = attention(z, row0, B, L, p['qn_g'], p['kn_g'], ctx_k, ctx_v)
    return jnp.concatenate([o_c, hyena(z, row0, B, L, p)], axis=-1), k, v


def kernel(x_prompt, x_sample, c, state_hgrn_fwd, state_hgrn_bwd, state_rwkv_fwd, state_rwkv_bwd, cache_k, cache_v, c_ctx, ada_w, ada_b, norm_g, ffn_w1, ffn_w3, ffn_w2, final_norm_g, ev_w_in, ev_w_out, hg_lb_fwd, hg_lb_bwd, hg_norm_g, rw_mu, rw_w0_f, rw_w2_f, rw_w0_b, rw_w2_b, rw_a0_f, rw_a2_f, rw_a0_b, rw_a2_b, rw_g2, rw_kk, rw_ka, rw_rk, rw_ln_g, rw_ln_b, od_w_in, od_w_out, at_qn_g, at_kn_g, hy_conv_w, hy_conv_b, hy_f_w1, hy_f_b1, hy_f_w2, hy_f_b2, hy_f_w3, hy_f_b3, hy_f_w4, hy_sin_freq, hy_bias):
    Bc, Lc, _ = x_prompt.shape
    Bl, Ll, _ = x_sample.shape
    seqs = Seqs(Bc, Lc, Bl, Ll)
    n_ctx_tok = Bc * Lc
    n_tok = seqs.n_tok
    assert Lc % ROW_BLOCK == 0 and Ll % ROW_BLOCK == 0 and n_ctx_tok % Ll == 0

    def mod_row(i, rows):
        return jnp.maximum(i * rows // Ll - (n_ctx_tok // Ll - 1), 0)

    h = jnp.concatenate([x_prompt.reshape(n_ctx_tok, D_MODEL), x_sample.reshape(Bl * Ll, D_MODEL)], axis=0)
    cvec = jnp.concatenate([c_ctx[None], c], axis=0)
    cvec = jnp.pad(jax.nn.silu(cvec), ((0, SUBLANES - (1 + Bl)), (0, 0))).astype(BF16)
    mods = [(mm_small(cvec, ada_w, bn=1024, w_index=(l,)) + ada_b[l])[:1 + Bl].reshape(1 + Bl, N_MOD, D_MODEL)
            for l in range(DEPTH)]

    lb_fwd_all = jnp.cumsum(jax.nn.softmax(hg_lb_fwd, axis=0)[1:], axis=0)
    lb_bwd_all = jnp.cumsum(jax.nn.softmax(hg_lb_bwd, axis=0)[1:], axis=0)

    ffn_w2_bf16 = ffn_w2.astype(BF16)

    def ffn(x, l, s, resid):
        mid = ffn_up(x, ffn_w1, ffn_w3, w_index=(l, s))
        return mm(mid, ffn_w2_bf16, bm=1024, bn=512, bk=D_FF // 2, w_index=(l, s), resid=resid)

    new_hf, new_hb, new_rf, new_rb, new_k, new_v = [], [], [], [], [], []
    x = ada_step(h, norm_g[0, 0], mod_row=mod_row, m_in=mods[0], i_in=0)
    for l in range(DEPTH):
        m = mods[l]
        h = ffn(x, l, 0, Resid(h, m, mod_row, 0, 0.5))
        xn = ada_step(h, norm_g[l, 1], mod_row=mod_row, m_in=m, i_in=1)
        j = l // 2
        if l % 2 == 0:
            p = dict(lb_f=lb_fwd_all[j], lb_b=lb_bwd_all[j], hg_g=hg_norm_g[j], mu=rw_mu[j],
                     w0f=rw_w0_f[j], w2f=rw_w2_f[j], w0b=rw_w0_b[j], w2b=rw_w2_b[j],
                     a0f=rw_a0_f[j], a2f=rw_a2_f[j], a0b=rw_a0_b[j], a2b=rw_a2_b[j], g2=rw_g2[j],
                     k_k=rw_kk[j], k_a=rw_ka[j], r_k=rw_rk[j], ln_g=rw_ln_g[j], ln_b=rw_ln_b[j])
            st = dict(hf=state_hgrn_fwd[:, j], hb=state_hgrn_bwd[:, j], rf=state_rwkv_fwd[:, j],
                      rb=state_rwkv_bwd[:, j])
            z = mm_ws(xn, ev_w_in, bm=1024, bn=512, w_index=(j,), n_cols=EV_MAIN)
            ztail = mm_ws(xn, ev_w_in[j, :, EV_MAIN:], bm=1024, bn=B_TAIL)
            o, (s_hf, s_hb, s_rf, s_rb) = even_mixer(z, ztail, seqs, st, p)
            new_hf.append(s_hf)
            new_hb.append(s_hb)
            new_rf.append(s_rf)
            new_rb.append(s_rb)
            w_out = ev_w_out
        else:
            p = dict(qn_g=at_qn_g[j], kn_g=at_kn_g[j], conv_w=hy_conv_w[j], conv_b=hy_conv_b[j],
                     f_w1=hy_f_w1[j], f_b1=hy_f_b1[j], f_w2=hy_f_w2[j], f_b2=hy_f_b2[j], f_w3=hy_f_w3[j],
                     f_b3=hy_f_b3[j], f_w4=hy_f_w4[j], sin_freq=hy_sin_freq[j], hy_bias=hy_bias[j])
            z = mm_ws(xn, od_w_in, bm=1024, bn=512, w_index=(j,))
            o_ctx, k_ctx, v_ctx = _odd_core(z, 0, Bc, Lc, None, None, p)
            o_lat, _, _ = _odd_core(z, n_ctx_tok, Bl, Ll, cache_k[:, j], cache_v[:, j], p)
            new_k.append(k_ctx)
            new_v.append(v_ctx)
            o = jnp.concatenate([o_ctx.reshape(n_ctx_tok, -1), o_lat.reshape(Bl * Ll, -1)], axis=0)
            w_out = od_w_out
        h = mm_ws(o, w_out, bm=1024, bn=512, w_index=(j,), resid=Resid(h, m, mod_row, 1, 1.0))
        x = ada_step(h, norm_g[l, 2], mod_row=mod_row, m_in=m, i_in=2)
        h = ffn(x, l, 1, Resid(h, m, mod_row, 2, 0.5))
        if l + 1 < DEPTH:
            x = ada_step(h, norm_g[l + 1, 0], mod_row=mod_row, m_in=mods[l + 1], i_in=0)
    y_out = ada_step(h, final_norm_g, mod_row=mod_row, out_dtype=F32)
    return (y_out[:n_ctx_tok].reshape(Bc, Lc, D_MODEL), y_out[n_ctx_tok:].reshape(Bl, Ll, D_MODEL),
            jnp.stack(new_hf, axis=1), jnp.stack(new_hb, axis=1), jnp.stack(new_rf, axis=1),
            jnp.stack(new_rb, axis=1), jnp.stack(new_k, axis=1), jnp.stack(new_v, axis=1))
```
